```python
import jax, jax.numpy as jnp
from jax import lax
import numpy as np

D_MODEL = 4096
BATCH = 4
SEQ = 4096
DEPTH = 1

EXPAND = 2
D_MIX = EXPAND * D_MODEL
D_ATTN = D_MIX // 2
D_RNN = D_MIX - D_ATTN
ATTN_HEAD_DIM = 64
N_Q_HEADS = D_ATTN // ATTN_HEAD_DIM
N_KV_HEADS = N_Q_HEADS // 8
GQA_GROUP = N_Q_HEADS // N_KV_HEADS
D_KV = N_KV_HEADS * ATTN_HEAD_DIM
WINDOW = 128
RNN_HEAD_DIM = 128
N_RNN_HEADS = D_RNN // RNN_HEAD_DIM
CHUNK = 64
NORM_EPS = 1e-6

COL_SIZES = [D_ATTN, D_KV, D_KV, D_ATTN, D_RNN, D_RNN, D_RNN, D_RNN]
D_IN = int(sum(COL_SIZES))
SPLIT_POINTS = [int(c) for c in np.cumsum(COL_SIZES)[:-1]]

kernel_name = "hymba_swa_sink_hgrn2_sandwich"


def rms_norm(x, gain):
    xf = x.astype(jnp.float32)
    y = xf * lax.rsqrt(jnp.mean(xf * xf, axis=-1, keepdims=True) + NORM_EPS)
    return (y * gain.astype(jnp.float32)).astype(x.dtype)


def sliding_window_attention(q, k, v, sinks):
    B, S = q.shape[0], q.shape[1]
    nb = S // WINDOW
    q = q.reshape(B, nb, WINDOW, N_KV_HEADS, GQA_GROUP, ATTN_HEAD_DIM)
    k = k.reshape(B, nb, WINDOW, N_KV_HEADS, ATTN_HEAD_DIM)
    v = v.reshape(B, nb, WINDOW, N_KV_HEADS, ATTN_HEAD_DIM)
    pad = ((0, 0), (1, 0), (0, 0), (0, 0), (0, 0))
    kk = jnp.concatenate([jnp.pad(k, pad)[:, :-1], k], axis=2)
    vv = jnp.concatenate([jnp.pad(v, pad)[:, :-1], v], axis=2)
    scale = ATTN_HEAD_DIM ** -0.5
    scores = jnp.einsum('bnqhgd,bnkhd->bnhgqk', q, kk).astype(jnp.float32) * scale
    qi = jnp.arange(WINDOW)[:, None]
    kj = jnp.arange(2 * WINDOW)[None, :]
    band = (kj > qi) & (kj <= qi + WINDOW)
    blk = jnp.arange(nb)[:, None, None]
    valid = band[None] & ((blk > 0) | (kj[None] >= WINDOW))
    scores = jnp.where(valid[None, :, None, None], scores, -jnp.inf)
    sink = sinks.astype(jnp.float32).reshape(N_KV_HEADS, GQA_GROUP)[None, None, :, :, None, None]
    m = jnp.maximum(jnp.max(scores, axis=-1, keepdims=True), sink)
    p = jnp.exp(scores - m)
    denom = jnp.sum(p, axis=-1, keepdims=True) + jnp.exp(sink - m)
    probs = (p / denom).astype(v.dtype)
    out = jnp.einsum('bnhgqk,bnkhd->bnqhgd', probs, vv)
    return out.reshape(B, S, D_ATTN)


def hgrn2_recurrence(q, k, v, g):
    B, S, H, dk = q.shape
    dv = v.shape[-1]
    nc = S // CHUNK

    def to_chunks(t):
        return t.astype(jnp.float32).reshape(B, nc, CHUNK, H, t.shape[-1]).transpose(1, 0, 3, 2, 4)

    qc, kc, vc, gc = to_chunks(q), to_chunks(k), to_chunks(v), to_chunks(g)
    causal = jnp.tril(jnp.ones((CHUNK, CHUNK), dtype=bool))

    def step(state, inp):
        qb, kb, vb, gb = inp
        G = jnp.cumsum(gb, axis=2)
        inter = jnp.einsum('bhtd,bhde->bhte', qb * jnp.exp(G), state)
        diff = G[:, :, :, None, :] - G[:, :, None, :, :]
        decay = jnp.exp(jnp.where(causal[:, :, None], diff, -jnp.inf))
        attn = jnp.einsum('bhtd,bhsd,bhtsd->bhts', qb, kb, decay)
        intra = jnp.einsum('bhts,bhse->bhte', attn, vb)
        G_last = G[:, :, -1:, :]
        new_state = (jnp.exp(G_last[:, :, 0, :])[..., None] * state
                     + jnp.einsum('bhsd,bhse->bhde', kb * jnp.exp(G_last - G), vb))
        return new_state, inter + intra

    init = jnp.zeros((B, H, dk, dv), jnp.float32)
    _, out = lax.scan(step, init, (qc, kc, vc, gc))
    return out.transpose(1, 0, 3, 2, 4).reshape(B, S, H, dv)


def setup_inputs(seed: int = 0) -> dict:
    key = jax.random.key(seed)
    ks = jax.random.split(key, 8)
    x = jax.random.normal(ks[0], (BATCH, SEQ, D_MODEL), jnp.float32)
    w_in = jax.random.normal(ks[1], (DEPTH, D_MODEL, D_IN), jnp.float32) * D_MODEL ** -0.5
    attn_sinks = jax.random.normal(ks[2], (DEPTH, N_Q_HEADS), jnp.float32)
    lb_logits = 0.5 * jax.random.normal(ks[3], (DEPTH + 1, D_RNN), jnp.float32)
    rnn_norm = 1.0 + 0.1 * jax.random.normal(ks[4], (DEPTH, D_RNN), jnp.float32)
    w_out = jax.random.normal(ks[5], (DEPTH, D_MIX, D_MODEL), jnp.float32) * D_MIX ** -0.5
    pre_norm = 1.0 + 0.1 * jax.random.normal(ks[6], (DEPTH, D_MODEL), jnp.float32)
    post_norm = 1.0 + 0.1 * jax.random.normal(ks[7], (DEPTH, D_MODEL), jnp.float32)
    return {"x": x, "w_in": w_in, "attn_sinks": attn_sinks, "lb_logits": lb_logits,
            "rnn_norm": rnn_norm, "w_out": w_out, "pre_norm": pre_norm, "post_norm": post_norm}


def reference(x, w_in, attn_sinks, lb_logits, rnn_norm, w_out, pre_norm, post_norm):
    B, S, _ = x.shape
    lb_table = jnp.cumsum(jax.nn.softmax(lb_logits.astype(jnp.float32), axis=0), axis=0)
    for layer in range(DEPTH):
        h = rms_norm(x, pre_norm[layer])
        proj = jnp.einsum('bsd,de->bse', h, w_in[layer])
        aq, ak, av, ag, rq, rf, ri, rg = jnp.split(proj, SPLIT_POINTS, axis=-1)

        attn = sliding_window_attention(
            aq.reshape(B, S, N_Q_HEADS, ATTN_HEAD_DIM),
            ak.reshape(B, S, N_KV_HEADS, ATTN_HEAD_DIM),
            av.reshape(B, S, N_KV_HEADS, ATTN_HEAD_DIM),
            attn_sinks[layer])
        attn = attn * jax.nn.silu(ag)

        lb = lb_table[layer]
        f = lb + (1.0 - lb) * jax.nn.sigmoid(rf.astype(jnp.float32))
        g = jnp.log(f)
        k = 1.0 - f
        q = jax.nn.silu(rq)
        shp = (B, S, N_RNN_HEADS, RNN_HEAD_DIM)
        o = hgrn2_recurrence(q.reshape(shp), k.reshape(shp), ri.reshape(shp), g.reshape(shp))
        o = rms_norm(o, rnn_norm[layer].reshape(N_RNN_HEADS, RNN_HEAD_DIM))
        o = o.reshape(B, S, D_RNN).astype(x.dtype) * jax.nn.silu(rg)

        mixed = jnp.concatenate([attn, o], axis=-1)
        y = jnp.einsum('bse,ed->bsd', mixed, w_out[layer])
        x = x + rms_norm(y, post_norm[layer])
    return x
```

```python
import functools

import jax
import jax.numpy as jnp
import numpy as np
from jax import lax
from jax.experimental import pallas as pl
from jax.experimental.pallas import tpu as pltpu

ATTN_HEAD_DIM = 64
GQA_GROUP = 8
WINDOW = 128
RNN_HEAD_DIM = 128
NORM_EPS = 1e-6

LANES = 128
CHUNK = 128
N_LEVELS = 7
VMEM_LIMIT = 56 * 1024 * 1024
OUT_COL_CHUNK = 1024
OUT_ROW_CHUNK = 32

_NT = (((1,), (1,)), ((), ()))
_TN = (((0,), (0,)), ((), ()))


def _silu(x):
    return x / (1.0 + jnp.exp(-x))


def _prenorm_kernel(x_ref, g_ref, o_ref):
    x = x_ref[...]
    ms = jnp.mean(x * x, axis=-1, keepdims=True)
    o_ref[...] = (x * lax.rsqrt(ms + NORM_EPS) * g_ref[...]).astype(o_ref.dtype)


def _prenorm(x2d, gain, rows=256):
    t, d = x2d.shape
    rows = min(rows, t)
    return pl.pallas_call(
        _prenorm_kernel,
        out_shape=jax.ShapeDtypeStruct((t, d), jnp.bfloat16),
        grid=(t // rows,),
        in_specs=[pl.BlockSpec((rows, d), lambda i: (i, 0)),
                  pl.BlockSpec((1, d), lambda i: (0, 0))],
        out_specs=pl.BlockSpec((rows, d), lambda i: (i, 0)),
        compiler_params=pltpu.CompilerParams(
            dimension_semantics=("parallel",), vmem_limit_bytes=VMEM_LIMIT),
        name="prenorm",
    )(x2d, gain.reshape(1, d))


def _in_proj_kernel(h_ref, w_ref, o_ref):
    o_ref[...] = jnp.dot(h_ref[...], w_ref[...],
                         preferred_element_type=jnp.float32).astype(o_ref.dtype)


def _in_proj(h, w, tm=1024, tn=1024):
    t, d = h.shape
    n = w.shape[1]
    tm = min(tm, t)
    return pl.pallas_call(
        _in_proj_kernel,
        out_shape=jax.ShapeDtypeStruct((t, n), jnp.bfloat16),
        grid=(t // tm, n // tn),
        in_specs=[pl.BlockSpec((tm, d), lambda i, j: (i, 0)),
                  pl.BlockSpec((d, tn), lambda i, j: (0, j))],
        out_specs=pl.BlockSpec((tm, tn), lambda i, j: (i, j)),
        compiler_params=pltpu.CompilerParams(
            dimension_semantics=("parallel", "parallel"), vmem_limit_bytes=VMEM_LIMIT),
        name="in_proj",
    )(h, w)


def _lane_halves(slab, head_in_high_half):
    lane = lax.broadcasted_iota(jnp.int32, slab.shape, 1)
    swapped = pltpu.roll(slab, 64, axis=1)
    zero = jnp.zeros_like(slab)
    if head_in_high_half:
        lo, hi = swapped, slab
    else:
        lo, hi = slab, swapped
    return jnp.where(lane < 64, lo, zero), jnp.where(lane >= 64, hi, zero)


def _swa_kernel(sink_ref, q_ref, kvp_ref, kvc_ref, g0_ref, g1_ref, g2_ref, g3_ref, o_ref,
                *, n_kv_heads, d_kv):
    n = pl.program_id(1)
    gate_refs = (g0_ref, g1_ref, g2_ref, g3_ref)
    w = WINDOW
    qi = lax.broadcasted_iota(jnp.int32, (w, 2 * w), 0)
    kj = lax.broadcasted_iota(jnp.int32, (w, 2 * w), 1)
    first_valid = jnp.where(n > 0, 0, w)
    valid = (kj > qi) & (kj <= qi + w) & (kj >= first_valid)
    lane = lax.broadcasted_iota(jnp.int32, (w, LANES), 1)
    scale = ATTN_HEAD_DIM ** -0.5
    pairs_per_kv = GQA_GROUP // 2

    for h in range(n_kv_heads):
        c0 = (h // 2) * LANES
        high = (h % 2) == 1
        k_slab = jnp.concatenate([kvp_ref[:, c0:c0 + LANES], kvc_ref[:, c0:c0 + LANES]],
                                 axis=0).astype(jnp.float32)
        v_slab = jnp.concatenate([kvp_ref[:, d_kv + c0:d_kv + c0 + LANES],
                                  kvc_ref[:, d_kv + c0:d_kv + c0 + LANES]],
                                 axis=0).astype(jnp.float32)
        k_lo, k_hi = _lane_halves(k_slab * scale, high)
        v_lo, v_hi = _lane_halves(v_slab, high)
        kcat = jnp.concatenate([k_lo, k_hi], axis=0).astype(jnp.bfloat16)
        vcat = jnp.concatenate([v_lo, v_hi], axis=0).astype(jnp.bfloat16)

        for pp in range(pairs_per_kv):
            p = h * pairs_per_kv + pp
            qp = q_ref[:, p * LANES:(p + 1) * LANES]
            s = lax.dot_general(qp, kcat, _NT, preferred_element_type=jnp.float32)
            probs, inv = [], []
            for hh in range(2):
                sink = sink_ref[2 * p + hh]
                sh = jnp.where(valid, s[:, hh * 2 * w:(hh + 1) * 2 * w], -jnp.inf)
                m = jnp.maximum(jnp.max(sh, axis=-1, keepdims=True), sink)
                e = jnp.exp(sh - m)
                denom = jnp.sum(e, axis=-1, keepdims=True) + jnp.exp(sink - m)
                probs.append(e.astype(jnp.bfloat16))
                inv.append(1.0 / denom)
            pcat = jnp.concatenate(probs, axis=1)
            o = jnp.dot(pcat, vcat, preferred_element_type=jnp.float32)
            o = o * jnp.where(lane < 64, inv[0], inv[1])
            gref = gate_refs[p // 8]
            gate = gref[:, (p % 8) * LANES:(p % 8 + 1) * LANES].astype(jnp.float32)
            o_ref[:, p * LANES:(p + 1) * LANES] = (o * _silu(gate)).astype(o_ref.dtype)


def _swa(proj, sinks, batch, seq, d_attn, d_mix):
    t = proj.shape[0]
    w = WINDOW
    nb = seq // w
    n_q = d_attn // ATTN_HEAD_DIM
    n_kv = n_q // GQA_GROUP
    d_kv = n_kv * ATTN_HEAD_DIM
    kvw = 2 * d_kv
    assert d_attn % kvw == 0 and d_attn == 4 * kvw and (d_attn + kvw) % kvw == 0
    kv_blk = d_attn // kvw
    gate_blk0 = (d_attn + kvw) // kvw

    def row(b, n):
        return b * nb + n

    in_specs = [
        pl.BlockSpec(memory_space=pltpu.SMEM),
        pl.BlockSpec((w, d_attn), lambda b, n: (row(b, n), 0)),
        pl.BlockSpec((w, kvw), lambda b, n: (row(b, jnp.maximum(n - 1, 0)), kv_blk)),
        pl.BlockSpec((w, kvw), lambda b, n: (row(b, n), kv_blk)),
    ] + [
        pl.BlockSpec((w, kvw), functools.partial(lambda b, n, j: (row(b, n), gate_blk0 + j), j=j))
        for j in range(4)
    ]
    return pl.pallas_call(
        functools.partial(_swa_kernel, n_kv_heads=n_kv, d_kv=d_kv),
        out_shape=jax.ShapeDtypeStruct((t, d_mix), jnp.bfloat16),
        grid=(batch, nb),
        in_specs=in_specs,
        out_specs=pl.BlockSpec((w, d_attn), lambda b, n: (row(b, n), 0)),
        compiler_params=pltpu.CompilerParams(
            dimension_semantics=("parallel", "parallel"), vmem_limit_bytes=VMEM_LIMIT),
        name="swa",
    )(sinks, proj, proj, proj, proj, proj, proj, proj)


def _decay_tables():
    c = CHUNK
    n = np.zeros((N_LEVELS + 2, c, c), np.float32)
    r = np.arange(c)
    for l in range(N_LEVELS):
        half = 1 << l
        m = 2 * half
        for t in range(c):
            start = (t // m) * m
            mid = start + half - 1
            if t > mid:
                n[l, t] = (r > mid) & (r <= t)
            else:
                n[l, t] = (r > t) & (r <= mid)
    n[N_LEVELS] = r[None, :] <= r[:, None]
    n[N_LEVELS + 1] = r[None, :] > r[:, None]
    n = n.reshape((N_LEVELS + 2) * c, c)
    ncat = np.concatenate([n, n], axis=1)
    tt, ss = np.meshgrid(r, r, indexing="ij")
    x = tt ^ ss
    level = np.full((c, c), -1, np.int32)
    for l in range(N_LEVELS):
        level[(ss < tt) & (x >= (1 << l)) & (x < (2 << l))] = l
    return ncat, level


def _hgrn2_kernel(ncat_ref, level_ref, lbl_ref, gain_ref, rq_ref, rf_ref, ri_ref, rg_ref,
                  mixed_in_ref, o_ref, state_ref, *, heads, chunks, layer):
    del mixed_in_ref
    c = CHUNK

    @pl.when(pl.program_id(2) == 0)
    def _():
        state_ref[...] = jnp.zeros_like(state_ref)

    lbl = lbl_ref[...]
    e = jnp.exp(lbl - jnp.max(lbl, axis=0, keepdims=True))
    lb_all = jnp.sum(e[:layer + 1], axis=0, keepdims=True) / jnp.sum(e, axis=0, keepdims=True)
    level = level_ref[...]
    ncat = ncat_ref[...]

    def chunk_body(ci, carry):
        r0 = pl.multiple_of(ci * c, c)
        rows = pl.ds(r0, c)
        for hh in range(heads):
            cols = slice(hh * LANES, (hh + 1) * LANES)
            lb = lb_all[:, cols]
            rf = rf_ref[rows, cols].astype(jnp.float32)
            f = lb + (1.0 - lb) * (1.0 / (1.0 + jnp.exp(-rf)))
            g = jnp.log(f)
            k = 1.0 - f
            q = _silu(rq_ref[rows, cols].astype(jnp.float32))
            v = ri_ref[rows, cols].astype(jnp.float32)
            v_bf = v.astype(jnp.bfloat16)

            g_hi = g.astype(jnp.bfloat16)
            g_lo = (g - g_hi.astype(jnp.float32)).astype(jnp.bfloat16)
            gcat = jnp.concatenate([g_hi, g_lo], axis=0)
            ex = jnp.dot(ncat, gcat, preferred_element_type=jnp.float32)

            attn = jnp.zeros((c, c), jnp.float32)
            for l in range(N_LEVELS):
                el = jnp.exp(ex[l * c:(l + 1) * c])
                a = lax.dot_general((q * el).astype(jnp.bfloat16), (k * el).astype(jnp.bfloat16),
                                    _NT, preferred_element_type=jnp.float32)
                attn = jnp.where(level == l, a, attn)
            diag = jnp.sum(q * k, axis=-1, keepdims=True)
            intra = jnp.dot(attn.astype(jnp.bfloat16), v_bf,
                            preferred_element_type=jnp.float32) + diag * v

            eg = jnp.exp(ex[N_LEVELS * c:(N_LEVELS + 1) * c])
            st = state_ref[hh]
            inter = lax.dot_general((q * eg).astype(jnp.bfloat16), st.astype(jnp.bfloat16),
                                    _NT, preferred_element_type=jnp.float32)
            o = inter + intra

            erest = jnp.exp(ex[(N_LEVELS + 1) * c:(N_LEVELS + 2) * c])
            upd = lax.dot_general(v_bf, (k * erest).astype(jnp.bfloat16), _TN,
                                  preferred_element_type=jnp.float32)
            state_ref[hh] = eg[c - 1:c, :] * st + upd

            ms = jnp.mean(o * o, axis=-1, keepdims=True)
            o = o * lax.rsqrt(ms + NORM_EPS) * gain_ref[:, cols]
            gate = rg_ref[rows, cols].astype(jnp.float32)
            o_ref[rows, cols] = (o * _silu(gate)).astype(o_ref.dtype)
        return carry

    lax.fori_loop(0, chunks, chunk_body, 0)


def _hgrn2(proj, mixed, lb_logits, gain, batch, seq, d_attn, d_rnn, layer, heads=4, chunks=2):
    t = proj.shape[0]
    rows = chunks * CHUNK
    if seq % rows:
        chunks, rows = 1, CHUNK
    nl = seq // rows
    wb = heads * RNN_HEAD_DIM
    n_hg = d_rnn // wb
    kvw = d_attn // 4
    base = d_attn + kvw + d_attn
    assert base % wb == 0 and d_attn % wb == 0
    ncat, level = _decay_tables()

    def seg(i):
        blk0 = (base + i * d_rnn) // wb
        return pl.BlockSpec((rows, wb), lambda b, hg, l: (b * nl + l, blk0 + hg))

    const = lambda shape: pl.BlockSpec(shape, lambda b, hg, l: (0, 0))
    return pl.pallas_call(
        functools.partial(_hgrn2_kernel, heads=heads, chunks=chunks, layer=layer),
        out_shape=jax.ShapeDtypeStruct(mixed.shape, mixed.dtype),
        grid=(batch, n_hg, nl),
        in_specs=[
            const(ncat.shape), const(level.shape),
            pl.BlockSpec((lb_logits.shape[0], wb), lambda b, hg, l: (0, hg)),
            pl.BlockSpec((1, wb), lambda b, hg, l: (0, hg)),
            seg(0), seg(1), seg(2), seg(3),
            pl.BlockSpec(memory_space=pl.ANY),
        ],
        out_specs=pl.BlockSpec((rows, wb), lambda b, hg, l: (b * nl + l, d_attn // wb + hg)),
        scratch_shapes=[pltpu.VMEM((heads, RNN_HEAD_DIM, RNN_HEAD_DIM), jnp.float32)],
        input_output_aliases={8: 0},
        compiler_params=pltpu.CompilerParams(
            dimension_semantics=("parallel", "parallel", "arbitrary"),
            vmem_limit_bytes=VMEM_LIMIT),
        name="hgrn2",
    )(jnp.asarray(ncat, jnp.bfloat16), jnp.asarray(level), lb_logits, gain.reshape(1, d_rnn),
      proj, proj, proj, proj, mixed)


def _out_proj_kernel(a_ref, w_ref, x_ref, g_ref, o_ref, acc_ref):
    kk = pl.program_id(1)

    @pl.when(kk == 0)
    def _():
        acc_ref[...] = jnp.zeros_like(acc_ref)

    d = acc_ref.shape[1]
    for n0 in range(0, d, OUT_COL_CHUNK):
        cols = slice(n0, n0 + OUT_COL_CHUNK)
        acc_ref[:, cols] += jnp.dot(a_ref[...], w_ref[:, cols], preferred_element_type=jnp.float32)

    @pl.when(kk == pl.num_programs(1) - 1)
    def _():
        def row_body(ri, carry):
            rows = pl.ds(pl.multiple_of(ri * OUT_ROW_CHUNK, OUT_ROW_CHUNK), OUT_ROW_CHUNK)
            y = acc_ref[rows, :]
            ms = jnp.mean(y * y, axis=-1, keepdims=True)
            o_ref[rows, :] = x_ref[rows, :] + y * lax.rsqrt(ms + NORM_EPS) * g_ref[...]
            return carry

        lax.fori_loop(0, acc_ref.shape[0] // OUT_ROW_CHUNK, row_body, 0)


def _out_proj(mixed, w, x2d, gain, tm=512, tk=512):
    t, kdim = mixed.shape
    d = w.shape[1]
    tm = min(tm, t)
    return pl.pallas_call(
        _out_proj_kernel,
        out_shape=jax.ShapeDtypeStruct((t, d), jnp.float32),
        grid=(t // tm, kdim // tk),
        in_specs=[pl.BlockSpec((tm, tk), lambda i, k: (i, k)),
                  pl.BlockSpec((tk, d), lambda i, k: (k, 0)),
                  pl.BlockSpec((tm, d), lambda i, k: (i, 0)),
                  pl.BlockSpec((1, d), lambda i, k: (0, 0))],
        out_specs=pl.BlockSpec((tm, d), lambda i, k: (i, 0)),
        scratch_shapes=[pltpu.VMEM((tm, d), jnp.float32)],
        compiler_params=pltpu.CompilerParams(
            dimension_semantics=("parallel", "arbitrary"), vmem_limit_bytes=VMEM_LIMIT),
        name="out_proj",
    )(mixed, w, x2d, gain.reshape(1, d))


def kernel(x, w_in, attn_sinks, lb_logits, rnn_norm, w_out, pre_norm, post_norm):
    batch, seq, d_model = x.shape
    depth = w_in.shape[0]
    d_mix = w_out.shape[1]
    d_attn = d_mix // 2
    d_rnn = d_mix - d_attn
    x2d = x.reshape(batch * seq, d_model)
    for layer in range(depth):
        h = _prenorm(x2d, pre_norm[layer])
        proj = _in_proj(h, w_in[layer].astype(jnp.bfloat16))
        mixed = _swa(proj, attn_sinks[layer], batch, seq, d_attn, d_mix)
        mixed = _hgrn2(proj, mixed, lb_logits, rnn_norm[layer], batch, seq, d_attn, d_rnn, layer)
        x2d = _out_proj(mixed, w_out[layer].astype(jnp.bfloat16), x2d, post_norm[layer])
    return x2d.reshape(batch, seq, d_model)
```

```python
import functools

import jax
import jax.numpy as jnp
import numpy as np
from jax import lax
from jax.experimental import pallas as pl
from jax.experimental.pallas import tpu as pltpu

ATTN_HEAD_DIM = 64
GQA_GROUP = 8
WINDOW = 128
RNN_HEAD_DIM = 128
NORM_EPS = 1e-6

LANES = 128
SUBLANES = 8
CHUNK = 128
N_LEVELS = 7
VMEM_LIMIT = 56 * 1024 * 1024
OUT_COL_CHUNK = 1024
OUT_ROW_CHUNK = 32
OUT_EPILOGUE_STEPS = 4

_NT = (((1,), (1,)), ((), ()))
_TN = (((0,), (0,)), ((), ()))


def _silu(x):
    h = 0.5 * x
    return h + h * jnp.tanh(h)


def _prenorm_kernel(x_ref, g_ref, o_ref):
    x = x_ref[...]
    ms = jnp.mean(x * x, axis=-1, keepdims=True)
    o_ref[...] = (x * lax.rsqrt(ms + NORM_EPS) * g_ref[...]).astype(o_ref.dtype)


def _prenorm(x2d, gain, rows=256):
    t, d = x2d.shape
    rows = min(rows, t)
    return pl.pallas_call(
        _prenorm_kernel,
        out_shape=jax.ShapeDtypeStruct((t, d), jnp.bfloat16),
        grid=(t // rows,),
        in_specs=[pl.BlockSpec((rows, d), lambda i: (i, 0)),
                  pl.BlockSpec((1, d), lambda i: (0, 0))],
        out_specs=pl.BlockSpec((rows, d), lambda i: (i, 0)),
        compiler_params=pltpu.CompilerParams(
            dimension_semantics=("parallel",), vmem_limit_bytes=VMEM_LIMIT),
        name="prenorm",
    )(x2d, gain.reshape(1, d))


def _in_proj_kernel(h_ref, w_ref, o_ref):
    o_ref[...] = jnp.dot(h_ref[...], w_ref[...],
                         preferred_element_type=jnp.float32).astype(o_ref.dtype)


def _in_proj(h, w, tm=1024, tn=1024):
    t, d = h.shape
    n = w.shape[1]
    tm = min(tm, t)
    return pl.pallas_call(
        _in_proj_kernel,
        out_shape=jax.ShapeDtypeStruct((t, n), jnp.bfloat16),
        grid=(t // tm, n // tn),
        in_specs=[pl.BlockSpec((tm, d), lambda i, j: (i, 0)),
                  pl.BlockSpec((d, tn), lambda i, j: (0, j))],
        out_specs=pl.BlockSpec((tm, tn), lambda i, j: (i, j)),
        compiler_params=pltpu.CompilerParams(
            dimension_semantics=("parallel", "parallel"), vmem_limit_bytes=VMEM_LIMIT),
        name="in_proj",
    )(h, w)


def _lane_halves(slab, head_in_high_half):
    lane = lax.broadcasted_iota(jnp.int32, slab.shape, 1)
    swapped = pltpu.roll(slab, 64, axis=1)
    zero = jnp.zeros_like(slab)
    if head_in_high_half:
        lo, hi = swapped, slab
    else:
        lo, hi = slab, swapped
    return jnp.where(lane < 64, lo, zero), jnp.where(lane >= 64, hi, zero)


def _swa_kernel(sink_ref, q_ref, kvp_ref, kvc_ref, g0_ref, g1_ref, g2_ref, g3_ref, o_ref,
                *, n_kv_heads, d_kv):
    n = pl.program_id(1)
    gate_refs = (g0_ref, g1_ref, g2_ref, g3_ref)
    w = WINDOW
    qi = lax.broadcasted_iota(jnp.int32, (w, 2 * w), 0)
    kj = lax.broadcasted_iota(jnp.int32, (w, 2 * w), 1)
    first_valid = jnp.where(n > 0, 0, w)
    valid = (kj > qi) & (kj <= qi + w) & (kj >= first_valid)
    lane = lax.broadcasted_iota(jnp.int32, (w, LANES), 1)
    scale = ATTN_HEAD_DIM ** -0.5
    pairs_per_kv = GQA_GROUP // 2

    for h in range(n_kv_heads):
        c0 = (h // 2) * LANES
        high = (h % 2) == 1
        k_slab = jnp.concatenate([kvp_ref[:, c0:c0 + LANES], kvc_ref[:, c0:c0 + LANES]],
                                 axis=0).astype(jnp.float32)
        v_slab = jnp.concatenate([kvp_ref[:, d_kv + c0:d_kv + c0 + LANES],
                                  kvc_ref[:, d_kv + c0:d_kv + c0 + LANES]],
                                 axis=0).astype(jnp.float32)
        k_lo, k_hi = _lane_halves(k_slab * scale, high)
        v_lo, v_hi = _lane_halves(v_slab, high)
        kcat = jnp.concatenate([k_lo, k_hi], axis=0).astype(jnp.bfloat16)
        vcat = jnp.concatenate([v_lo, v_hi], axis=0).astype(jnp.bfloat16)

        for pp in range(pairs_per_kv):
            p = h * pairs_per_kv + pp
            qp = q_ref[:, p * LANES:(p + 1) * LANES]
            s = lax.dot_general(qp, kcat, _NT, preferred_element_type=jnp.float32)
            probs, inv = [], []
            for hh in range(2):
                sink = sink_ref[2 * p + hh]
                sh = jnp.where(valid, s[:, hh * 2 * w:(hh + 1) * 2 * w], -jnp.inf)
                m = jnp.maximum(jnp.max(sh, axis=-1, keepdims=True), sink)
                e = jnp.exp(sh - m)
                denom = jnp.sum(e, axis=-1, keepdims=True) + jnp.exp(sink - m)
                probs.append(e.astype(jnp.bfloat16))
                inv.append(1.0 / denom)
            pcat = jnp.concatenate(probs, axis=1)
            o = jnp.dot(pcat, vcat, preferred_element_type=jnp.float32)
            o = o * jnp.where(lane < 64, inv[0], inv[1])
            gref = gate_refs[p // 8]
            gate = gref[:, (p % 8) * LANES:(p % 8 + 1) * LANES].astype(jnp.float32)
            o_ref[:, p * LANES:(p + 1) * LANES] = (o * _silu(gate)).astype(o_ref.dtype)


def _swa(proj, sinks, batch, seq, d_attn):
    t = proj.shape[0]
    w = WINDOW
    nb = seq // w
    n_q = d_attn // ATTN_HEAD_DIM
    n_kv = n_q // GQA_GROUP
    d_kv = n_kv * ATTN_HEAD_DIM
    kvw = 2 * d_kv
    assert d_attn == 4 * kvw
    kv_blk = d_attn // kvw
    gate_blk0 = (d_attn + kvw) // kvw

    def row(b, n):
        return b * nb + n

    in_specs = [
        pl.BlockSpec(memory_space=pltpu.SMEM),
        pl.BlockSpec((w, d_attn), lambda b, n: (row(b, n), 0)),
        pl.BlockSpec((w, kvw), lambda b, n: (row(b, jnp.maximum(n - 1, 0)), kv_blk)),
        pl.BlockSpec((w, kvw), lambda b, n: (row(b, n), kv_blk)),
    ] + [
        pl.BlockSpec((w, kvw), functools.partial(lambda b, n, j: (row(b, n), gate_blk0 + j), j=j))
        for j in range(4)
    ]
    return pl.pallas_call(
        functools.partial(_swa_kernel, n_kv_heads=n_kv, d_kv=d_kv),
        out_shape=jax.ShapeDtypeStruct((t, d_attn), jnp.bfloat16),
        grid=(batch, nb),
        in_specs=in_specs,
        out_specs=pl.BlockSpec((w, d_attn), lambda b, n: (row(b, n), 0)),
        compiler_params=pltpu.CompilerParams(
            dimension_semantics=("parallel", "parallel"), vmem_limit_bytes=VMEM_LIMIT),
        name="swa",
    )(sinks, proj, proj, proj, proj, proj, proj, proj)


def _decay_tables():
    c = CHUNK
    n = np.zeros((N_LEVELS + 1, c, c), np.float32)
    r = np.arange(c)
    for l in range(1, N_LEVELS):
        half = 1 << l
        m = 2 * half
        for t in range(c):
            start = (t // m) * m
            mid = start + half - 1
            if t > mid:
                n[l - 1, t] = (r > mid) & (r <= t)
            else:
                n[l - 1, t] = (r > t) & (r <= mid)
    n[N_LEVELS - 1] = r[None, :] <= r[:, None]
    n[N_LEVELS] = r[None, :] > r[:, None]
    n = n.reshape((N_LEVELS + 1) * c, c)
    ncat = np.concatenate([n, n], axis=1)
    tt, ss = np.meshgrid(r, r, indexing="ij")
    x = tt ^ ss
    level = np.full((c, c), -1, np.int32)
    for l in range(N_LEVELS):
        level[(ss < tt) & (x >= (1 << l)) & (x < (2 << l))] = l
    return ncat, level


def _interleave_rows(k, q, half):
    pieces = []
    for r0 in range(0, k.shape[0], half):
        src = q if (r0 // half) % 2 else k
        pieces.append(src[r0:r0 + half])
    return jnp.concatenate(pieces, axis=0)


def _hgrn2_kernel(ncat_ref, level_ref, lbl_ref, gain_ref, rq_ref, rf_ref, ri_ref, rg_ref,
                  o_ref, state_ref, *, heads, chunks, layer):
    c = CHUNK
    pw = 2 * LANES

    @pl.when(pl.program_id(2) == 0)
    def _():
        state_ref[...] = jnp.zeros_like(state_ref)

    lbl = lbl_ref[...]
    e = jnp.exp(lbl - jnp.max(lbl, axis=0, keepdims=True))
    lb_all = jnp.sum(e[:layer + 1], axis=0, keepdims=True) / jnp.sum(e, axis=0, keepdims=True)
    fa_all = 0.5 * (1.0 + lb_all)
    fb_all = 0.5 * (1.0 - lb_all)
    level = level_ref[...]
    ncat = ncat_ref[...]
    trow = lax.broadcasted_iota(jnp.int32, (c, pw), 0)

    def chunk_body(ci, carry):
        r0 = pl.multiple_of(ci * c, c)
        rows = pl.ds(r0, c)
        for pr in range(heads // 2):
            cols = slice(pr * pw, (pr + 1) * pw)
            th = jnp.tanh(0.5 * rf_ref[rows, cols].astype(jnp.float32))
            bt = fb_all[:, cols] * th
            f = fa_all[:, cols] + bt
            k = fb_all[:, cols] - bt
            g = jnp.log(f)
            q = _silu(rq_ref[rows, cols].astype(jnp.float32))
            v = ri_ref[rows, cols].astype(jnp.float32)
            v_bf = v.astype(jnp.bfloat16)

            g_hi = g.astype(jnp.bfloat16)
            g_lo = (g - g_hi.astype(jnp.float32)).astype(jnp.bfloat16)
            gcat = jnp.concatenate([g_hi, g_lo], axis=0)
            ex = jnp.dot(ncat, gcat, preferred_element_type=jnp.float32)

            z = [jnp.where((trow & 1) == 1, q * f, k).astype(jnp.bfloat16)]
            for l in range(1, N_LEVELS):
                half = 1 << l
                if half < SUBLANES:
                    qk = jnp.where((trow & half) != 0, q, k)
                else:
                    qk = _interleave_rows(k, q, half)
                z.append((qk * jnp.exp(ex[(l - 1) * c:l * c])).astype(jnp.bfloat16))
            eg = jnp.exp(ex[(N_LEVELS - 1) * c:N_LEVELS * c])
            erest = jnp.exp(ex[N_LEVELS * c:(N_LEVELS + 1) * c])
            qe = (q * eg).astype(jnp.bfloat16)
            kd = (k * erest).astype(jnp.bfloat16)
            qk_diag = q * k

            outs = []
            for hh in range(2):
                hs = slice(hh * LANES, (hh + 1) * LANES)
                attn = jnp.zeros((c, c), jnp.float32)
                for l in range(N_LEVELS):
                    zl = z[l][:, hs]
                    a = lax.dot_general(zl, zl, _NT, preferred_element_type=jnp.float32)
                    attn = jnp.where(level == l, a, attn)
                diag = jnp.sum(qk_diag[:, hs], axis=-1, keepdims=True)
                intra = jnp.dot(attn.astype(jnp.bfloat16), v_bf[:, hs],
                                preferred_element_type=jnp.float32) + diag * v[:, hs]
                st = state_ref[2 * pr + hh]
                inter = lax.dot_general(qe[:, hs], st.astype(jnp.bfloat16), _NT,
                                        preferred_element_type=jnp.float32)
                upd = lax.dot_general(v_bf[:, hs], kd[:, hs], _TN,
                                      preferred_element_type=jnp.float32)
                state_ref[2 * pr + hh] = eg[c - 1:c, hs] * st + upd
                o = inter + intra
                ms = jnp.mean(o * o, axis=-1, keepdims=True)
                outs.append(o * lax.rsqrt(ms + NORM_EPS))
            o2 = jnp.concatenate(outs, axis=1) * gain_ref[:, cols]
            gate = rg_ref[rows, cols].astype(jnp.float32)
            o_ref[rows, cols] = (o2 * _silu(gate)).astype(o_ref.dtype)
        return carry

    lax.fori_loop(0, chunks, chunk_body, 0)


def _hgrn2(proj, lb_logits, gain, batch, seq, d_attn, d_rnn, layer, heads=4, chunks=2):
    t = proj.shape[0]
    rows = chunks * CHUNK
    if seq % rows:
        chunks, rows = 1, CHUNK
    nl = seq // rows
    wb = heads * RNN_HEAD_DIM
    n_hg = d_rnn // wb
    kvw = d_attn // 4
    base = d_attn + kvw + d_attn
    assert base % wb == 0 and heads % 2 == 0
    ncat, level = _decay_tables()

    def seg(i):
        blk0 = (base + i * d_rnn) // wb
        return pl.BlockSpec((rows, wb), lambda b, hg, l: (b * nl + l, blk0 + hg))

    const = lambda shape: pl.BlockSpec(shape, lambda b, hg, l: (0, 0))
    return pl.pallas_call(
        functools.partial(_hgrn2_kernel, heads=heads, chunks=chunks, layer=layer),
        out_shape=jax.ShapeDtypeStruct((t, d_rnn), jnp.bfloat16),
        grid=(batch, n_hg, nl),
        in_specs=[
            const(ncat.shape), const(level.shape),
            pl.BlockSpec((lb_logits.shape[0], wb), lambda b, hg, l: (0, hg)),
            pl.BlockSpec((1, wb), lambda b, hg, l: (0, hg)),
            seg(0), seg(1), seg(2), seg(3),
        ],
        out_specs=pl.BlockSpec((rows, wb), lambda b, hg, l: (b * nl + l, hg)),
        scratch_shapes=[pltpu.VMEM((heads, RNN_HEAD_DIM, RNN_HEAD_DIM), jnp.float32)],
        compiler_params=pltpu.CompilerParams(
            dimension_semantics=("parallel", "parallel", "arbitrary"),
            vmem_limit_bytes=VMEM_LIMIT),
        name="hgrn2",
    )(jnp.asarray(ncat, jnp.bfloat16), jnp.asarray(level), lb_logits, gain.reshape(1, d_rnn),
      proj, proj, proj, proj)


def _out_proj_kernel(a1_ref, a2_ref, w_ref, x_ref, g_ref, o_ref, acc_ref, *, nk1, nk):
    kk = pl.program_id(1)
    d = acc_ref.shape[1]

    @pl.when(kk == 0)
    def _():
        acc_ref[...] = jnp.zeros_like(acc_ref)

    def accumulate(a_ref):
        for n0 in range(0, d, OUT_COL_CHUNK):
            cols = slice(n0, n0 + OUT_COL_CHUNK)
            acc_ref[:, cols] += jnp.dot(a_ref[...], w_ref[:, cols],
                                        preferred_element_type=jnp.float32)

    @pl.when(kk < nk1)
    def _():
        accumulate(a1_ref)

    @pl.when((kk >= nk1) & (kk < nk))
    def _():
        accumulate(a2_ref)

    @pl.when(kk >= nk)
    def _():
        rows_per_step = o_ref.shape[0]
        base = (kk - nk) * rows_per_step

        def row_body(ri, carry):
            off = pl.multiple_of(ri * OUT_ROW_CHUNK, OUT_ROW_CHUNK)
            y = acc_ref[pl.ds(pl.multiple_of(base + off, OUT_ROW_CHUNK), OUT_ROW_CHUNK), :]
            ms = jnp.mean(y * y, axis=-1, keepdims=True)
            rows = pl.ds(off, OUT_ROW_CHUNK)
            o_ref[rows, :] = x_ref[rows, :] + y * lax.rsqrt(ms + NORM_EPS) * g_ref[...]
            return carry

        lax.fori_loop(0, rows_per_step // OUT_ROW_CHUNK, row_body, 0)


def _out_proj(a1, a2, w, x2d, gain, tm=512, tk=1024):
    t, k1 = a1.shape
    k2 = a2.shape[1]
    d = w.shape[1]
    tm = min(tm, t)
    nk1, nk2 = k1 // tk, k2 // tk
    nk = nk1 + nk2
    ne = OUT_EPILOGUE_STEPS
    te = tm // ne

    def epi(i, k):
        return i * ne + jnp.maximum(k - nk, 0)

    return pl.pallas_call(
        functools.partial(_out_proj_kernel, nk1=nk1, nk=nk),
        out_shape=jax.ShapeDtypeStruct((t, d), jnp.float32),
        grid=(t // tm, nk + ne),
        in_specs=[pl.BlockSpec((tm, tk), lambda i, k: (i, jnp.minimum(k, nk1 - 1))),
                  pl.BlockSpec((tm, tk), lambda i, k: (i, jnp.clip(k - nk1, 0, nk2 - 1))),
                  pl.BlockSpec((tk, d), lambda i, k: (jnp.minimum(k, nk - 1), 0)),
                  pl.BlockSpec((te, d), lambda i, k: (epi(i, k), 0)),
                  pl.BlockSpec((1, d), lambda i, k: (0, 0))],
        out_specs=pl.BlockSpec((te, d), lambda i, k: (epi(i, k), 0)),
        scratch_shapes=[pltpu.VMEM((tm, d), jnp.float32)],
        compiler_params=pltpu.CompilerParams(
            dimension_semantics=("parallel", "arbitrary"), vmem_limit_bytes=VMEM_LIMIT),
        name="out_proj",
    )(a1, a2, w, x2d, gain.reshape(1, d))


def kernel(x, w_in, attn_sinks, lb_logits, rnn_norm, w_out, pre_norm, post_norm):
    batch, seq, d_model = x.shape
    depth = w_in.shape[0]
    d_mix = w_out.shape[1]
    d_attn = d_mix // 2
    d_rnn = d_mix - d_attn
    x2d = x.reshape(batch * seq, d_model)
    for layer in range(depth):
        h = _prenorm(x2d, pre_norm[layer])
        proj = _in_proj(h, w_in[layer].astype(jnp.bfloat16))
        attn = _swa(proj, attn_sinks[layer], batch, seq, d_attn)
        rnn = _hgrn2(proj, lb_logits, rnn_norm[layer], batch, seq, d_attn, d_rnn, layer)
        x2d = _out_proj(attn, rnn, w_out[layer].astype(jnp.bfloat16), x2d, post_norm[layer])
    return x2d.reshape(batch, seq, d_model)
```

```python
import functools

import jax
import jax.numpy as jnp
import numpy as np
from jax import lax
from jax.experimental import pallas as pl
from jax.experimental.pallas import tpu as pltpu

ATTN_HEAD_DIM = 64
GQA_GROUP = 8
WINDOW = 128
RNN_HEAD_DIM = 128
NORM_EPS = 1e-6

LANES = 128
SUBLANES = 8
CHUNK = 128
N_LEVELS = 7
SAFE_LOG2 = 100.0
LOG2_E = 1.4426950408889634
VMEM_LIMIT = 56 * 1024 * 1024
OUT_COL_CHUNK = 1024
OUT_ROW_CHUNK = 32
OUT_EPILOGUE_ROWS = 128

_NT = (((1,), (1,)), ((), ()))
_TN = (((0,), (0,)), ((), ()))


def _skewed(n_items, stages):
    carry = [None] * n_items
    for step in range(n_items + len(stages) - 1):
        for j in reversed(range(len(stages))):
            i = step - j
            if 0 <= i < n_items:
                carry[i] = stages[j](i, carry[i])


def _silu(x):
    h = 0.5 * x
    return h + h * jnp.tanh(h)


def _prenorm_kernel(x_ref, g_ref, o_ref):
    x = x_ref[...]
    ms = jnp.mean(x * x, axis=-1, keepdims=True)
    o_ref[...] = (x * lax.rsqrt(ms + NORM_EPS) * g_ref[...]).astype(o_ref.dtype)


def _prenorm(x2d, gain, rows=256):
    t, d = x2d.shape
    rows = min(rows, t)
    return pl.pallas_call(
        _prenorm_kernel,
        out_shape=jax.ShapeDtypeStruct((t, d), jnp.bfloat16),
        grid=(t // rows,),
        in_specs=[pl.BlockSpec((rows, d), lambda i: (i, 0)),
                  pl.BlockSpec((1, d), lambda i: (0, 0))],
        out_specs=pl.BlockSpec((rows, d), lambda i: (i, 0)),
        compiler_params=pltpu.CompilerParams(
            dimension_semantics=("parallel",), vmem_limit_bytes=VMEM_LIMIT),
        name="prenorm",
    )(x2d, gain.reshape(1, d))


def _in_proj_kernel(h_ref, w_ref, o_ref):
    o_ref[...] = jnp.dot(h_ref[...], w_ref[...],
                         preferred_element_type=jnp.float32).astype(o_ref.dtype)


def _in_proj(h, w, tm=1024, tn=1024):
    t, d = h.shape
    n = w.shape[1]
    tm = min(tm, t)
    return pl.pallas_call(
        _in_proj_kernel,
        out_shape=jax.ShapeDtypeStruct((t, n), jnp.bfloat16),
        grid=(t // tm, n // tn),
        in_specs=[pl.BlockSpec((tm, d), lambda i, j: (i, 0)),
                  pl.BlockSpec((d, tn), lambda i, j: (0, j))],
        out_specs=pl.BlockSpec((tm, tn), lambda i, j: (i, j)),
        compiler_params=pltpu.CompilerParams(
            dimension_semantics=("parallel", "parallel"), vmem_limit_bytes=VMEM_LIMIT),
        name="in_proj",
    )(h, w)


def _lane_halves(slab, head_in_high_half):
    lane = lax.broadcasted_iota(jnp.int32, slab.shape, 1)
    swapped = pltpu.roll(slab, 64, axis=1)
    zero = jnp.zeros_like(slab)
    if head_in_high_half:
        lo, hi = swapped, slab
    else:
        lo, hi = slab, swapped
    return jnp.where(lane < 64, lo, zero), jnp.where(lane >= 64, hi, zero)


def _swa_kernel(sink_ref, q_ref, kvp_ref, kvc_ref, g0_ref, g1_ref, g2_ref, g3_ref, o_ref,
                *, n_kv_heads, d_kv):
    n = pl.program_id(1)
    gate_refs = (g0_ref, g1_ref, g2_ref, g3_ref)
    w = WINDOW
    qi = lax.broadcasted_iota(jnp.int32, (w, 2 * w), 0)
    kj = lax.broadcasted_iota(jnp.int32, (w, 2 * w), 1)
    first_valid = jnp.where(n > 0, 0, w)
    valid = (kj > qi) & (kj <= qi + w) & (kj >= first_valid)
    lane = lax.broadcasted_iota(jnp.int32, (w, LANES), 1)
    scale = ATTN_HEAD_DIM ** -0.5
    pairs_per_kv = GQA_GROUP // 2

    kv_cache = {}

    def kv_operands(h):
        if h not in kv_cache:
            c0 = (h // 2) * LANES
            high = (h % 2) == 1
            k_slab = jnp.concatenate([kvp_ref[:, c0:c0 + LANES], kvc_ref[:, c0:c0 + LANES]],
                                     axis=0).astype(jnp.float32)
            v_slab = jnp.concatenate([kvp_ref[:, d_kv + c0:d_kv + c0 + LANES],
                                      kvc_ref[:, d_kv + c0:d_kv + c0 + LANES]],
                                     axis=0).astype(jnp.float32)
            k_lo, k_hi = _lane_halves(k_slab * scale, high)
            v_lo, v_hi = _lane_halves(v_slab, high)
            kv_cache[h] = (jnp.concatenate([k_lo, k_hi], axis=0).astype(jnp.bfloat16),
                           jnp.concatenate([v_lo, v_hi], axis=0).astype(jnp.bfloat16))
        return kv_cache[h]

    def score_stage(p, _):
        qp = q_ref[:, p * LANES:(p + 1) * LANES]
        kcat = kv_operands(p // pairs_per_kv)[0]
        return lax.dot_general(qp, kcat, _NT, preferred_element_type=jnp.float32)

    def softmax_stage(p, s):
        probs, inv = [], []
        for hh in range(2):
            sink = sink_ref[2 * p + hh]
            sh = jnp.where(valid, s[:, hh * 2 * w:(hh + 1) * 2 * w], -jnp.inf)
            m = jnp.maximum(jnp.max(sh, axis=-1, keepdims=True), sink)
            e = jnp.exp(sh - m)
            denom = jnp.sum(e, axis=-1, keepdims=True) + jnp.exp(sink - m)
            probs.append(e.astype(jnp.bfloat16))
            inv.append(1.0 / denom)
        return jnp.concatenate(probs, axis=1), inv

    def value_stage(p, carry):
        pcat, inv = carry
        vcat = kv_operands(p // pairs_per_kv)[1]
        return jnp.dot(pcat, vcat, preferred_element_type=jnp.float32), inv

    def out_stage(p, carry):
        o, inv = carry
        o = o * jnp.where(lane < 64, inv[0], inv[1])
        gref = gate_refs[p // 8]
        gate = gref[:, (p % 8) * LANES:(p % 8 + 1) * LANES].astype(jnp.float32)
        o_ref[:, p * LANES:(p + 1) * LANES] = (o * _silu(gate)).astype(o_ref.dtype)

    _skewed(n_kv_heads * pairs_per_kv, [score_stage, softmax_stage, value_stage, out_stage])


def _swa(proj, sinks, batch, seq, d_attn):
    t = proj.shape[0]
    w = WINDOW
    nb = seq // w
    n_q = d_attn // ATTN_HEAD_DIM
    n_kv = n_q // GQA_GROUP
    d_kv = n_kv * ATTN_HEAD_DIM
    kvw = 2 * d_kv
    assert d_attn == 4 * kvw
    kv_blk = d_attn // kvw
    gate_blk0 = (d_attn + kvw) // kvw

    def row(b, n):
        return b * nb + n

    in_specs = [
        pl.BlockSpec(memory_space=pltpu.SMEM),
        pl.BlockSpec((w, d_attn), lambda b, n: (row(b, n), 0)),
        pl.BlockSpec((w, kvw), lambda b, n: (row(b, jnp.maximum(n - 1, 0)), kv_blk)),
        pl.BlockSpec((w, kvw), lambda b, n: (row(b, n), kv_blk)),
    ] + [
        pl.BlockSpec((w, kvw), functools.partial(lambda b, n, j: (row(b, n), gate_blk0 + j), j=j))
        for j in range(4)
    ]
    return pl.pallas_call(
        functools.partial(_swa_kernel, n_kv_heads=n_kv, d_kv=d_kv),
        out_shape=jax.ShapeDtypeStruct((t, d_attn), jnp.bfloat16),
        grid=(batch, nb),
        in_specs=in_specs,
        out_specs=pl.BlockSpec((w, d_attn), lambda b, n: (row(b, n), 0)),
        compiler_params=pltpu.CompilerParams(
            dimension_semantics=("parallel", "parallel"), vmem_limit_bytes=VMEM_LIMIT),
        name="swa",
    )(sinks, proj, proj, proj, proj, proj, proj, proj)


def _decay_tables():
    c = CHUNK
    n = np.zeros((N_LEVELS + 1, c, c), np.float32)
    r = np.arange(c)
    for l in range(1, N_LEVELS):
        half = 1 << l
        m = 2 * half
        for t in range(c):
            start = (t // m) * m
            mid = start + half - 1
            if t > mid:
                n[l - 1, t] = (r > mid) & (r <= t)
            else:
                n[l - 1, t] = (r > t) & (r <= mid)
    n[N_LEVELS - 1] = r[None, :] <= r[:, None]
    n[N_LEVELS] = r[None, :] > r[:, None]
    cumsum = np.concatenate([n[N_LEVELS - 1], n[N_LEVELS - 1]], axis=1)
    n = n.reshape((N_LEVELS + 1) * c, c)
    ncat = np.concatenate([n, n], axis=1)
    tt, ss = np.meshgrid(r, r, indexing="ij")
    x = tt ^ ss
    level = np.full((c, c), -1, np.int32)
    for l in range(N_LEVELS):
        level[(ss < tt) & (x >= (1 << l)) & (x < (2 << l))] = l
    return ncat, cumsum, level


def _interleave_rows(k, q, half):
    pieces = []
    for r0 in range(0, k.shape[0], half):
        src = q if (r0 // half) % 2 else k
        pieces.append(src[r0:r0 + half])
    return jnp.concatenate(pieces, axis=0)


def _hgrn2_kernel(ncat_ref, cum_ref, level_ref, lbl_ref, gain_ref, rq_ref, rf_ref, ri_ref, rg_ref,
                  o_ref, state_ref, q_s, k_s, ghi_s, glo_s, *, heads, chunks, layer):
    c = CHUNK
    pw = 2 * LANES
    n_pairs = heads // 2

    @pl.when(pl.program_id(2) == 0)
    def _():
        state_ref[...] = jnp.zeros_like(state_ref)

    lbl = lbl_ref[...]
    e = jnp.exp(lbl - jnp.max(lbl, axis=0, keepdims=True))
    lb_all = jnp.sum(e[:layer + 1], axis=0, keepdims=True) / jnp.sum(e, axis=0, keepdims=True)
    fa_all = 0.5 * (1.0 + lb_all)
    fb_all = 0.5 * (1.0 - lb_all)

    worst = jnp.zeros((1, pw), jnp.float32)
    for ci in range(chunks):
        rows = slice(ci * c, (ci + 1) * c)
        for pr in range(n_pairs):
            cols = slice(pr * pw, (pr + 1) * pw)
            th = jnp.tanh(0.5 * rf_ref[rows, cols].astype(jnp.float32))
            bt = fb_all[:, cols] * th
            g2 = jnp.log(fa_all[:, cols] + bt) * LOG2_E
            k_s[rows, cols] = fb_all[:, cols] - bt
            q_s[rows, cols] = _silu(rq_ref[rows, cols].astype(jnp.float32))
            g_hi = g2.astype(jnp.bfloat16)
            ghi_s[rows, cols] = g_hi
            glo_s[rows, cols] = (g2 - g_hi.astype(jnp.float32)).astype(jnp.bfloat16)
            for half in range(2):
                hsum = jnp.sum(g2[half * (c // 2):(half + 1) * (c // 2)], axis=0, keepdims=True)
                worst = jnp.minimum(worst, hsum)
    mild = jnp.min(worst) > -SAFE_LOG2

    def head_tail(a_bf, qi, kl, egl, v_bf, idx):
        intra = jnp.dot(a_bf, v_bf, preferred_element_type=jnp.float32)
        st = state_ref[idx]
        inter = lax.dot_general(qi, st.astype(jnp.bfloat16), _NT,
                                preferred_element_type=jnp.float32)
        upd = lax.dot_general(v_bf, kl, _TN, preferred_element_type=jnp.float32)
        state_ref[idx] = egl * st + upd
        return inter + intra

    def write_out(outs, rows, cols):
        normed = []
        for o in outs:
            ms = jnp.mean(o * o, axis=-1, keepdims=True)
            normed.append(o * lax.rsqrt(ms + NORM_EPS))
        o2 = jnp.concatenate(normed, axis=1) * gain_ref[:, cols]
        gate = rg_ref[rows, cols].astype(jnp.float32)
        o_ref[rows, cols] = (o2 * _silu(gate)).astype(o_ref.dtype)

    @pl.when(mild)
    def _():
        ti = lax.broadcasted_iota(jnp.int32, (c, c), 0)
        si = lax.broadcasted_iota(jnp.int32, (c, c), 1)
        causal = si <= ti
        halves = [slice(hh * LANES, (hh + 1) * LANES) for hh in range(2)]

        def where(item):
            ci, pr = divmod(item, n_pairs)
            return slice(ci * c, (ci + 1) * c), slice(pr * pw, (pr + 1) * pw), pr

        def cumsum_stage(item, _):
            rows, cols, _ = where(item)
            gcat = jnp.concatenate([ghi_s[rows, cols], glo_s[rows, cols]], axis=0)
            return jnp.dot(cum_ref[...], gcat, preferred_element_type=jnp.float32)

        def scale_stage(item, gsum):
            rows, cols, _ = where(item)
            ref = gsum[c // 2 - 1:c // 2, :]
            last = gsum[c - 1:c, :]
            d = gsum - ref
            qp = q_s[rows, cols] * jnp.exp2(d)
            kp = k_s[rows, cols] * jnp.exp2(-d)
            qi = (qp * jnp.exp2(ref)).astype(jnp.bfloat16)
            kl = (kp * jnp.exp2(last - ref)).astype(jnp.bfloat16)
            qp = qp.astype(jnp.bfloat16)
            kp = kp.astype(jnp.bfloat16)
            a = [lax.dot_general(qp[:, hs], kp[:, hs], _NT, preferred_element_type=jnp.float32)
                 for hs in halves]
            return a, qi, kl, jnp.exp2(last)

        def mix_stage(item, carry):
            a, qi, kl, egl = carry
            rows, cols, pr = where(item)
            v_bf = ri_ref[rows, cols]
            return [head_tail(jnp.where(causal, a[hh], 0.0).astype(jnp.bfloat16),
                              qi[:, hs], kl[:, hs], egl[:, hs], v_bf[:, hs], 2 * pr + hh)
                    for hh, hs in enumerate(halves)]

        def out_stage(item, outs):
            rows, cols, _ = where(item)
            write_out(outs, rows, cols)

        _skewed(chunks * n_pairs, [cumsum_stage, scale_stage, mix_stage, out_stage])

    @pl.when(jnp.logical_not(mild))
    def _():
        level = level_ref[...]
        trow = lax.broadcasted_iota(jnp.int32, (c, pw), 0)

        def chunk_body(ci, carry):
            rows = pl.ds(pl.multiple_of(ci * c, c), c)
            for pr in range(n_pairs):
                cols = slice(pr * pw, (pr + 1) * pw)
                q = q_s[rows, cols]
                k = k_s[rows, cols]
                g_hi = ghi_s[rows, cols]
                g_lo = glo_s[rows, cols]
                f = jnp.exp2(g_hi.astype(jnp.float32) + g_lo.astype(jnp.float32))
                gcat = jnp.concatenate([g_hi, g_lo], axis=0)
                ex = jnp.dot(ncat_ref[...], gcat, preferred_element_type=jnp.float32)

                z = [jnp.where((trow & 1) == 1, q * f, k).astype(jnp.bfloat16)]
                for l in range(1, N_LEVELS):
                    half = 1 << l
                    if half < SUBLANES:
                        qk = jnp.where((trow & half) != 0, q, k)
                    else:
                        qk = _interleave_rows(k, q, half)
                    z.append((qk * jnp.exp2(ex[(l - 1) * c:l * c])).astype(jnp.bfloat16))
                eg = jnp.exp2(ex[(N_LEVELS - 1) * c:N_LEVELS * c])
                erest = jnp.exp2(ex[N_LEVELS * c:(N_LEVELS + 1) * c])
                qi = (q * eg).astype(jnp.bfloat16)
                kl = (k * erest).astype(jnp.bfloat16)
                qk_diag = q * k
                v_bf = ri_ref[rows, cols]
                outs = []
                for hh in range(2):
                    hs = slice(hh * LANES, (hh + 1) * LANES)
                    attn = jnp.zeros((c, c), jnp.float32)
                    for l in range(N_LEVELS):
                        zl = z[l][:, hs]
                        a = lax.dot_general(zl, zl, _NT, preferred_element_type=jnp.float32)
                        attn = jnp.where(level == l, a, attn)
                    diag = jnp.sum(qk_diag[:, hs], axis=-1, keepdims=True)
                    o = head_tail(attn.astype(jnp.bfloat16), qi[:, hs], kl[:, hs],
                                  eg[c - 1:c, hs], v_bf[:, hs], 2 * pr + hh)
                    outs.append(o + diag * v_bf[:, hs].astype(jnp.float32))
                write_out(outs, rows, cols)
            return carry

        lax.fori_loop(0, chunks, chunk_body, 0)


def _hgrn2(proj, lb_logits, gain, batch, seq, d_attn, d_rnn, layer, heads=4, chunks=4):
    t = proj.shape[0]
    rows = chunks * CHUNK
    if seq % rows:
        chunks, rows = 1, CHUNK
    nl = seq // rows
    wb = heads * RNN_HEAD_DIM
    n_hg = d_rnn // wb
    kvw = d_attn // 4
    base = d_attn + kvw + d_attn
    assert base % wb == 0 and heads % 2 == 0
    ncat, cumsum, level = _decay_tables()

    def seg(i):
        blk0 = (base + i * d_rnn) // wb
        return pl.BlockSpec((rows, wb), lambda b, hg, l: (b * nl + l, blk0 + hg))

    const = lambda shape: pl.BlockSpec(shape, lambda b, hg, l: (0, 0))
    return pl.pallas_call(
        functools.partial(_hgrn2_kernel, heads=heads, chunks=chunks, layer=layer),
        out_shape=jax.ShapeDtypeStruct((t, d_rnn), jnp.bfloat16),
        grid=(batch, n_hg, nl),
        in_specs=[
            const(ncat.shape), const(cumsum.shape), const(level.shape),
            pl.BlockSpec((lb_logits.shape[0], wb), lambda b, hg, l: (0, hg)),
            pl.BlockSpec((1, wb), lambda b, hg, l: (0, hg)),
            seg(0), seg(1), seg(2), seg(3),
        ],
        out_specs=pl.BlockSpec((rows, wb), lambda b, hg, l: (b * nl + l, hg)),
        scratch_shapes=[pltpu.VMEM((heads, RNN_HEAD_DIM, RNN_HEAD_DIM), jnp.float32),
                        pltpu.VMEM((rows, wb), jnp.float32),
                        pltpu.VMEM((rows, wb), jnp.float32),
                        pltpu.VMEM((rows, wb), jnp.bfloat16),
                        pltpu.VMEM((rows, wb), jnp.bfloat16)],
        compiler_params=pltpu.CompilerParams(
            dimension_semantics=("parallel", "parallel", "arbitrary"),
            vmem_limit_bytes=VMEM_LIMIT),
        name="hgrn2",
    )(jnp.asarray(ncat, jnp.bfloat16), jnp.asarray(cumsum, jnp.bfloat16), jnp.asarray(level),
      lb_logits, gain.reshape(1, d_rnn), proj, proj, proj, proj)


def _out_proj_kernel(a1_ref, a2_ref, w_ref, x_ref, g_ref, o_ref, acc_ref, *, nk1, nk):
    kk = pl.program_id(1)
    d = acc_ref.shape[1]

    @pl.when(kk == 0)
    def _():
        acc_ref[...] = jnp.zeros_like(acc_ref)

    def accumulate(a_ref):
        for n0 in range(0, d, OUT_COL_CHUNK):
            cols = slice(n0, n0 + OUT_COL_CHUNK)
            acc_ref[:, cols] += jnp.dot(a_ref[...], w_ref[:, cols],
                                        preferred_element_type=jnp.float32)

    @pl.when(kk < nk1)
    def _():
        accumulate(a1_ref)

    @pl.when((kk >= nk1) & (kk < nk))
    def _():
        accumulate(a2_ref)

    @pl.when(kk >= nk)
    def _():
        rows_per_step = o_ref.shape[0]
        base = (kk - nk) * rows_per_step

        def row_body(ri, carry):
            off = pl.multiple_of(ri * OUT_ROW_CHUNK, OUT_ROW_CHUNK)
            y = acc_ref[pl.ds(pl.multiple_of(base + off, OUT_ROW_CHUNK), OUT_ROW_CHUNK), :]
            ms = jnp.mean(y * y, axis=-1, keepdims=True)
            rows = pl.ds(off, OUT_ROW_CHUNK)
            o_ref[rows, :] = x_ref[rows, :] + y * lax.rsqrt(ms + NORM_EPS) * g_ref[...]
            return carry

        lax.fori_loop(0, rows_per_step // OUT_ROW_CHUNK, row_body, 0)


def _out_proj(a1, a2, w, x2d, gain, tm=1024, tk=1024):
    t, k1 = a1.shape
    k2 = a2.shape[1]
    d = w.shape[1]
    tm = min(tm, t)
    nk1, nk2 = k1 // tk, k2 // tk
    nk = nk1 + nk2
    te = min(OUT_EPILOGUE_ROWS, tm)
    ne = tm // te

    def epi(i, k):
        return i * ne + jnp.maximum(k - nk, 0)

    return pl.pallas_call(
        functools.partial(_out_proj_kernel, nk1=nk1, nk=nk),
        out_shape=jax.ShapeDtypeStruct((t, d), jnp.float32),
        grid=(t // tm, nk + ne),
        in_specs=[pl.BlockSpec((tm, tk), lambda i, k: (i, jnp.minimum(k, nk1 - 1))),
                  pl.BlockSpec((tm, tk), lambda i, k: (i, jnp.clip(k - nk1, 0, nk2 - 1))),
                  pl.BlockSpec((tk, d), lambda i, k: (jnp.minimum(k, nk - 1), 0)),
                  pl.BlockSpec((te, d), lambda i, k: (epi(i, k), 0)),
                  pl.BlockSpec((1, d), lambda i, k: (0, 0))],
        out_specs=pl.BlockSpec((te, d), lambda i, k: (epi(i, k), 0)),
        scratch_shapes=[pltpu.VMEM((tm, d), jnp.float32)],
        compiler_params=pltpu.CompilerParams(
            dimension_semantics=("parallel", "arbitrary"), vmem_limit_bytes=VMEM_LIMIT),
        name="out_proj",
    )(a1, a2, w, x2d, gain.reshape(1, d))


def kernel(x, w_in, attn_sinks, lb_logits, rnn_norm, w_out, pre_norm, post_norm):
    batch, seq, d_model = x.shape
    depth = w_in.shape[0]
    d_mix = w_out.shape[1]
    d_attn = d_mix // 2
    d_rnn = d_mix - d_attn
    x2d = x.reshape(batch * seq, d_model)
    for layer in range(depth):
        h = _prenorm(x2d, pre_norm[layer])
        proj = _in_proj(h, w_in[layer].astype(jnp.bfloat16))
        attn = _swa(proj, attn_sinks[layer], batch, seq, d_attn)
        rnn = _hgrn2(proj, lb_logits, rnn_norm[layer], batch, seq, d_attn, d_rnn, layer)
        x2d = _out_proj(attn, rnn, w_out[layer].astype(jnp.bfloat16), x2d, post_norm[layer])
    return x2d.reshape(batch, seq, d_model)
```

```python
import functools

import jax
import jax.numpy as jnp
import numpy as np
from jax import lax
from jax.experimental import pallas as pl
from jax.experimental.pallas import tpu as pltpu

ATTN_HEAD_DIM = 64
GQA_GROUP = 8
WINDOW = 128
RNN_HEAD_DIM = 128
NORM_EPS = 1e-6

LANES = 128
SUBLANES = 8
CHUNK = 128
N_LEVELS = 7
SAFE_LOG2 = 100.0
LOG2_E = 1.4426950408889634
VMEM_LIMIT = 56 * 1024 * 1024
OUT_COL_CHUNK = 1024
OUT_ROW_CHUNK = 32
OUT_EPILOGUE_ROWS = 128

_NT = (((1,), (1,)), ((), ()))
_TN = (((0,), (0,)), ((), ()))


def _skewed(n_items, stages, gap=2):
    carry = [None] * n_items
    for step in range(n_items + gap * (len(stages) - 1)):
        for j in reversed(range(len(stages))):
            i = step - gap * j
            if 0 <= i < n_items:
                carry[i] = stages[j](i, carry[i])


def _silu(x):
    h = 0.5 * x
    return h + h * jnp.tanh(h)


def _prenorm_kernel(x_ref, g_ref, o_ref):
    x = x_ref[...]
    ms = jnp.mean(x * x, axis=-1, keepdims=True)
    o_ref[...] = (x * lax.rsqrt(ms + NORM_EPS) * g_ref[...]).astype(o_ref.dtype)


def _prenorm(x2d, gain, rows=256):
    t, d = x2d.shape
    rows = min(rows, t)
    return pl.pallas_call(
        _prenorm_kernel,
        out_shape=jax.ShapeDtypeStruct((t, d), jnp.bfloat16),
        grid=(t // rows,),
        in_specs=[pl.BlockSpec((rows, d), lambda i: (i, 0)),
                  pl.BlockSpec((1, d), lambda i: (0, 0))],
        out_specs=pl.BlockSpec((rows, d), lambda i: (i, 0)),
        compiler_params=pltpu.CompilerParams(
            dimension_semantics=("parallel",), vmem_limit_bytes=VMEM_LIMIT),
        name="prenorm",
    )(x2d, gain.reshape(1, d))


def _in_proj_kernel(h_ref, w_ref, o_ref):
    o_ref[...] = jnp.dot(h_ref[...], w_ref[...],
                         preferred_element_type=jnp.float32).astype(o_ref.dtype)


def _in_proj(h, w, tm=1024, tn=1024):
    t, d = h.shape
    n = w.shape[1]
    tm = min(tm, t)
    return pl.pallas_call(
        _in_proj_kernel,
        out_shape=jax.ShapeDtypeStruct((t, n), jnp.bfloat16),
        grid=(t // tm, n // tn),
        in_specs=[pl.BlockSpec((tm, d), lambda i, j: (i, 0)),
                  pl.BlockSpec((d, tn), lambda i, j: (0, j))],
        out_specs=pl.BlockSpec((tm, tn), lambda i, j: (i, j)),
        compiler_params=pltpu.CompilerParams(
            dimension_semantics=("parallel", "parallel"), vmem_limit_bytes=VMEM_LIMIT),
        name="in_proj",
    )(h, w)


def _lane_halves(slab, head_in_high_half):
    lane = lax.broadcasted_iota(jnp.int32, slab.shape, 1)
    swapped = pltpu.roll(slab, 64, axis=1)
    zero = jnp.zeros_like(slab)
    if head_in_high_half:
        lo, hi = swapped, slab
    else:
        lo, hi = slab, swapped
    return jnp.where(lane < 64, lo, zero), jnp.where(lane >= 64, hi, zero)


def _swa_kernel(sink_ref, q_ref, kvp_ref, kvc_ref, g0_ref, g1_ref, g2_ref, g3_ref, o_ref,
                *, n_kv_heads, d_kv):
    n = pl.program_id(1)
    gate_refs = (g0_ref, g1_ref, g2_ref, g3_ref)
    w = WINDOW
    qi = lax.broadcasted_iota(jnp.int32, (w, w), 0)
    kj = lax.broadcasted_iota(jnp.int32, (w, w), 1)
    lower = kj <= qi
    prev_bias = jnp.where(n > 0, 0.0, -jnp.inf)
    lane = lax.broadcasted_iota(jnp.int32, (w, LANES), 1)
    scale = ATTN_HEAD_DIM ** -0.5 * LOG2_E
    pairs_per_kv = GQA_GROUP // 2

    kv_cache = {}

    def kv_operands(h):
        if h not in kv_cache:
            c0 = (h // 2) * LANES
            high = (h % 2) == 1
            k_slab = jnp.concatenate([kvp_ref[:, c0:c0 + LANES], kvc_ref[:, c0:c0 + LANES]],
                                     axis=0).astype(jnp.float32)
            v_slab = jnp.concatenate([kvp_ref[:, d_kv + c0:d_kv + c0 + LANES],
                                      kvc_ref[:, d_kv + c0:d_kv + c0 + LANES]],
                                     axis=0).astype(jnp.float32)
            k_lo, k_hi = _lane_halves(k_slab * scale, high)
            v_lo, v_hi = _lane_halves(v_slab, high)
            kv_cache[h] = (jnp.concatenate([k_lo, k_hi], axis=0).astype(jnp.bfloat16),
                           jnp.concatenate([v_lo, v_hi], axis=0).astype(jnp.bfloat16))
        return kv_cache[h]

    def score_stage(p, _):
        qp = q_ref[:, p * LANES:(p + 1) * LANES]
        kcat = kv_operands(p // pairs_per_kv)[0]
        return lax.dot_general(qp, kcat, _NT, preferred_element_type=jnp.float32)

    def max_stage(p, s):
        merged, mx = [], []
        for hh in range(2):
            prev = s[:, hh * 2 * w:hh * 2 * w + w] + prev_bias
            cur = s[:, hh * 2 * w + w:(hh + 1) * 2 * w]
            sh = jnp.where(lower, cur, prev)
            merged.append(sh)
            mx.append(jnp.maximum(jnp.max(sh, axis=-1, keepdims=True), sink_ref[2 * p + hh] * LOG2_E))
        return merged, mx

    def exp_stage(p, carry):
        merged, mx = carry
        probs, denom = [], []
        for hh in range(2):
            e = jnp.exp2(merged[hh] - mx[hh])
            denom.append(jnp.sum(e, axis=-1, keepdims=True)
                         + jnp.exp2(sink_ref[2 * p + hh] * LOG2_E - mx[hh]))
            e = e.astype(jnp.bfloat16)
            zero = jnp.zeros_like(e)
            probs += [jnp.where(lower, zero, e), jnp.where(lower, e, zero)]
        return jnp.concatenate(probs, axis=1), denom

    def value_stage(p, carry):
        pcat, denom = carry
        vcat = kv_operands(p // pairs_per_kv)[1]
        return jnp.dot(pcat, vcat, preferred_element_type=jnp.float32), denom

    def out_stage(p, carry):
        o, denom = carry
        o = o * jnp.where(lane < 64, 1.0 / denom[0], 1.0 / denom[1])
        gref = gate_refs[p // 8]
        gate = gref[:, (p % 8) * LANES:(p % 8 + 1) * LANES].astype(jnp.float32)
        o_ref[:, p * LANES:(p + 1) * LANES] = (o * _silu(gate)).astype(o_ref.dtype)

    _skewed(n_kv_heads * pairs_per_kv,
            [score_stage, max_stage, exp_stage, value_stage, out_stage])


def _swa(proj, sinks, batch, seq, d_attn):
    t = proj.shape[0]
    w = WINDOW
    nb = seq // w
    n_q = d_attn // ATTN_HEAD_DIM
    n_kv = n_q // GQA_GROUP
    d_kv = n_kv * ATTN_HEAD_DIM
    kvw = 2 * d_kv
    assert d_attn == 4 * kvw
    kv_blk = d_attn // kvw
    gate_blk0 = (d_attn + kvw) // kvw

    def row(b, n):
        return b * nb + n

    in_specs = [
        pl.BlockSpec(memory_space=pltpu.SMEM),
        pl.BlockSpec((w, d_attn), lambda b, n: (row(b, n), 0)),
        pl.BlockSpec((w, kvw), lambda b, n: (row(b, jnp.maximum(n - 1, 0)), kv_blk)),
        pl.BlockSpec((w, kvw), lambda b, n: (row(b, n), kv_blk)),
    ] + [
        pl.BlockSpec((w, kvw), functools.partial(lambda b, n, j: (row(b, n), gate_blk0 + j), j=j))
        for j in range(4)
    ]
    return pl.pallas_call(
        functools.partial(_swa_kernel, n_kv_heads=n_kv, d_kv=d_kv),
        out_shape=jax.ShapeDtypeStruct((t, d_attn), jnp.bfloat16),
        grid=(batch, nb),
        in_specs=in_specs,
        out_specs=pl.BlockSpec((w, d_attn), lambda b, n: (row(b, n), 0)),
        compiler_params=pltpu.CompilerParams(
            dimension_semantics=("parallel", "parallel"), vmem_limit_bytes=VMEM_LIMIT),
        name="swa",
    )(sinks, proj, proj, proj, proj, proj, proj, proj)


def _decay_tables():
    c = CHUNK
    n = np.zeros((N_LEVELS + 1, c, c), np.float32)
    r = np.arange(c)
    for l in range(1, N_LEVELS):
        half = 1 << l
        m = 2 * half
        for t in range(c):
            start = (t // m) * m
            mid = start + half - 1
            if t > mid:
                n[l - 1, t] = (r > mid) & (r <= t)
            else:
                n[l - 1, t] = (r > t) & (r <= mid)
    n[N_LEVELS - 1] = r[None, :] <= r[:, None]
    n[N_LEVELS] = r[None, :] > r[:, None]
    cumsum = np.concatenate([n[N_LEVELS - 1], n[N_LEVELS - 1]], axis=1)
    n = n.reshape((N_LEVELS + 1) * c, c)
    ncat = np.concatenate([n, n], axis=1)
    tt, ss = np.meshgrid(r, r, indexing="ij")
    x = tt ^ ss
    level = np.full((c, c), -1, np.int32)
    for l in range(N_LEVELS):
        level[(ss < tt) & (x >= (1 << l)) & (x < (2 << l))] = l
    return ncat, cumsum, level


def _interleave_rows(k, q, half):
    pieces = []
    for r0 in range(0, k.shape[0], half):
        src = q if (r0 // half) % 2 else k
        pieces.append(src[r0:r0 + half])
    return jnp.concatenate(pieces, axis=0)


def _hgrn2_kernel(ncat_ref, cum_ref, level_ref, lbl_ref, gain_ref, rq_ref, rf_ref, ri_ref, rg_ref,
                  o_ref, state_ref, q_s, k_s, ghi_s, glo_s, *, heads, chunks, layer):
    c = CHUNK
    pw = 2 * LANES
    n_pairs = heads // 2

    @pl.when(pl.program_id(2) == 0)
    def _():
        state_ref[...] = jnp.zeros_like(state_ref)

    lbl = lbl_ref[...]
    e = jnp.exp(lbl - jnp.max(lbl, axis=0, keepdims=True))
    lb_all = jnp.sum(e[:layer + 1], axis=0, keepdims=True) / jnp.sum(e, axis=0, keepdims=True)
    fa_all = 0.5 * (1.0 + lb_all)
    fb_all = 0.5 * (1.0 - lb_all)

    worst = jnp.zeros((1, pw), jnp.float32)
    for ci in range(chunks):
        rows = slice(ci * c, (ci + 1) * c)
        for pr in range(n_pairs):
            cols = slice(pr * pw, (pr + 1) * pw)
            th = jnp.tanh(0.5 * rf_ref[rows, cols].astype(jnp.float32))
            bt = fb_all[:, cols] * th
            g2 = jnp.log(fa_all[:, cols] + bt) * LOG2_E
            k_s[rows, cols] = fb_all[:, cols] - bt
            q_s[rows, cols] = _silu(rq_ref[rows, cols].astype(jnp.float32))
            g_hi = g2.astype(jnp.bfloat16)
            ghi_s[rows, cols] = g_hi
            glo_s[rows, cols] = (g2 - g_hi.astype(jnp.float32)).astype(jnp.bfloat16)
            for half in range(2):
                hsum = jnp.sum(g2[half * (c // 2):(half + 1) * (c // 2)], axis=0, keepdims=True)
                worst = jnp.minimum(worst, hsum)
    mild = jnp.min(worst) > -SAFE_LOG2

    def head_matmuls(a_bf, qi, kl, v_bf, idx):
        intra = jnp.dot(a_bf, v_bf, preferred_element_type=jnp.float32)
        st = state_ref[idx]
        inter = lax.dot_general(qi, st.astype(jnp.bfloat16), _NT,
                                preferred_element_type=jnp.float32)
        upd = lax.dot_general(v_bf, kl, _TN, preferred_element_type=jnp.float32)
        return inter + intra, st, upd

    def head_tail(a_bf, qi, kl, egl, v_bf, idx):
        o, st, upd = head_matmuls(a_bf, qi, kl, v_bf, idx)
        state_ref[idx] = egl * st + upd
        return o

    def write_out(outs, rows, cols):
        normed = []
        for o in outs:
            ms = jnp.mean(o * o, axis=-1, keepdims=True)
            normed.append(o * lax.rsqrt(ms + NORM_EPS))
        o2 = jnp.concatenate(normed, axis=1) * gain_ref[:, cols]
        gate = rg_ref[rows, cols].astype(jnp.float32)
        o_ref[rows, cols] = (o2 * _silu(gate)).astype(o_ref.dtype)

    @pl.when(mild)
    def _():
        ti = lax.broadcasted_iota(jnp.int32, (c, c), 0)
        si = lax.broadcasted_iota(jnp.int32, (c, c), 1)
        causal = si <= ti
        halves = [slice(hh * LANES, (hh + 1) * LANES) for hh in range(2)]

        def where(item):
            ci, pr = divmod(item, n_pairs)
            return slice(ci * c, (ci + 1) * c), slice(pr * pw, (pr + 1) * pw), pr

        def cumsum_stage(item, _):
            rows, cols, _ = where(item)
            gcat = jnp.concatenate([ghi_s[rows, cols], glo_s[rows, cols]], axis=0)
            return jnp.dot(cum_ref[...], gcat, preferred_element_type=jnp.float32)

        def scale_stage(item, gsum):
            rows, cols, _ = where(item)
            ref = gsum[c // 2 - 1:c // 2, :]
            last = gsum[c - 1:c, :]
            d = gsum - ref
            qp = q_s[rows, cols] * jnp.exp2(d)
            kp = k_s[rows, cols] * jnp.exp2(-d)
            qi = (qp * jnp.exp2(ref)).astype(jnp.bfloat16)
            kl = (kp * jnp.exp2(last - ref)).astype(jnp.bfloat16)
            qp = qp.astype(jnp.bfloat16)
            kp = kp.astype(jnp.bfloat16)
            a = [lax.dot_general(qp[:, hs], kp[:, hs], _NT, preferred_element_type=jnp.float32)
                 for hs in halves]
            return a, qi, kl, jnp.exp2(last)

        def mix_stage(item, carry):
            a, qi, kl, egl = carry
            rows, cols, pr = where(item)
            v_bf = ri_ref[rows, cols]
            return [head_matmuls(jnp.where(causal, a[hh], 0.0).astype(jnp.bfloat16),
                                 qi[:, hs], kl[:, hs], v_bf[:, hs], 2 * pr + hh)
                    for hh, hs in enumerate(halves)], egl

        def out_stage(item, carry):
            heads_out, egl = carry
            rows, cols, pr = where(item)
            for hh, hs in enumerate(halves):
                _, st, upd = heads_out[hh]
                state_ref[2 * pr + hh] = egl[:, hs] * st + upd
            write_out([o for o, _, _ in heads_out], rows, cols)

        _skewed(chunks * n_pairs, [cumsum_stage, scale_stage, mix_stage, out_stage])

    @pl.when(jnp.logical_not(mild))
    def _():
        level = level_ref[...]
        trow = lax.broadcasted_iota(jnp.int32, (c, pw), 0)

        def chunk_body(ci, carry):
            rows = pl.ds(pl.multiple_of(ci * c, c), c)
            for pr in range(n_pairs):
                cols = slice(pr * pw, (pr + 1) * pw)
                q = q_s[rows, cols]
                k = k_s[rows, cols]
                g_hi = ghi_s[rows, cols]
                g_lo = glo_s[rows, cols]
                f = jnp.exp2(g_hi.astype(jnp.float32) + g_lo.astype(jnp.float32))
                gcat = jnp.concatenate([g_hi, g_lo], axis=0)
                ex = jnp.dot(ncat_ref[...], gcat, preferred_element_type=jnp.float32)

                z = [jnp.where((trow & 1) == 1, q * f, k).astype(jnp.bfloat16)]
                for l in range(1, N_LEVELS):
                    half = 1 << l
                    if half < SUBLANES:
                        qk = jnp.where((trow & half) != 0, q, k)
                    else:
                        qk = _interleave_rows(k, q, half)
                    z.append((qk * jnp.exp2(ex[(l - 1) * c:l * c])).astype(jnp.bfloat16))
                eg = jnp.exp2(ex[(N_LEVELS - 1) * c:N_LEVELS * c])
                erest = jnp.exp2(ex[N_LEVELS * c:(N_LEVELS + 1) * c])
                qi = (q * eg).astype(jnp.bfloat16)
                kl = (k * erest).astype(jnp.bfloat16)
                qk_diag = q * k
                v_bf = ri_ref[rows, cols]
                outs = []
                for hh in range(2):
                    hs = slice(hh * LANES, (hh + 1) * LANES)
                    attn = jnp.zeros((c, c), jnp.float32)
                    for l in range(N_LEVELS):
                        zl = z[l][:, hs]
                        a = lax.dot_general(zl, zl, _NT, preferred_element_type=jnp.float32)
                        attn = jnp.where(level == l, a, attn)
                    diag = jnp.sum(qk_diag[:, hs], axis=-1, keepdims=True)
                    o = head_tail(attn.astype(jnp.bfloat16), qi[:, hs], kl[:, hs],
                                  eg[c - 1:c, hs], v_bf[:, hs], 2 * pr + hh)
                    outs.append(o + diag * v_bf[:, hs].astype(jnp.float32))
                write_out(outs, rows, cols)
            return carry

        lax.fori_loop(0, chunks, chunk_body, 0)


def _hgrn2(proj, lb_logits, gain, batch, seq, d_attn, d_rnn, layer, heads=4, chunks=8):
    t = proj.shape[0]
    rows = chunks * CHUNK
    if seq % rows:
        chunks, rows = 1, CHUNK
    nl = seq // rows
    wb = heads * RNN_HEAD_DIM
    n_hg = d_rnn // wb
    kvw = d_attn // 4
    base = d_attn + kvw + d_attn
    assert base % wb == 0 and heads % 2 == 0
    ncat, cumsum, level = _decay_tables()

    def seg(i):
        blk0 = (base + i * d_rnn) // wb
        return pl.BlockSpec((rows, wb), lambda b, hg, l: (b * nl + l, blk0 + hg))

    const = lambda shape: pl.BlockSpec(shape, lambda b, hg, l: (0, 0))
    return pl.pallas_call(
        functools.partial(_hgrn2_kernel, heads=heads, chunks=chunks, layer=layer),
        out_shape=jax.ShapeDtypeStruct((t, d_rnn), jnp.bfloat16),
        grid=(batch, n_hg, nl),
        in_specs=[
            const(ncat.shape), const(cumsum.shape), const(level.shape),
            pl.BlockSpec((lb_logits.shape[0], wb), lambda b, hg, l: (0, hg)),
            pl.BlockSpec((1, wb), lambda b, hg, l: (0, hg)),
            seg(0), seg(1), seg(2), seg(3),
        ],
        out_specs=pl.BlockSpec((rows, wb), lambda b, hg, l: (b * nl + l, hg)),
        scratch_shapes=[pltpu.VMEM((heads, RNN_HEAD_DIM, RNN_HEAD_DIM), jnp.float32),
                        pltpu.VMEM((rows, wb), jnp.float32),
                        pltpu.VMEM((rows, wb), jnp.float32),
                        pltpu.VMEM((rows, wb), jnp.bfloat16),
                        pltpu.VMEM((rows, wb), jnp.bfloat16)],
        compiler_params=pltpu.CompilerParams(
            dimension_semantics=("parallel", "parallel", "arbitrary"),
            vmem_limit_bytes=VMEM_LIMIT),
        name="hgrn2",
    )(jnp.asarray(ncat, jnp.bfloat16), jnp.asarray(cumsum, jnp.bfloat16), jnp.asarray(level),
      lb_logits, gain.reshape(1, d_rnn), proj, proj, proj, proj)


def _out_proj_kernel(a1_ref, a2_ref, w_ref, x_ref, g_ref, o_ref, acc_ref, *, nk1, nk):
    kk = pl.program_id(1)
    d = acc_ref.shape[1]

    @pl.when(kk == 0)
    def _():
        acc_ref[...] = jnp.zeros_like(acc_ref)

    def accumulate(a_ref):
        for n0 in range(0, d, OUT_COL_CHUNK):
            cols = slice(n0, n0 + OUT_COL_CHUNK)
            acc_ref[:, cols] += jnp.dot(a_ref[...], w_ref[:, cols],
                                        preferred_element_type=jnp.float32)

    @pl.when(kk < nk1)
    def _():
        accumulate(a1_ref)

    @pl.when((kk >= nk1) & (kk < nk))
    def _():
        accumulate(a2_ref)

    @pl.when(kk >= nk)
    def _():
        rows_per_step = o_ref.shape[0]
        base = (kk - nk) * rows_per_step

        def row_body(ri, carry):
            off = pl.multiple_of(ri * OUT_ROW_CHUNK, OUT_ROW_CHUNK)
            y = acc_ref[pl.ds(pl.multiple_of(base + off, OUT_ROW_CHUNK), OUT_ROW_CHUNK), :]
            ms = jnp.mean(y * y, axis=-1, keepdims=True)
            rows = pl.ds(off, OUT_ROW_CHUNK)
            o_ref[rows, :] = x_ref[rows, :] + y * lax.rsqrt(ms + NORM_EPS) * g_ref[...]
            return carry

        lax.fori_loop(0, rows_per_step // OUT_ROW_CHUNK, row_body, 0)


def _out_proj(a1, a2, w, x2d, gain, tm=1024, tk=1024):
    t, k1 = a1.shape
    k2 = a2.shape[1]
    d = w.shape[1]
    tm = min(tm, t)
    nk1, nk2 = k1 // tk, k2 // tk
    nk = nk1 + nk2
    te = min(OUT_EPILOGUE_ROWS, tm)
    ne = tm // te

    def epi(i, k):
        return i * ne + jnp.maximum(k - nk, 0)

    return pl.pallas_call(
        functools.partial(_out_proj_kernel, nk1=nk1, nk=nk),
        out_shape=jax.ShapeDtypeStruct((t, d), jnp.float32),
        grid=(t // tm, nk + ne),
        in_specs=[pl.BlockSpec((tm, tk), lambda i, k: (i, jnp.minimum(k, nk1 - 1))),
                  pl.BlockSpec((tm, tk), lambda i, k: (i, jnp.clip(k - nk1, 0, nk2 - 1))),
                  pl.BlockSpec((tk, d), lambda i, k: (jnp.minimum(k, nk - 1), 0)),
                  pl.BlockSpec((te, d), lambda i, k: (epi(i, k), 0)),
                  pl.BlockSpec((1, d), lambda i, k: (0, 0))],
        out_specs=pl.BlockSpec((te, d), lambda i, k: (epi(i, k), 0)),
        scratch_shapes=[pltpu.VMEM((tm, d), jnp.float32)],
        compiler_params=pltpu.CompilerParams(
            dimension_semantics=("parallel", "arbitrary"), vmem_limit_bytes=VMEM_LIMIT),
        name="out_proj",
    )(a1, a2, w, x2d, gain.reshape(1, d))


def kernel(x, w_in, attn_sinks, lb_logits, rnn_norm, w_out, pre_norm, post_norm):
    batch, seq, d_model = x.shape
    depth = w_in.shape[0]
    d_mix = w_out.shape[1]
    d_attn = d_mix // 2
    d_rnn = d_mix - d_attn
    x2d = x.reshape(batch * seq, d_model)
    for layer in range(depth):
        h = _prenorm(x2d, pre_norm[layer])
        proj = _in_proj(h, w_in[layer].astype(jnp.bfloat16))
        attn = _swa(proj, attn_sinks[layer], batch, seq, d_attn)
        rnn = _hgrn2(proj, lb_logits, rnn_norm[layer], batch, seq, d_attn, d_rnn, layer)
        x2d = _out_proj(attn, rnn, w_out[layer].astype(jnp.bfloat16), x2d, post_norm[layer])
    return x2d.reshape(batch, seq, d_model)
```

```python
import functools

import jax
import jax.numpy as jnp
import numpy as np
from jax import lax
from jax.experimental import pallas as pl
from jax.experimental.pallas import tpu as pltpu

ATTN_HEAD_DIM = 64
GQA_GROUP = 8
WINDOW = 128
RNN_HEAD_DIM = 128
NORM_EPS = 1e-6

LANES = 128
SUBLANES = 8
CHUNK = 128
N_LEVELS = 7
SAFE_LOG2 = 100.0
LOG2_E = 1.4426950408889634
VMEM_LIMIT = 56 * 1024 * 1024
OUT_COL_CHUNK = 1024
OUT_ROW_CHUNK = 32
OUT_EPILOGUE_ROWS = 128

_NT = (((1,), (1,)), ((), ()))
_TN = (((0,), (0,)), ((), ()))


def _skewed(n_items, stages, gap=2):
    carry = [None] * n_items
    for step in range(n_items + gap * (len(stages) - 1)):
        for j in reversed(range(len(stages))):
            i = step - gap * j
            if 0 <= i < n_items:
                carry[i] = stages[j](i, carry[i])


def _silu(x):
    h = 0.5 * x
    return h + h * jnp.tanh(h)


def _prenorm_kernel(x_ref, g_ref, o_ref):
    x = x_ref[...]
    ms = jnp.mean(x * x, axis=-1, keepdims=True)
    o_ref[...] = (x * lax.rsqrt(ms + NORM_EPS) * g_ref[...]).astype(o_ref.dtype)


def _prenorm(x2d, gain, rows=256):
    t, d = x2d.shape
    rows = min(rows, t)
    return pl.pallas_call(
        _prenorm_kernel,
        out_shape=jax.ShapeDtypeStruct((t, d), jnp.bfloat16),
        grid=(t // rows,),
        in_specs=[pl.BlockSpec((rows, d), lambda i: (i, 0)),
                  pl.BlockSpec((1, d), lambda i: (0, 0))],
        out_specs=pl.BlockSpec((rows, d), lambda i: (i, 0)),
        compiler_params=pltpu.CompilerParams(
            dimension_semantics=("parallel",), vmem_limit_bytes=VMEM_LIMIT),
        name="prenorm",
    )(x2d, gain.reshape(1, d))


def _in_proj_kernel(h_ref, wchunk_ref, w2_ref, o_ref, w2o_ref, wbuf_ref):
    jo = pl.program_id(0)
    i = pl.program_id(1)
    rows = wchunk_ref.shape[0]
    chunk = pl.ds(pl.multiple_of(i * rows, rows), rows)

    def casts(slot):
        wbuf_ref[slot, chunk, :] = wchunk_ref[...].astype(wbuf_ref.dtype)
        w2o_ref[...] = w2_ref[...].astype(w2o_ref.dtype)

    @pl.when(jo == 0)
    def _():
        casts(0)

    @pl.when(jo > 0)
    def _():
        casts(jo % 2)
        o_ref[...] = jnp.dot(h_ref[...], wbuf_ref[(jo - 1) % 2],
                             preferred_element_type=jnp.float32).astype(o_ref.dtype)


def _in_proj(h, w, w2, tm=1024, tn=1024, w2_col_blocks=16):
    t, d = h.shape
    n = w.shape[1]
    k2, d2 = w2.shape
    tm = min(tm, t)
    ni, nj = t // tm, n // tn
    assert d % ni == 0 and (d // ni) % 16 == 0 and k2 % ni == 0 and nj + 1 >= w2_col_blocks
    w2_blk = (k2 // ni, d2 // w2_col_blocks)

    def w2_index(jo, i):
        done = jo >= w2_col_blocks
        return (jnp.where(done, ni - 1, i), jnp.where(done, w2_col_blocks - 1, jo))

    return pl.pallas_call(
        _in_proj_kernel,
        out_shape=(jax.ShapeDtypeStruct((t, n), jnp.bfloat16),
                   jax.ShapeDtypeStruct((k2, d2), jnp.bfloat16)),
        grid=(nj + 1, ni),
        in_specs=[pl.BlockSpec((tm, d), lambda jo, i: (jnp.where(jo == 0, 0, i), 0)),
                  pl.BlockSpec((d // ni, tn), lambda jo, i: (i, jnp.minimum(jo, nj - 1))),
                  pl.BlockSpec(w2_blk, w2_index)],
        out_specs=(pl.BlockSpec((tm, tn),
                                lambda jo, i: (jnp.where(jo == 0, 0, i), jnp.maximum(jo - 1, 0))),
                   pl.BlockSpec(w2_blk, w2_index)),
        scratch_shapes=[pltpu.VMEM((2, d, tn), jnp.bfloat16)],
        compiler_params=pltpu.CompilerParams(
            dimension_semantics=("arbitrary", "arbitrary"), vmem_limit_bytes=VMEM_LIMIT),
        name="in_proj",
    )(h, w, w2)


def _lane_halves(slab, head_in_high_half):
    lane = lax.broadcasted_iota(jnp.int32, slab.shape, 1)
    swapped = pltpu.roll(slab, 64, axis=1)
    zero = jnp.zeros_like(slab)
    if head_in_high_half:
        lo, hi = swapped, slab
    else:
        lo, hi = slab, swapped
    return jnp.where(lane < 64, lo, zero), jnp.where(lane >= 64, hi, zero)


def _swa_kernel(sink_ref, q_ref, kvp_ref, kvc_ref, g0_ref, g1_ref, g2_ref, g3_ref, o_ref,
                *, n_kv_heads, d_kv):
    n = pl.program_id(1)
    gate_refs = (g0_ref, g1_ref, g2_ref, g3_ref)
    w = WINDOW
    qi = lax.broadcasted_iota(jnp.int32, (w, w), 0)
    kj = lax.broadcasted_iota(jnp.int32, (w, w), 1)
    lower = kj <= qi
    prev_bias = jnp.where(n > 0, 0.0, -jnp.inf)
    lane = lax.broadcasted_iota(jnp.int32, (w, LANES), 1)
    scale = ATTN_HEAD_DIM ** -0.5 * LOG2_E
    pairs_per_kv = GQA_GROUP // 2

    kv_cache = {}

    def kv_operands(h):
        if h not in kv_cache:
            c0 = (h // 2) * LANES
            high = (h % 2) == 1
            k_slab = jnp.concatenate([kvp_ref[:, c0:c0 + LANES], kvc_ref[:, c0:c0 + LANES]],
                                     axis=0).astype(jnp.float32)
            v_slab = jnp.concatenate([kvp_ref[:, d_kv + c0:d_kv + c0 + LANES],
                                      kvc_ref[:, d_kv + c0:d_kv + c0 + LANES]],
                                     axis=0).astype(jnp.float32)
            k_lo, k_hi = _lane_halves(k_slab * scale, high)
            v_lo, v_hi = _lane_halves(v_slab, high)
            kv_cache[h] = (jnp.concatenate([k_lo, k_hi], axis=0).astype(jnp.bfloat16),
                           jnp.concatenate([v_lo, v_hi], axis=0).astype(jnp.bfloat16))
        return kv_cache[h]

    def score_stage(p, _):
        qp = q_ref[:, p * LANES:(p + 1) * LANES]
        kcat = kv_operands(p // pairs_per_kv)[0]
        return lax.dot_general(qp, kcat, _NT, preferred_element_type=jnp.float32)

    def max_stage(p, s):
        merged, mx = [], []
        for hh in range(2):
            prev = s[:, hh * 2 * w:hh * 2 * w + w] + prev_bias
            cur = s[:, hh * 2 * w + w:(hh + 1) * 2 * w]
            sh = jnp.where(lower, cur, prev)
            merged.append(sh)
            mx.append(jnp.maximum(jnp.max(sh, axis=-1, keepdims=True), sink_ref[2 * p + hh] * LOG2_E))
        return merged, mx

    def exp_stage(p, carry):
        merged, mx = carry
        probs, denom = [], []
        for hh in range(2):
            e = jnp.exp2(merged[hh] - mx[hh])
            denom.append(jnp.sum(e, axis=-1, keepdims=True)
                         + jnp.exp2(sink_ref[2 * p + hh] * LOG2_E - mx[hh]))
            e = e.astype(jnp.bfloat16)
            zero = jnp.zeros_like(e)
            probs += [jnp.where(lower, zero, e), jnp.where(lower, e, zero)]
        return jnp.concatenate(probs, axis=1), denom

    def value_stage(p, carry):
        pcat, denom = carry
        vcat = kv_operands(p // pairs_per_kv)[1]
        return jnp.dot(pcat, vcat, preferred_element_type=jnp.float32), denom

    def out_stage(p, carry):
        o, denom = carry
        o = o * jnp.where(lane < 64, 1.0 / denom[0], 1.0 / denom[1])
        gref = gate_refs[p // 8]
        gate = gref[:, (p % 8) * LANES:(p % 8 + 1) * LANES].astype(jnp.float32)
        o_ref[:, p * LANES:(p + 1) * LANES] = (o * _silu(gate)).astype(o_ref.dtype)

    _skewed(n_kv_heads * pairs_per_kv,
            [score_stage, max_stage, exp_stage, value_stage, out_stage])


def _swa(proj, sinks, batch, seq, d_attn):
    t = proj.shape[0]
    w = WINDOW
    nb = seq // w
    n_q = d_attn // ATTN_HEAD_DIM
    n_kv = n_q // GQA_GROUP
    d_kv = n_kv * ATTN_HEAD_DIM
    kvw = 2 * d_kv
    assert d_attn == 4 * kvw
    kv_blk = d_attn // kvw
    gate_blk0 = (d_attn + kvw) // kvw

    def row(b, n):
        return b * nb + n

    in_specs = [
        pl.BlockSpec(memory_space=pltpu.SMEM),
        pl.BlockSpec((w, d_attn), lambda b, n: (row(b, n), 0)),
        pl.BlockSpec((w, kvw), lambda b, n: (row(b, jnp.maximum(n - 1, 0)), kv_blk)),
        pl.BlockSpec((w, kvw), lambda b, n: (row(b, n), kv_blk)),
    ] + [
        pl.BlockSpec((w, kvw), functools.partial(lambda b, n, j: (row(b, n), gate_blk0 + j), j=j))
        for j in range(4)
    ]
    return pl.pallas_call(
        functools.partial(_swa_kernel, n_kv_heads=n_kv, d_kv=d_kv),
        out_shape=jax.ShapeDtypeStruct((t, d_attn), jnp.bfloat16),
        grid=(batch, nb),
        in_specs=in_specs,
        out_specs=pl.BlockSpec((w, d_attn), lambda b, n: (row(b, n), 0)),
        compiler_params=pltpu.CompilerParams(
            dimension_semantics=("parallel", "parallel"), vmem_limit_bytes=VMEM_LIMIT),
        name="swa",
    )(sinks, proj, proj, proj, proj, proj, proj, proj)


def _decay_tables():
    c = CHUNK
    n = np.zeros((N_LEVELS + 1, c, c), np.float32)
    r = np.arange(c)
    for l in range(1, N_LEVELS):
        half = 1 << l
        m = 2 * half
        for t in range(c):
            start = (t // m) * m
            mid = start + half - 1
            if t > mid:
                n[l - 1, t] = (r > mid) & (r <= t)
            else:
                n[l - 1, t] = (r > t) & (r <= mid)
    n[N_LEVELS - 1] = r[None, :] <= r[:, None]
    n[N_LEVELS] = r[None, :] > r[:, None]
    cumsum = np.concatenate([n[N_LEVELS - 1], n[N_LEVELS - 1]], axis=1)
    n = n.reshape((N_LEVELS + 1) * c, c)
    ncat = np.concatenate([n, n], axis=1)
    tt, ss = np.meshgrid(r, r, indexing="ij")
    x = tt ^ ss
    level = np.full((c, c), -1, np.int32)
    for l in range(N_LEVELS):
        level[(ss < tt) & (x >= (1 << l)) & (x < (2 << l))] = l
    return ncat, cumsum, level


def _interleave_rows(k, q, half):
    pieces = []
    for r0 in range(0, k.shape[0], half):
        src = q if (r0 // half) % 2 else k
        pieces.append(src[r0:r0 + half])
    return jnp.concatenate(pieces, axis=0)


def _hgrn2_kernel(ncat_ref, cum_ref, level_ref, lbl_ref, gain_ref, rq_ref, rf_ref, ri_ref, rg_ref,
                  o_ref, state_ref, q_s, k_s, ghi_s, glo_s, *, heads, chunks, layer):
    c = CHUNK
    pw = 2 * LANES
    n_pairs = heads // 2

    @pl.when(pl.program_id(2) == 0)
    def _():
        state_ref[...] = jnp.zeros_like(state_ref)

    lbl = lbl_ref[...]
    e = jnp.exp(lbl - jnp.max(lbl, axis=0, keepdims=True))
    lb_all = jnp.sum(e[:layer + 1], axis=0, keepdims=True) / jnp.sum(e, axis=0, keepdims=True)
    fa_all = 0.5 * (1.0 + lb_all)
    fb_all = 0.5 * (1.0 - lb_all)

    worst = jnp.zeros((1, pw), jnp.float32)
    for ci in range(chunks):
        rows = slice(ci * c, (ci + 1) * c)
        for pr in range(n_pairs):
            cols = slice(pr * pw, (pr + 1) * pw)
            th = jnp.tanh(0.5 * rf_ref[rows, cols].astype(jnp.float32))
            bt = fb_all[:, cols] * th
            g2 = jnp.log(fa_all[:, cols] + bt) * LOG2_E
            k_s[rows, cols] = fb_all[:, cols] - bt
            q_s[rows, cols] = _silu(rq_ref[rows, cols].astype(jnp.float32))
            g_hi = g2.astype(jnp.bfloat16)
            ghi_s[rows, cols] = g_hi
            glo_s[rows, cols] = (g2 - g_hi.astype(jnp.float32)).astype(jnp.bfloat16)
            for half in range(2):
                hsum = jnp.sum(g2[half * (c // 2):(half + 1) * (c // 2)], axis=0, keepdims=True)
                worst = jnp.minimum(worst, hsum)
    mild = jnp.min(worst) > -SAFE_LOG2

    def head_matmuls(a_bf, qi, kl, v_bf, idx):
        intra = jnp.dot(a_bf, v_bf, preferred_element_type=jnp.float32)
        st = state_ref[idx]
        inter = lax.dot_general(qi, st.astype(jnp.bfloat16), _NT,
                                preferred_element_type=jnp.float32)
        upd = lax.dot_general(v_bf, kl, _TN, preferred_element_type=jnp.float32)
        return inter + intra, st, upd

    def head_tail(a_bf, qi, kl, egl, v_bf, idx):
        o, st, upd = head_matmuls(a_bf, qi, kl, v_bf, idx)
        state_ref[idx] = egl * st + upd
        return o

    def write_out(outs, rows, cols):
        normed = []
        for o in outs:
            ms = jnp.mean(o * o, axis=-1, keepdims=True)
            normed.append(o * lax.rsqrt(ms + NORM_EPS))
        o2 = jnp.concatenate(normed, axis=1) * gain_ref[:, cols]
        gate = rg_ref[rows, cols].astype(jnp.float32)
        o_ref[rows, cols] = (o2 * _silu(gate)).astype(o_ref.dtype)

    @pl.when(mild)
    def _():
        ti = lax.broadcasted_iota(jnp.int32, (c, c), 0)
        si = lax.broadcasted_iota(jnp.int32, (c, c), 1)
        causal = si <= ti
        halves = [slice(hh * LANES, (hh + 1) * LANES) for hh in range(2)]

        def where(item):
            ci, pr = divmod(item, n_pairs)
            return slice(ci * c, (ci + 1) * c), slice(pr * pw, (pr + 1) * pw), pr

        def cumsum_stage(item, _):
            rows, cols, _ = where(item)
            gcat = jnp.concatenate([ghi_s[rows, cols], glo_s[rows, cols]], axis=0)
            return jnp.dot(cum_ref[...], gcat, preferred_element_type=jnp.float32)

        def scale_stage(item, gsum):
            rows, cols, _ = where(item)
            ref = gsum[c // 2 - 1:c // 2, :]
            last = gsum[c - 1:c, :]
            d = gsum - ref
            qp = q_s[rows, cols] * jnp.exp2(d)
            kp = k_s[rows, cols] * jnp.exp2(-d)
            qi = (qp * jnp.exp2(ref)).astype(jnp.bfloat16)
            kl = (kp * jnp.exp2(last - ref)).astype(jnp.bfloat16)
            qp = qp.astype(jnp.bfloat16)
            kp = kp.astype(jnp.bfloat16)
            a = [lax.dot_general(qp[:, hs], kp[:, hs], _NT, preferred_element_type=jnp.float32)
                 for hs in halves]
            return a, qi, kl, jnp.exp2(last)

        def mix_stage(item, carry):
            a, qi, kl, egl = carry
            rows, cols, pr = where(item)
            v_bf = ri_ref[rows, cols]
            return [head_matmuls(jnp.where(causal, a[hh], 0.0).astype(jnp.bfloat16),
                                 qi[:, hs], kl[:, hs], v_bf[:, hs], 2 * pr + hh)
                    for hh, hs in enumerate(halves)], egl

        def out_stage(item, carry):
            heads_out, egl = carry
            rows, cols, pr = where(item)
            for hh, hs in enumerate(halves):
                _, st, upd = heads_out[hh]
                state_ref[2 * pr + hh] = egl[:, hs] * st + upd
            write_out([o for o, _, _ in heads_out], rows, cols)

        _skewed(chunks * n_pairs, [cumsum_stage, scale_stage, mix_stage, out_stage])

    @pl.when(jnp.logical_not(mild))
    def _():
        level = level_ref[...]
        trow = lax.broadcasted_iota(jnp.int32, (c, pw), 0)

        def chunk_body(ci, carry):
            rows = pl.ds(pl.multiple_of(ci * c, c), c)
            for pr in range(n_pairs):
                cols = slice(pr * pw, (pr + 1) * pw)
                q = q_s[rows, cols]
                k = k_s[rows, cols]
                g_hi = ghi_s[rows, cols]
                g_lo = glo_s[rows, cols]
                f = jnp.exp2(g_hi.astype(jnp.float32) + g_lo.astype(jnp.float32))
                gcat = jnp.concatenate([g_hi, g_lo], axis=0)
                ex = jnp.dot(ncat_ref[...], gcat, preferred_element_type=jnp.float32)

                z = [jnp.where((trow & 1) == 1, q * f, k).astype(jnp.bfloat16)]
                for l in range(1, N_LEVELS):
                    half = 1 << l
                    if half < SUBLANES:
                        qk = jnp.where((trow & half) != 0, q, k)
                    else:
                        qk = _interleave_rows(k, q, half)
                    z.append((qk * jnp.exp2(ex[(l - 1) * c:l * c])).astype(jnp.bfloat16))
                eg = jnp.exp2(ex[(N_LEVELS - 1) * c:N_LEVELS * c])
                erest = jnp.exp2(ex[N_LEVELS * c:(N_LEVELS + 1) * c])
                qi = (q * eg).astype(jnp.bfloat16)
                kl = (k * erest).astype(jnp.bfloat16)
                qk_diag = q * k
                v_bf = ri_ref[rows, cols]
                outs = []
                for hh in range(2):
                    hs = slice(hh * LANES, (hh + 1) * LANES)
                    attn = jnp.zeros((c, c), jnp.float32)
                    for l in range(N_LEVELS):
                        zl = z[l][:, hs]
                        a = lax.dot_general(zl, zl, _NT, preferred_element_type=jnp.float32)
                        attn = jnp.where(level == l, a, attn)
                    diag = jnp.sum(qk_diag[:, hs], axis=-1, keepdims=True)
                    o = head_tail(attn.astype(jnp.bfloat16), qi[:, hs], kl[:, hs],
                                  eg[c - 1:c, hs], v_bf[:, hs], 2 * pr + hh)
                    outs.append(o + diag * v_bf[:, hs].astype(jnp.float32))
                write_out(outs, rows, cols)
            return carry

        lax.fori_loop(0, chunks, chunk_body, 0)


def _hgrn2(proj, lb_logits, gain, batch, seq, d_attn, d_rnn, layer, heads=4, chunks=8):
    t = proj.shape[0]
    rows = chunks * CHUNK
    if seq % rows:
        chunks, rows = 1, CHUNK
    nl = seq // rows
    wb = heads * RNN_HEAD_DIM
    n_hg = d_rnn // wb
    kvw = d_attn // 4
    base = d_attn + kvw + d_attn
    assert base % wb == 0 and heads % 2 == 0
    ncat, cumsum, level = _decay_tables()

    def seg(i):
        blk0 = (base + i * d_rnn) // wb
        return pl.BlockSpec((rows, wb), lambda b, hg, l: (b * nl + l, blk0 + hg))

    const = lambda shape: pl.BlockSpec(shape, lambda b, hg, l: (0, 0))
    return pl.pallas_call(
        functools.partial(_hgrn2_kernel, heads=heads, chunks=chunks, layer=layer),
        out_shape=jax.ShapeDtypeStruct((t, d_rnn), jnp.bfloat16),
        grid=(batch, n_hg, nl),
        in_specs=[
            const(ncat.shape), const(cumsum.shape), const(level.shape),
            pl.BlockSpec((lb_logits.shape[0], wb), lambda b, hg, l: (0, hg)),
            pl.BlockSpec((1, wb), lambda b, hg, l: (0, hg)),
            seg(0), seg(1), seg(2), seg(3),
        ],
        out_specs=pl.BlockSpec((rows, wb), lambda b, hg, l: (b * nl + l, hg)),
        scratch_shapes=[pltpu.VMEM((heads, RNN_HEAD_DIM, RNN_HEAD_DIM), jnp.float32),
                        pltpu.VMEM((rows, wb), jnp.float32),
                        pltpu.VMEM((rows, wb), jnp.float32),
                        pltpu.VMEM((rows, wb), jnp.bfloat16),
                        pltpu.VMEM((rows, wb), jnp.bfloat16)],
        compiler_params=pltpu.CompilerParams(
            dimension_semantics=("parallel", "parallel", "arbitrary"),
            vmem_limit_bytes=VMEM_LIMIT),
        name="hgrn2",
    )(jnp.asarray(ncat, jnp.bfloat16), jnp.asarray(cumsum, jnp.bfloat16), jnp.asarray(level),
      lb_logits, gain.reshape(1, d_rnn), proj, proj, proj, proj)


def _out_proj_kernel(a1_ref, a2_ref, w_ref, x_ref, g_ref, o_ref, acc_ref, *, nk1, nk):
    kk = pl.program_id(1)
    d = acc_ref.shape[1]

    @pl.when(kk == 0)
    def _():
        acc_ref[...] = jnp.zeros_like(acc_ref)

    def accumulate(a_ref):
        for n0 in range(0, d, OUT_COL_CHUNK):
            cols = slice(n0, n0 + OUT_COL_CHUNK)
            acc_ref[:, cols] += jnp.dot(a_ref[...], w_ref[:, cols],
                                        preferred_element_type=jnp.float32)

    @pl.when(kk < nk1)
    def _():
        accumulate(a1_ref)

    @pl.when((kk >= nk1) & (kk < nk))
    def _():
        accumulate(a2_ref)

    @pl.when(kk >= nk)
    def _():
        rows_per_step = o_ref.shape[0]
        base = (kk - nk) * rows_per_step

        def row_body(ri, carry):
            off = pl.multiple_of(ri * OUT_ROW_CHUNK, OUT_ROW_CHUNK)
            y = acc_ref[pl.ds(pl.multiple_of(base + off, OUT_ROW_CHUNK), OUT_ROW_CHUNK), :]
            ms = jnp.mean(y * y, axis=-1, keepdims=True)
            rows = pl.ds(off, OUT_ROW_CHUNK)
            o_ref[rows, :] = x_ref[rows, :] + y * lax.rsqrt(ms + NORM_EPS) * g_ref[...]
            return carry

        lax.fori_loop(0, rows_per_step // OUT_ROW_CHUNK, row_body, 0)


def _out_proj(a1, a2, w, x2d, gain, tm=1024, tk=1024):
    t, k1 = a1.shape
    k2 = a2.shape[1]
    d = w.shape[1]
    tm = min(tm, t)
    nk1, nk2 = k1 // tk, k2 // tk
    nk = nk1 + nk2
    te = min(OUT_EPILOGUE_ROWS, tm)
    ne = tm // te

    def epi(i, k):
        return i * ne + jnp.maximum(k - nk, 0)

    return pl.pallas_call(
        functools.partial(_out_proj_kernel, nk1=nk1, nk=nk),
        out_shape=jax.ShapeDtypeStruct((t, d), jnp.float32),
        grid=(t // tm, nk + ne),
        in_specs=[pl.BlockSpec((tm, tk), lambda i, k: (i, jnp.minimum(k, nk1 - 1))),
                  pl.BlockSpec((tm, tk), lambda i, k: (i, jnp.clip(k - nk1, 0, nk2 - 1))),
                  pl.BlockSpec((tk, d), lambda i, k: (jnp.minimum(k, nk - 1), 0)),
                  pl.BlockSpec((te, d), lambda i, k: (epi(i, k), 0)),
                  pl.BlockSpec((1, d), lambda i, k: (0, 0))],
        out_specs=pl.BlockSpec((te, d), lambda i, k: (epi(i, k), 0)),
        scratch_shapes=[pltpu.VMEM((tm, d), jnp.float32)],
        compiler_params=pltpu.CompilerParams(
            dimension_semantics=("parallel", "arbitrary"), vmem_limit_bytes=VMEM_LIMIT),
        name="out_proj",
    )(a1, a2, w, x2d, gain.reshape(1, d))


def kernel(x, w_in, attn_sinks, lb_logits, rnn_norm, w_out, pre_norm, post_norm):
    batch, seq, d_model = x.shape
    depth = w_in.shape[0]
    d_mix = w_out.shape[1]
    d_attn = d_mix // 2
    d_rnn = d_mix - d_attn
    x2d = x.reshape(batch * seq, d_model)
    for layer in range(depth):
        h = _prenorm(x2d, pre_norm[layer])
        proj, w_out_bf = _in_proj(h, w_in[layer], w_out[layer])
        attn = _swa(proj, attn_sinks[layer], batch, seq, d_attn)
        rnn = _hgrn2(proj, lb_logits, rnn_norm[layer], batch, seq, d_attn, d_rnn, layer)
        x2d = _out_proj(attn, rnn, w_out_bf, x2d, post_norm[layer])
    return x2d.reshape(batch, seq, d_model)
```

```python
import functools

import jax
import jax.numpy as jnp
import numpy as np
from jax import lax
from jax.experimental import pallas as pl
from jax.experimental.pallas import tpu as pltpu

ATTN_HEAD_DIM = 64
GQA_GROUP = 8
WINDOW = 128
RNN_HEAD_DIM = 128
NORM_EPS = 1e-6

LANES = 128
SUBLANES = 8
CHUNK = 128
N_LEVELS = 7
SAFE_LOG2 = 100.0
LOG2_E = 1.4426950408889634
VMEM_LIMIT = 56 * 1024 * 1024
OUT_COL_CHUNK = 1024
OUT_ROW_CHUNK = 32
OUT_EPILOGUE_ROWS = 128

_NT = (((1,), (1,)), ((), ()))
_TN = (((0,), (0,)), ((), ()))


def _skewed(n_items, stages, gap=2):
    carry = [None] * n_items
    for step in range(n_items + gap * (len(stages) - 1)):
        for j in reversed(range(len(stages))):
            i = step - gap * j
            if 0 <= i < n_items:
                carry[i] = stages[j](i, carry[i])


def _silu(x):
    h = 0.5 * x
    return h + h * jnp.tanh(h)


def _prenorm_kernel(x_ref, g_ref, o_ref):
    x = x_ref[...]
    ms = jnp.mean(x * x, axis=-1, keepdims=True)
    o_ref[...] = (x * lax.rsqrt(ms + NORM_EPS) * g_ref[...]).astype(o_ref.dtype)


def _prenorm(x2d, gain, rows=512):
    t, d = x2d.shape
    rows = min(rows, t)
    return pl.pallas_call(
        _prenorm_kernel,
        out_shape=jax.ShapeDtypeStruct((t, d), jnp.bfloat16),
        grid=(t // rows,),
        in_specs=[pl.BlockSpec((rows, d), lambda i: (i, 0)),
                  pl.BlockSpec((1, d), lambda i: (0, 0))],
        out_specs=pl.BlockSpec((rows, d), lambda i: (i, 0)),
        compiler_params=pltpu.CompilerParams(
            dimension_semantics=("parallel",), vmem_limit_bytes=VMEM_LIMIT),
        name="prenorm",
    )(x2d, gain.reshape(1, d))


def _in_proj_kernel(h_ref, wchunk_ref, w2_ref, o_ref, w2o_ref, wbuf_ref):
    jo = pl.program_id(0)
    i = pl.program_id(1)
    rows = wchunk_ref.shape[0]
    chunk = pl.ds(pl.multiple_of(i * rows, rows), rows)

    def casts(slot):
        wbuf_ref[slot, chunk, :] = wchunk_ref[...].astype(wbuf_ref.dtype)
        w2o_ref[...] = w2_ref[...].astype(w2o_ref.dtype)

    @pl.when(jo == 0)
    def _():
        casts(0)

    @pl.when(jo > 0)
    def _():
        casts(jo % 2)
        o_ref[...] = jnp.dot(h_ref[...], wbuf_ref[(jo - 1) % 2],
                             preferred_element_type=jnp.float32).astype(o_ref.dtype)


def _in_proj(h, w, w2, tm=1024, tn=1024, w2_col_blocks=16):
    t, d = h.shape
    n = w.shape[1]
    k2, d2 = w2.shape
    tm = min(tm, t)
    ni, nj = t // tm, n // tn
    assert d % ni == 0 and (d // ni) % 16 == 0 and k2 % ni == 0 and nj + 1 >= w2_col_blocks
    w2_blk = (k2 // ni, d2 // w2_col_blocks)

    def w2_index(jo, i):
        done = jo >= w2_col_blocks
        return (jnp.where(done, ni - 1, i), jnp.where(done, w2_col_blocks - 1, jo))

    return pl.pallas_call(
        _in_proj_kernel,
        out_shape=(jax.ShapeDtypeStruct((t, n), jnp.bfloat16),
                   jax.ShapeDtypeStruct((k2, d2), jnp.bfloat16)),
        grid=(nj + 1, ni),
        in_specs=[pl.BlockSpec((tm, d), lambda jo, i: (jnp.where(jo == 0, 0, i), 0)),
                  pl.BlockSpec((d // ni, tn), lambda jo, i: (i, jnp.minimum(jo, nj - 1))),
                  pl.BlockSpec(w2_blk, w2_index)],
        out_specs=(pl.BlockSpec((tm, tn),
                                lambda jo, i: (jnp.where(jo == 0, 0, i), jnp.maximum(jo - 1, 0))),
                   pl.BlockSpec(w2_blk, w2_index)),
        scratch_shapes=[pltpu.VMEM((2, d, tn), jnp.bfloat16)],
        compiler_params=pltpu.CompilerParams(
            dimension_semantics=("arbitrary", "arbitrary"), vmem_limit_bytes=VMEM_LIMIT),
        name="in_proj",
    )(h, w, w2)


def _lane_halves(slab, head_in_high_half):
    lane = lax.broadcasted_iota(jnp.int32, slab.shape, 1)
    swapped = pltpu.roll(slab, 64, axis=1)
    zero = jnp.zeros_like(slab)
    if head_in_high_half:
        lo, hi = swapped, slab
    else:
        lo, hi = slab, swapped
    return jnp.where(lane < 64, lo, zero), jnp.where(lane >= 64, hi, zero)


def _swa_kernel(sink_ref, q_ref, kvp_ref, kvc_ref, g0_ref, g1_ref, g2_ref, g3_ref, o_ref,
                *, n_kv_heads, d_kv):
    n = pl.program_id(1)
    gate_refs = (g0_ref, g1_ref, g2_ref, g3_ref)
    w = WINDOW
    qi = lax.broadcasted_iota(jnp.int32, (w, w), 0)
    kj = lax.broadcasted_iota(jnp.int32, (w, w), 1)
    lower = kj <= qi
    prev_bias = jnp.where(n > 0, 0.0, -jnp.inf)
    lane = lax.broadcasted_iota(jnp.int32, (w, LANES), 1)
    scale = ATTN_HEAD_DIM ** -0.5 * LOG2_E
    pairs_per_kv = GQA_GROUP // 2

    kv_cache = {}

    def kv_operands(h):
        if h not in kv_cache:
            c0 = (h // 2) * LANES
            high = (h % 2) == 1
            k_slab = jnp.concatenate([kvp_ref[:, c0:c0 + LANES], kvc_ref[:, c0:c0 + LANES]],
                                     axis=0).astype(jnp.float32)
            v_slab = jnp.concatenate([kvp_ref[:, d_kv + c0:d_kv + c0 + LANES],
                                      kvc_ref[:, d_kv + c0:d_kv + c0 + LANES]],
                                     axis=0).astype(jnp.float32)
            k_lo, k_hi = _lane_halves(k_slab * scale, high)
            v_lo, v_hi = _lane_halves(v_slab, high)
            kv_cache[h] = (jnp.concatenate([k_lo, k_hi], axis=0).astype(jnp.bfloat16),
                           jnp.concatenate([v_lo, v_hi], axis=0).astype(jnp.bfloat16))
        return kv_cache[h]

    def score_stage(p, _):
        qp = q_ref[:, p * LANES:(p + 1) * LANES]
        kcat = kv_operands(p // pairs_per_kv)[0]
        return lax.dot_general(qp, kcat, _NT, preferred_element_type=jnp.float32)

    def max_stage(p, s):
        merged, mx = [], []
        for hh in range(2):
            prev = s[:, hh * 2 * w:hh * 2 * w + w] + prev_bias
            cur = s[:, hh * 2 * w + w:(hh + 1) * 2 * w]
            sh = jnp.where(lower, cur, prev)
            merged.append(sh)
            mx.append(jnp.maximum(jnp.max(sh, axis=-1, keepdims=True), sink_ref[2 * p + hh] * LOG2_E))
        return merged, mx

    def exp_stage(p, carry):
        merged, mx = carry
        probs, denom = [], []
        for hh in range(2):
            e = jnp.exp2(merged[hh] - mx[hh])
            denom.append(jnp.sum(e, axis=-1, keepdims=True)
                         + jnp.exp2(sink_ref[2 * p + hh] * LOG2_E - mx[hh]))
            e = e.astype(jnp.bfloat16)
            zero = jnp.zeros_like(e)
            probs += [jnp.where(lower, zero, e), jnp.where(lower, e, zero)]
        return jnp.concatenate(probs, axis=1), denom

    def value_stage(p, carry):
        pcat, denom = carry
        vcat = kv_operands(p // pairs_per_kv)[1]
        return jnp.dot(pcat, vcat, preferred_element_type=jnp.float32), denom

    def out_stage(p, carry):
        o, denom = carry
        o = o * jnp.where(lane < 64, 1.0 / denom[0], 1.0 / denom[1])
        gref = gate_refs[p // 8]
        gate = gref[:, (p % 8) * LANES:(p % 8 + 1) * LANES].astype(jnp.float32)
        o_ref[:, p * LANES:(p + 1) * LANES] = (o * _silu(gate)).astype(o_ref.dtype)

    _skewed(n_kv_heads * pairs_per_kv,
            [score_stage, max_stage, exp_stage, value_stage, out_stage])


def _swa(proj, sinks, batch, seq, d_attn):
    t = proj.shape[0]
    w = WINDOW
    nb = seq // w
    n_q = d_attn // ATTN_HEAD_DIM
    n_kv = n_q // GQA_GROUP
    d_kv = n_kv * ATTN_HEAD_DIM
    kvw = 2 * d_kv
    assert d_attn == 4 * kvw
    kv_blk = d_attn // kvw
    gate_blk0 = (d_attn + kvw) // kvw

    def row(b, n):
        return b * nb + n

    in_specs = [
        pl.BlockSpec(memory_space=pltpu.SMEM),
        pl.BlockSpec((w, d_attn), lambda b, n: (row(b, n), 0)),
        pl.BlockSpec((w, kvw), lambda b, n: (row(b, jnp.maximum(n - 1, 0)), kv_blk)),
        pl.BlockSpec((w, kvw), lambda b, n: (row(b, n), kv_blk)),
    ] + [
        pl.BlockSpec((w, kvw), functools.partial(lambda b, n, j: (row(b, n), gate_blk0 + j), j=j))
        for j in range(4)
    ]
    return pl.pallas_call(
        functools.partial(_swa_kernel, n_kv_heads=n_kv, d_kv=d_kv),
        out_shape=jax.ShapeDtypeStruct((t, d_attn), jnp.bfloat16),
        grid=(batch, nb),
        in_specs=in_specs,
        out_specs=pl.BlockSpec((w, d_attn), lambda b, n: (row(b, n), 0)),
        compiler_params=pltpu.CompilerParams(
            dimension_semantics=("parallel", "parallel"), vmem_limit_bytes=VMEM_LIMIT),
        name="swa",
    )(sinks, proj, proj, proj, proj, proj, proj, proj)


def _decay_tables():
    c = CHUNK
    n = np.zeros((N_LEVELS + 1, c, c), np.float32)
    r = np.arange(c)
    for l in range(1, N_LEVELS):
        half = 1 << l
        m = 2 * half
        for t in range(c):
            start = (t // m) * m
            mid = start + half - 1
            if t > mid:
                n[l - 1, t] = (r > mid) & (r <= t)
            else:
                n[l - 1, t] = (r > t) & (r <= mid)
    n[N_LEVELS - 1] = r[None, :] <= r[:, None]
    n[N_LEVELS] = r[None, :] > r[:, None]
    cumsum = np.concatenate([n[N_LEVELS - 1], n[N_LEVELS - 1]], axis=1)
    n = n.reshape((N_LEVELS + 1) * c, c)
    ncat = np.concatenate([n, n], axis=1)
    tt, ss = np.meshgrid(r, r, indexing="ij")
    x = tt ^ ss
    level = np.full((c, c), -1, np.int32)
    for l in range(N_LEVELS):
        level[(ss < tt) & (x >= (1 << l)) & (x < (2 << l))] = l
    return ncat, cumsum, level


def _interleave_rows(k, q, half):
    pieces = []
    for r0 in range(0, k.shape[0], half):
        src = q if (r0 // half) % 2 else k
        pieces.append(src[r0:r0 + half])
    return jnp.concatenate(pieces, axis=0)


def _hgrn2_kernel(ncat_ref, cum_ref, level_ref, lbl_ref, gain_ref, rq_ref, rf_ref, ri_ref, rg_ref,
                  o_ref, state_ref, q_s, k_s, ghi_s, glo_s, *, heads, chunks, layer):
    c = CHUNK
    pw = 2 * LANES
    n_pairs = heads // 2

    @pl.when(pl.program_id(2) == 0)
    def _():
        state_ref[...] = jnp.zeros_like(state_ref)

    lbl = lbl_ref[...]
    e = jnp.exp(lbl - jnp.max(lbl, axis=0, keepdims=True))
    lb_all = jnp.sum(e[:layer + 1], axis=0, keepdims=True) / jnp.sum(e, axis=0, keepdims=True)
    fa_all = 0.5 * (1.0 + lb_all)
    fb_all = 0.5 * (1.0 - lb_all)

    worst = jnp.zeros((1, pw), jnp.float32)
    for ci in range(chunks):
        rows = slice(ci * c, (ci + 1) * c)
        for pr in range(n_pairs):
            cols = slice(pr * pw, (pr + 1) * pw)
            th = jnp.tanh(0.5 * rf_ref[rows, cols].astype(jnp.float32))
            bt = fb_all[:, cols] * th
            g2 = jnp.log(fa_all[:, cols] + bt) * LOG2_E
            k_s[rows, cols] = fb_all[:, cols] - bt
            q_s[rows, cols] = _silu(rq_ref[rows, cols].astype(jnp.float32))
            g_hi = g2.astype(jnp.bfloat16)
            ghi_s[rows, cols] = g_hi
            glo_s[rows, cols] = (g2 - g_hi.astype(jnp.float32)).astype(jnp.bfloat16)
            for half in range(2):
                hsum = jnp.sum(g2[half * (c // 2):(half + 1) * (c // 2)], axis=0, keepdims=True)
                worst = jnp.minimum(worst, hsum)
    mild = jnp.min(worst) > -SAFE_LOG2

    def head_matmuls(a_bf, qi, kl, v_bf, idx):
        intra = jnp.dot(a_bf, v_bf, preferred_element_type=jnp.float32)
        st = state_ref[idx]
        inter = lax.dot_general(qi, st.astype(jnp.bfloat16), _NT,
                                preferred_element_type=jnp.float32)
        upd = lax.dot_general(v_bf, kl, _TN, preferred_element_type=jnp.float32)
        return inter + intra, st, upd

    def head_tail(a_bf, qi, kl, egl, v_bf, idx):
        o, st, upd = head_matmuls(a_bf, qi, kl, v_bf, idx)
        state_ref[idx] = egl * st + upd
        return o

    def write_out(outs, rows, cols):
        normed = []
        for o in outs:
            ms = jnp.mean(o * o, axis=-1, keepdims=True)
            normed.append(o * lax.rsqrt(ms + NORM_EPS))
        o2 = jnp.concatenate(normed, axis=1) * gain_ref[:, cols]
        gate = rg_ref[rows, cols].astype(jnp.float32)
        o_ref[rows, cols] = (o2 * _silu(gate)).astype(o_ref.dtype)

    @pl.when(mild)
    def _():
        ti = lax.broadcasted_iota(jnp.int32, (c, c), 0)
        si = lax.broadcasted_iota(jnp.int32, (c, c), 1)
        causal = si <= ti
        halves = [slice(hh * LANES, (hh + 1) * LANES) for hh in range(2)]

        def where(item):
            ci, pr = divmod(item, n_pairs)
            return slice(ci * c, (ci + 1) * c), slice(pr * pw, (pr + 1) * pw), pr

        def cumsum_stage(item, _):
            rows, cols, _ = where(item)
            gcat = jnp.concatenate([ghi_s[rows, cols], glo_s[rows, cols]], axis=0)
            return jnp.dot(cum_ref[...], gcat, preferred_element_type=jnp.float32)

        def scale_stage(item, gsum):
            rows, cols, _ = where(item)
            ref = gsum[c // 2 - 1:c // 2, :]
            last = gsum[c - 1:c, :]
            d = gsum - ref
            qp = q_s[rows, cols] * jnp.exp2(d)
            kp = k_s[rows, cols] * jnp.exp2(-d)
            qi = (qp * jnp.exp2(ref)).astype(jnp.bfloat16)
            kl = (kp * jnp.exp2(last - ref)).astype(jnp.bfloat16)
            qp = qp.astype(jnp.bfloat16)
            kp = kp.astype(jnp.bfloat16)
            a = [lax.dot_general(qp[:, hs], kp[:, hs], _NT, preferred_element_type=jnp.float32)
                 for hs in halves]
            return a, qi, kl, jnp.exp2(last)

        def mix_stage(item, carry):
            a, qi, kl, egl = carry
            rows, cols, pr = where(item)
            v_bf = ri_ref[rows, cols]
            return [head_matmuls(jnp.where(causal, a[hh], 0.0).astype(jnp.bfloat16),
                                 qi[:, hs], kl[:, hs], v_bf[:, hs], 2 * pr + hh)
                    for hh, hs in enumerate(halves)], egl

        def out_stage(item, carry):
            heads_out, egl = carry
            rows, cols, pr = where(item)
            for hh, hs in enumerate(halves):
                _, st, upd = heads_out[hh]
                state_ref[2 * pr + hh] = egl[:, hs] * st + upd
            write_out([o for o, _, _ in heads_out], rows, cols)

        _skewed(chunks * n_pairs, [cumsum_stage, scale_stage, mix_stage, out_stage])

    @pl.when(jnp.logical_not(mild))
    def _():
        level = level_ref[...]
        trow = lax.broadcasted_iota(jnp.int32, (c, pw), 0)

        def chunk_body(ci, carry):
            rows = pl.ds(pl.multiple_of(ci * c, c), c)
            for pr in range(n_pairs):
                cols = slice(pr * pw, (pr + 1) * pw)
                q = q_s[rows, cols]
                k = k_s[rows, cols]
                g_hi = ghi_s[rows, cols]
                g_lo = glo_s[rows, cols]
                f = jnp.exp2(g_hi.astype(jnp.float32) + g_lo.astype(jnp.float32))
                gcat = jnp.concatenate([g_hi, g_lo], axis=0)
                ex = jnp.dot(ncat_ref[...], gcat, preferred_element_type=jnp.float32)

                z = [jnp.where((trow & 1) == 1, q * f, k).astype(jnp.bfloat16)]
                for l in range(1, N_LEVELS):
                    half = 1 << l
                    if half < SUBLANES:
                        qk = jnp.where((trow & half) != 0, q, k)
                    else:
                        qk = _interleave_rows(k, q, half)
                    z.append((qk * jnp.exp2(ex[(l - 1) * c:l * c])).astype(jnp.bfloat16))
                eg = jnp.exp2(ex[(N_LEVELS - 1) * c:N_LEVELS * c])
                erest = jnp.exp2(ex[N_LEVELS * c:(N_LEVELS + 1) * c])
                qi = (q * eg).astype(jnp.bfloat16)
                kl = (k * erest).astype(jnp.bfloat16)
                qk_diag = q * k
                v_bf = ri_ref[rows, cols]
                outs = []
                for hh in range(2):
                    hs = slice(hh * LANES, (hh + 1) * LANES)
                    attn = jnp.zeros((c, c), jnp.float32)
                    for l in range(N_LEVELS):
                        zl = z[l][:, hs]
                        a = lax.dot_general(zl, zl, _NT, preferred_element_type=jnp.float32)
                        attn = jnp.where(level == l, a, attn)
                    diag = jnp.sum(qk_diag[:, hs], axis=-1, keepdims=True)
                    o = head_tail(attn.astype(jnp.bfloat16), qi[:, hs], kl[:, hs],
                                  eg[c - 1:c, hs], v_bf[:, hs], 2 * pr + hh)
                    outs.append(o + diag * v_bf[:, hs].astype(jnp.float32))
                write_out(outs, rows, cols)
            return carry

        lax.fori_loop(0, chunks, chunk_body, 0)


def _hgrn2(proj, lb_logits, gain, batch, seq, d_attn, d_rnn, layer, heads=8, chunks=8):
    t = proj.shape[0]
    rows = chunks * CHUNK
    if seq % rows:
        chunks, rows = 1, CHUNK
    nl = seq // rows
    wb = heads * RNN_HEAD_DIM
    n_hg = d_rnn // wb
    kvw = d_attn // 4
    base = d_attn + kvw + d_attn
    assert base % wb == 0 and heads % 2 == 0
    ncat, cumsum, level = _decay_tables()

    def seg(i):
        blk0 = (base + i * d_rnn) // wb
        return pl.BlockSpec((rows, wb), lambda b, hg, l: (b * nl + l, blk0 + hg))

    const = lambda shape: pl.BlockSpec(shape, lambda b, hg, l: (0, 0))
    return pl.pallas_call(
        functools.partial(_hgrn2_kernel, heads=heads, chunks=chunks, layer=layer),
        out_shape=jax.ShapeDtypeStruct((t, d_rnn), jnp.bfloat16),
        grid=(batch, n_hg, nl),
        in_specs=[
            const(ncat.shape), const(cumsum.shape), const(level.shape),
            pl.BlockSpec((lb_logits.shape[0], wb), lambda b, hg, l: (0, hg)),
            pl.BlockSpec((1, wb), lambda b, hg, l: (0, hg)),
            seg(0), seg(1), seg(2), seg(3),
        ],
        out_specs=pl.BlockSpec((rows, wb), lambda b, hg, l: (b * nl + l, hg)),
        scratch_shapes=[pltpu.VMEM((heads, RNN_HEAD_DIM, RNN_HEAD_DIM), jnp.float32),
                        pltpu.VMEM((rows, wb), jnp.float32),
                        pltpu.VMEM((rows, wb), jnp.float32),
                        pltpu.VMEM((rows, wb), jnp.bfloat16),
                        pltpu.VMEM((rows, wb), jnp.bfloat16)],
        compiler_params=pltpu.CompilerParams(
            dimension_semantics=("parallel", "parallel", "arbitrary"),
            vmem_limit_bytes=VMEM_LIMIT),
        name="hgrn2",
    )(jnp.asarray(ncat, jnp.bfloat16), jnp.asarray(cumsum, jnp.bfloat16), jnp.asarray(level),
      lb_logits, gain.reshape(1, d_rnn), proj, proj, proj, proj)


def _out_proj_kernel(a1_ref, a2_ref, w_ref, x_ref, g_ref, o_ref, acc_ref, *, nk1, nk):
    kk = pl.program_id(1)
    d = acc_ref.shape[1]

    def accumulate(a_ref, first=False):
        for n0 in range(0, d, OUT_COL_CHUNK):
            cols = slice(n0, n0 + OUT_COL_CHUNK)
            part = jnp.dot(a_ref[...], w_ref[:, cols], preferred_element_type=jnp.float32)
            acc_ref[:, cols] = part if first else acc_ref[:, cols] + part

    @pl.when(kk == 0)
    def _():
        accumulate(a1_ref, first=True)

    @pl.when((kk > 0) & (kk < nk1))
    def _():
        accumulate(a1_ref)

    @pl.when((kk >= nk1) & (kk < nk))
    def _():
        accumulate(a2_ref)

    @pl.when(kk >= nk)
    def _():
        rows_per_step = o_ref.shape[0]
        base = (kk - nk) * rows_per_step

        def row_body(ri, carry):
            off = pl.multiple_of(ri * OUT_ROW_CHUNK, OUT_ROW_CHUNK)
            y = acc_ref[pl.ds(pl.multiple_of(base + off, OUT_ROW_CHUNK), OUT_ROW_CHUNK), :]
            ms = jnp.mean(y * y, axis=-1, keepdims=True)
            rows = pl.ds(off, OUT_ROW_CHUNK)
            o_ref[rows, :] = x_ref[rows, :] + y * lax.rsqrt(ms + NORM_EPS) * g_ref[...]
            return carry

        lax.fori_loop(0, rows_per_step // OUT_ROW_CHUNK, row_body, 0)


def _out_proj(a1, a2, w, x2d, gain, tm=1024, tk=1024):
    t, k1 = a1.shape
    k2 = a2.shape[1]
    d = w.shape[1]
    tm = min(tm, t)
    nk1, nk2 = k1 // tk, k2 // tk
    nk = nk1 + nk2
    te = min(OUT_EPILOGUE_ROWS, tm)
    ne = tm // te

    def epi(i, k):
        return i * ne + jnp.maximum(k - nk, 0)

    return pl.pallas_call(
        functools.partial(_out_proj_kernel, nk1=nk1, nk=nk),
        out_shape=jax.ShapeDtypeStruct((t, d), jnp.float32),
        grid=(t // tm, nk + ne),
        in_specs=[pl.BlockSpec((tm, tk), lambda i, k: (i, jnp.minimum(k, nk1 - 1))),
                  pl.BlockSpec((tm, tk), lambda i, k: (i, jnp.clip(k - nk1, 0, nk2 - 1))),
                  pl.BlockSpec((tk, d), lambda i, k: (jnp.minimum(k, nk - 1), 0)),
                  pl.BlockSpec((te, d), lambda i, k: (epi(i, k), 0)),
                  pl.BlockSpec((1, d), lambda i, k: (0, 0))],
        out_specs=pl.BlockSpec((te, d), lambda i, k: (epi(i, k), 0)),
        scratch_shapes=[pltpu.VMEM((tm, d), jnp.float32)],
        compiler_params=pltpu.CompilerParams(
            dimension_semantics=("parallel", "arbitrary"), vmem_limit_bytes=VMEM_LIMIT),
        name="out_proj",
    )(a1, a2, w, x2d, gain.reshape(1, d))


def kernel(x, w_in, attn_sinks, lb_logits, rnn_norm, w_out, pre_norm, post_norm):
    batch, seq, d_model = x.shape
    depth = w_in.shape[0]
    d_mix = w_out.shape[1]
    d_attn = d_mix // 2
    d_rnn = d_mix - d_attn
    x2d = x.reshape(batch * seq, d_model)
    for layer in range(depth):
        h = _prenorm(x2d, pre_norm[layer])
        proj, w_out_bf = _in_proj(h, w_in[layer], w_out[layer])
        attn = _swa(proj, attn_sinks[layer], batch, seq, d_attn)
        rnn = _hgrn2(proj, lb_logits, rnn_norm[layer], batch, seq, d_attn, d_rnn, layer)
        x2d = _out_proj(attn, rnn, w_out_bf, x2d, post_norm[layer])
    return x2d.reshape(batch, seq, d_model)
```

```python
import functools

import jax
import jax.numpy as jnp
import numpy as np
from jax import lax
from jax.experimental import pallas as pl
from jax.experimental.pallas import tpu as pltpu

ATTN_HEAD_DIM = 64
GQA_GROUP = 8
WINDOW = 128
RNN_HEAD_DIM = 128
NORM_EPS = 1e-6

LANES = 128
SUBLANES = 8
CHUNK = 128
N_LEVELS = 7
SAFE_LOG2 = 100.0
LOG2_E = 1.4426950408889634
VMEM_LIMIT = 56 * 1024 * 1024
OUT_COL_CHUNK = 1024
OUT_ROW_CHUNK = 32
OUT_EPILOGUE_ROWS = 128

_NT = (((1,), (1,)), ((), ()))
_TN = (((0,), (0,)), ((), ()))


def _skewed(n_items, stages, gap=2):
    carry = [None] * n_items
    for step in range(n_items + gap * (len(stages) - 1)):
        for j in reversed(range(len(stages))):
            i = step - gap * j
            if 0 <= i < n_items:
                carry[i] = stages[j](i, carry[i])


def _silu(x):
    h = 0.5 * x
    return h + h * jnp.tanh(h)


def _prenorm_kernel(x_ref, g_ref, o_ref):
    x = x_ref[...]
    ms = jnp.mean(x * x, axis=-1, keepdims=True)
    o_ref[...] = (x * lax.rsqrt(ms + NORM_EPS) * g_ref[...]).astype(o_ref.dtype)


def _prenorm(x2d, gain, rows=512):
    t, d = x2d.shape
    rows = min(rows, t)
    return pl.pallas_call(
        _prenorm_kernel,
        out_shape=jax.ShapeDtypeStruct((t, d), jnp.bfloat16),
        grid=(t // rows,),
        in_specs=[pl.BlockSpec((rows, d), lambda i: (i, 0)),
                  pl.BlockSpec((1, d), lambda i: (0, 0))],
        out_specs=pl.BlockSpec((rows, d), lambda i: (i, 0)),
        compiler_params=pltpu.CompilerParams(
            dimension_semantics=("parallel",), vmem_limit_bytes=VMEM_LIMIT),
        name="prenorm",
    )(x2d, gain.reshape(1, d))


def _in_proj_kernel(h_ref, wchunk_ref, w2_ref, lbl_ref, o_ref, glo_ref, w2o_ref, wbuf_ref,
                    *, silu_tiles, gate_tiles, layer):
    jo = pl.program_id(0)
    i = pl.program_id(1)
    rows = wchunk_ref.shape[0]
    chunk = pl.ds(pl.multiple_of(i * rows, rows), rows)
    tile = jo - 1

    def in_ranges(ranges):
        hits = [(tile >= lo) & (tile < hi) for lo, hi in ranges]
        return functools.reduce(jnp.logical_or, hits)

    is_silu = in_ranges(silu_tiles)
    is_gate = in_ranges(gate_tiles)

    def casts_and_dot(slot):
        wbuf_ref[slot, chunk, :] = wchunk_ref[...].astype(wbuf_ref.dtype)
        w2o_ref[...] = w2_ref[...].astype(w2o_ref.dtype)
        return jnp.dot(h_ref[...], wbuf_ref[1 - slot], preferred_element_type=jnp.float32)

    @pl.when(jo == 0)
    def _():
        wbuf_ref[0, chunk, :] = wchunk_ref[...].astype(wbuf_ref.dtype)
        w2o_ref[...] = w2_ref[...].astype(w2o_ref.dtype)

    @pl.when((jo > 0) & jnp.logical_not(is_silu | is_gate))
    def _():
        o_ref[...] = casts_and_dot(jo % 2).astype(o_ref.dtype)

    @pl.when(is_silu)
    def _():
        o_ref[...] = _silu(casts_and_dot(jo % 2)).astype(o_ref.dtype)

    @pl.when(is_gate)
    def _():
        lbl = lbl_ref[...]
        e = jnp.exp(lbl - jnp.max(lbl, axis=0, keepdims=True))
        lb = jnp.sum(e[:layer + 1], axis=0, keepdims=True) / jnp.sum(e, axis=0, keepdims=True)
        f = 0.5 * (1.0 + lb) + 0.5 * (1.0 - lb) * jnp.tanh(0.5 * casts_and_dot(jo % 2))
        g2 = jnp.log(f) * LOG2_E
        g_hi = g2.astype(o_ref.dtype)
        o_ref[...] = g_hi
        glo_ref[...] = (g2 - g_hi.astype(jnp.float32)).astype(glo_ref.dtype)


def _tile_ranges(col_ranges, tn):
    assert all(lo % tn == 0 and hi % tn == 0 for lo, hi in col_ranges)
    return tuple((lo // tn, hi // tn) for lo, hi in col_ranges)


def _in_proj(h, w, w2, lb_logits, d_attn, d_rnn, layer, tm=1024, tn=1024, w2_col_blocks=16):
    t, d = h.shape
    n = w.shape[1]
    k2, d2 = w2.shape
    tm = min(tm, t)
    ni, nj = t // tm, n // tn
    assert d % ni == 0 and (d // ni) % 16 == 0 and k2 % ni == 0 and nj + 1 >= w2_col_blocks
    w2_blk = (k2 // ni, d2 // w2_col_blocks)
    rnn0 = d_attn + d_attn // 4 + d_attn
    assert rnn0 + 4 * d_rnn == n
    silu_tiles = _tile_ranges([(rnn0 - d_attn, rnn0), (rnn0, rnn0 + d_rnn),
                               (rnn0 + 3 * d_rnn, n)], tn)
    gate_tiles = _tile_ranges([(rnn0 + d_rnn, rnn0 + 2 * d_rnn)], tn)
    g0, g1 = gate_tiles[0]

    def w2_index(jo, i):
        done = jo >= w2_col_blocks
        return (jnp.where(done, ni - 1, i), jnp.where(done, w2_col_blocks - 1, jo))

    def glo_index(jo, i):
        tile = jo - 1
        row = jnp.where(tile < g0, 0, jnp.where(tile >= g1, ni - 1, i))
        return (row, jnp.clip(tile - g0, 0, g1 - g0 - 1))

    return pl.pallas_call(
        functools.partial(_in_proj_kernel, silu_tiles=silu_tiles, gate_tiles=gate_tiles,
                          layer=layer),
        out_shape=(jax.ShapeDtypeStruct((t, n), jnp.bfloat16),
                   jax.ShapeDtypeStruct((t, d_rnn), jnp.bfloat16),
                   jax.ShapeDtypeStruct((k2, d2), jnp.bfloat16)),
        grid=(nj + 1, ni),
        in_specs=[pl.BlockSpec((tm, d), lambda jo, i: (jnp.where(jo == 0, 0, i), 0)),
                  pl.BlockSpec((d // ni, tn), lambda jo, i: (i, jnp.minimum(jo, nj - 1))),
                  pl.BlockSpec(w2_blk, w2_index),
                  pl.BlockSpec((lb_logits.shape[0], tn),
                               lambda jo, i: (0, jnp.clip(jo - 1 - g0, 0, g1 - g0 - 1)))],
        out_specs=(pl.BlockSpec((tm, tn),
                                lambda jo, i: (jnp.where(jo == 0, 0, i), jnp.maximum(jo - 1, 0))),
                   pl.BlockSpec((tm, tn), glo_index),
                   pl.BlockSpec(w2_blk, w2_index)),
        scratch_shapes=[pltpu.VMEM((2, d, tn), jnp.bfloat16)],
        compiler_params=pltpu.CompilerParams(
            dimension_semantics=("arbitrary", "arbitrary"), vmem_limit_bytes=VMEM_LIMIT),
        name="in_proj",
    )(h, w, w2, lb_logits)


def _lane_halves(slab, head_in_high_half):
    lane = lax.broadcasted_iota(jnp.int32, slab.shape, 1)
    swapped = pltpu.roll(slab, 64, axis=1)
    zero = jnp.zeros_like(slab)
    if head_in_high_half:
        lo, hi = swapped, slab
    else:
        lo, hi = slab, swapped
    return jnp.where(lane < 64, lo, zero), jnp.where(lane >= 64, hi, zero)


def _swa_kernel(sink_ref, q_ref, kvp_ref, kvc_ref, g0_ref, g1_ref, g2_ref, g3_ref, o_ref,
                *, n_kv_heads, d_kv):
    n = pl.program_id(1)
    gate_refs = (g0_ref, g1_ref, g2_ref, g3_ref)
    w = WINDOW
    qi = lax.broadcasted_iota(jnp.int32, (w, w), 0)
    kj = lax.broadcasted_iota(jnp.int32, (w, w), 1)
    lower = kj <= qi
    prev_bias = jnp.where(n > 0, 0.0, -jnp.inf)
    lane = lax.broadcasted_iota(jnp.int32, (w, LANES), 1)
    scale = ATTN_HEAD_DIM ** -0.5 * LOG2_E
    pairs_per_kv = GQA_GROUP // 2

    kv_cache = {}

    def kv_operands(h):
        if h not in kv_cache:
            c0 = (h // 2) * LANES
            high = (h % 2) == 1
            k_slab = jnp.concatenate([kvp_ref[:, c0:c0 + LANES], kvc_ref[:, c0:c0 + LANES]],
                                     axis=0).astype(jnp.float32)
            v_slab = jnp.concatenate([kvp_ref[:, d_kv + c0:d_kv + c0 + LANES],
                                      kvc_ref[:, d_kv + c0:d_kv + c0 + LANES]],
                                     axis=0).astype(jnp.float32)
            k_lo, k_hi = _lane_halves(k_slab * scale, high)
            v_lo, v_hi = _lane_halves(v_slab, high)
            kv_cache[h] = (jnp.concatenate([k_lo, k_hi], axis=0).astype(jnp.bfloat16),
                           jnp.concatenate([v_lo, v_hi], axis=0).astype(jnp.bfloat16))
        return kv_cache[h]

    def score_stage(p, _):
        qp = q_ref[:, p * LANES:(p + 1) * LANES]
        kcat = kv_operands(p // pairs_per_kv)[0]
        return lax.dot_general(qp, kcat, _NT, preferred_element_type=jnp.float32)

    def max_stage(p, s):
        merged, mx = [], []
        for hh in range(2):
            prev = s[:, hh * 2 * w:hh * 2 * w + w] + prev_bias
            cur = s[:, hh * 2 * w + w:(hh + 1) * 2 * w]
            sh = jnp.where(lower, cur, prev)
            merged.append(sh)
            mx.append(jnp.maximum(jnp.max(sh, axis=-1, keepdims=True), sink_ref[2 * p + hh] * LOG2_E))
        return merged, mx

    def exp_stage(p, carry):
        merged, mx = carry
        probs, denom = [], []
        for hh in range(2):
            e = jnp.exp2(merged[hh] - mx[hh])
            denom.append(jnp.sum(e, axis=-1, keepdims=True)
                         + jnp.exp2(sink_ref[2 * p + hh] * LOG2_E - mx[hh]))
            e = e.astype(jnp.bfloat16)
            zero = jnp.zeros_like(e)
            probs += [jnp.where(lower, zero, e), jnp.where(lower, e, zero)]
        return jnp.concatenate(probs, axis=1), denom

    def value_stage(p, carry):
        pcat, denom = carry
        vcat = kv_operands(p // pairs_per_kv)[1]
        return jnp.dot(pcat, vcat, preferred_element_type=jnp.float32), denom

    def out_stage(p, carry):
        o, denom = carry
        o = o * jnp.where(lane < 64, 1.0 / denom[0], 1.0 / denom[1])
        gref = gate_refs[p // 8]
        gate = gref[:, (p % 8) * LANES:(p % 8 + 1) * LANES].astype(jnp.float32)
        o_ref[:, p * LANES:(p + 1) * LANES] = (o * gate).astype(o_ref.dtype)

    _skewed(n_kv_heads * pairs_per_kv,
            [score_stage, max_stage, exp_stage, value_stage, out_stage])


def _swa(proj, sinks, batch, seq, d_attn):
    t = proj.shape[0]
    w = WINDOW
    nb = seq // w
    n_q = d_attn // ATTN_HEAD_DIM
    n_kv = n_q // GQA_GROUP
    d_kv = n_kv * ATTN_HEAD_DIM
    kvw = 2 * d_kv
    assert d_attn == 4 * kvw
    kv_blk = d_attn // kvw
    gate_blk0 = (d_attn + kvw) // kvw

    def row(b, n):
        return b * nb + n

    in_specs = [
        pl.BlockSpec(memory_space=pltpu.SMEM),
        pl.BlockSpec((w, d_attn), lambda b, n: (row(b, n), 0)),
        pl.BlockSpec((w, kvw), lambda b, n: (row(b, jnp.maximum(n - 1, 0)), kv_blk)),
        pl.BlockSpec((w, kvw), lambda b, n: (row(b, n), kv_blk)),
    ] + [
        pl.BlockSpec((w, kvw), functools.partial(lambda b, n, j: (row(b, n), gate_blk0 + j), j=j))
        for j in range(4)
    ]
    return pl.pallas_call(
        functools.partial(_swa_kernel, n_kv_heads=n_kv, d_kv=d_kv),
        out_shape=jax.ShapeDtypeStruct((t, d_attn), jnp.bfloat16),
        grid=(batch, nb),
        in_specs=in_specs,
        out_specs=pl.BlockSpec((w, d_attn), lambda b, n: (row(b, n), 0)),
        compiler_params=pltpu.CompilerParams(
            dimension_semantics=("parallel", "parallel"), vmem_limit_bytes=VMEM_LIMIT),
        name="swa",
    )(sinks, proj, proj, proj, proj, proj, proj, proj)


def _decay_tables():
    c = CHUNK
    n = np.zeros((N_LEVELS + 1, c, c), np.float32)
    r = np.arange(c)
    for l in range(1, N_LEVELS):
        half = 1 << l
        m = 2 * half
        for t in range(c):
            start = (t // m) * m
            mid = start + half - 1
            if t > mid:
                n[l - 1, t] = (r > mid) & (r <= t)
            else:
                n[l - 1, t] = (r > t) & (r <= mid)
    n[N_LEVELS - 1] = r[None, :] <= r[:, None]
    n[N_LEVELS] = r[None, :] > r[:, None]
    cumsum = np.concatenate([n[N_LEVELS - 1], n[N_LEVELS - 1]], axis=1)
    n = n.reshape((N_LEVELS + 1) * c, c)
    ncat = np.concatenate([n, n], axis=1)
    tt, ss = np.meshgrid(r, r, indexing="ij")
    x = tt ^ ss
    level = np.full((c, c), -1, np.int32)
    for l in range(N_LEVELS):
        level[(ss < tt) & (x >= (1 << l)) & (x < (2 << l))] = l
    return ncat, cumsum, level


def _interleave_rows(k, q, half):
    pieces = []
    for r0 in range(0, k.shape[0], half):
        src = q if (r0 // half) % 2 else k
        pieces.append(src[r0:r0 + half])
    return jnp.concatenate(pieces, axis=0)


def _hgrn2_kernel(ncat_ref, cum_ref, level_ref, gain_ref, q_ref, ghi_ref, glo_ref, v_ref, gate_ref,
                  o_ref, state_ref, k_s, *, heads, chunks):
    c = CHUNK
    pw = 2 * LANES
    n_pairs = heads // 2

    @pl.when(pl.program_id(2) == 0)
    def _():
        state_ref[...] = jnp.zeros_like(state_ref)

    worst = jnp.zeros((1, pw), jnp.float32)
    for ci in range(chunks):
        rows = slice(ci * c, (ci + 1) * c)
        for pr in range(n_pairs):
            cols = slice(pr * pw, (pr + 1) * pw)
            g2 = ghi_ref[rows, cols].astype(jnp.float32) + glo_ref[rows, cols].astype(jnp.float32)
            k_s[rows, cols] = 1.0 - jnp.exp2(g2)
            for half in range(2):
                hsum = jnp.sum(g2[half * (c // 2):(half + 1) * (c // 2)], axis=0, keepdims=True)
                worst = jnp.minimum(worst, hsum)
    mild = jnp.min(worst) > -SAFE_LOG2

    def head_matmuls(a_bf, qi, kl, v_bf, idx):
        intra = jnp.dot(a_bf, v_bf, preferred_element_type=jnp.float32)
        st = state_ref[idx]
        inter = lax.dot_general(qi, st.astype(jnp.bfloat16), _NT,
                                preferred_element_type=jnp.float32)
        upd = lax.dot_general(v_bf, kl, _TN, preferred_element_type=jnp.float32)
        return inter + intra, st, upd

    def head_tail(a_bf, qi, kl, egl, v_bf, idx):
        o, st, upd = head_matmuls(a_bf, qi, kl, v_bf, idx)
        state_ref[idx] = egl * st + upd
        return o

    def write_out(outs, rows, cols):
        normed = []
        for o in outs:
            ms = jnp.mean(o * o, axis=-1, keepdims=True)
            normed.append(o * lax.rsqrt(ms + NORM_EPS))
        o2 = jnp.concatenate(normed, axis=1) * gain_ref[:, cols]
        gate = gate_ref[rows, cols].astype(jnp.float32)
        o_ref[rows, cols] = (o2 * gate).astype(o_ref.dtype)

    @pl.when(mild)
    def _():
        ti = lax.broadcasted_iota(jnp.int32, (c, c), 0)
        si = lax.broadcasted_iota(jnp.int32, (c, c), 1)
        causal = si <= ti
        halves = [slice(hh * LANES, (hh + 1) * LANES) for hh in range(2)]

        def where(item):
            ci, pr = divmod(item, n_pairs)
            return slice(ci * c, (ci + 1) * c), slice(pr * pw, (pr + 1) * pw), pr

        def cumsum_stage(item, _):
            rows, cols, _ = where(item)
            gcat = jnp.concatenate([ghi_ref[rows, cols], glo_ref[rows, cols]], axis=0)
            return jnp.dot(cum_ref[...], gcat, preferred_element_type=jnp.float32)

        def scale_stage(item, gsum):
            rows, cols, _ = where(item)
            ref = gsum[c // 2 - 1:c // 2, :]
            last = gsum[c - 1:c, :]
            d = gsum - ref
            qp = q_ref[rows, cols].astype(jnp.float32) * jnp.exp2(d)
            kp = k_s[rows, cols] * jnp.exp2(-d)
            qi = (qp * jnp.exp2(ref)).astype(jnp.bfloat16)
            kl = (kp * jnp.exp2(last - ref)).astype(jnp.bfloat16)
            qp = qp.astype(jnp.bfloat16)
            kp = kp.astype(jnp.bfloat16)
            a = [lax.dot_general(qp[:, hs], kp[:, hs], _NT, preferred_element_type=jnp.float32)
                 for hs in halves]
            return a, qi, kl, jnp.exp2(last)

        def mix_stage(item, carry):
            a, qi, kl, egl = carry
            rows, cols, pr = where(item)
            v_bf = v_ref[rows, cols]
            return [head_matmuls(jnp.where(causal, a[hh], 0.0).astype(jnp.bfloat16),
                                 qi[:, hs], kl[:, hs], v_bf[:, hs], 2 * pr + hh)
                    for hh, hs in enumerate(halves)], egl

        def out_stage(item, carry):
            heads_out, egl = carry
            rows, cols, pr = where(item)
            for hh, hs in enumerate(halves):
                _, st, upd = heads_out[hh]
                state_ref[2 * pr + hh] = egl[:, hs] * st + upd
            write_out([o for o, _, _ in heads_out], rows, cols)

        _skewed(chunks * n_pairs, [cumsum_stage, scale_stage, mix_stage, out_stage])

    @pl.when(jnp.logical_not(mild))
    def _():
        level = level_ref[...]
        trow = lax.broadcasted_iota(jnp.int32, (c, pw), 0)

        def chunk_body(ci, carry):
            rows = pl.ds(pl.multiple_of(ci * c, c), c)
            for pr in range(n_pairs):
                cols = slice(pr * pw, (pr + 1) * pw)
                q = q_ref[rows, cols].astype(jnp.float32)
                k = k_s[rows, cols]
                g_hi = ghi_ref[rows, cols]
                g_lo = glo_ref[rows, cols]
                f = jnp.exp2(g_hi.astype(jnp.float32) + g_lo.astype(jnp.float32))
                gcat = jnp.concatenate([g_hi, g_lo], axis=0)
                ex = jnp.dot(ncat_ref[...], gcat, preferred_element_type=jnp.float32)

                z = [jnp.where((trow & 1) == 1, q * f, k).astype(jnp.bfloat16)]
                for l in range(1, N_LEVELS):
                    half = 1 << l
                    if half < SUBLANES:
                        qk = jnp.where((trow & half) != 0, q, k)
                    else:
                        qk = _interleave_rows(k, q, half)
                    z.append((qk * jnp.exp2(ex[(l - 1) * c:l * c])).astype(jnp.bfloat16))
                eg = jnp.exp2(ex[(N_LEVELS - 1) * c:N_LEVELS * c])
                erest = jnp.exp2(ex[N_LEVELS * c:(N_LEVELS + 1) * c])
                qi = (q * eg).astype(jnp.bfloat16)
                kl = (k * erest).astype(jnp.bfloat16)
                qk_diag = q * k
                v_bf = v_ref[rows, cols]
                outs = []
                for hh in range(2):
                    hs = slice(hh * LANES, (hh + 1) * LANES)
                    attn = jnp.zeros((c, c), jnp.float32)
                    for l in range(N_LEVELS):
                        zl = z[l][:, hs]
                        a = lax.dot_general(zl, zl, _NT, preferred_element_type=jnp.float32)
                        attn = jnp.where(level == l, a, attn)
                    diag = jnp.sum(qk_diag[:, hs], axis=-1, keepdims=True)
                    o = head_tail(attn.astype(jnp.bfloat16), qi[:, hs], kl[:, hs],
                                  eg[c - 1:c, hs], v_bf[:, hs], 2 * pr + hh)
                    outs.append(o + diag * v_bf[:, hs].astype(jnp.float32))
                write_out(outs, rows, cols)
            return carry

        lax.fori_loop(0, chunks, chunk_body, 0)


def _hgrn2(proj, glo, gain, batch, seq, d_attn, d_rnn, heads=8, chunks=8):
    t = proj.shape[0]
    rows = chunks * CHUNK
    if seq % rows:
        chunks, rows = 1, CHUNK
    nl = seq // rows
    wb = heads * RNN_HEAD_DIM
    n_hg = d_rnn // wb
    kvw = d_attn // 4
    base = d_attn + kvw + d_attn
    assert base % wb == 0 and heads % 2 == 0
    ncat, cumsum, level = _decay_tables()

    def seg(i):
        blk0 = (base + i * d_rnn) // wb
        return pl.BlockSpec((rows, wb), lambda b, hg, l: (b * nl + l, blk0 + hg))

    const = lambda shape: pl.BlockSpec(shape, lambda b, hg, l: (0, 0))
    own = pl.BlockSpec((rows, wb), lambda b, hg, l: (b * nl + l, hg))
    return pl.pallas_call(
        functools.partial(_hgrn2_kernel, heads=heads, chunks=chunks),
        out_shape=jax.ShapeDtypeStruct((t, d_rnn), jnp.bfloat16),
        grid=(batch, n_hg, nl),
        in_specs=[
            const(ncat.shape), const(cumsum.shape), const(level.shape),
            pl.BlockSpec((1, wb), lambda b, hg, l: (0, hg)),
            seg(0), seg(1), own, seg(2), seg(3),
        ],
        out_specs=own,
        scratch_shapes=[pltpu.VMEM((heads, RNN_HEAD_DIM, RNN_HEAD_DIM), jnp.float32),
                        pltpu.VMEM((rows, wb), jnp.float32)],
        compiler_params=pltpu.CompilerParams(
            dimension_semantics=("parallel", "parallel", "arbitrary"),
            vmem_limit_bytes=VMEM_LIMIT),
        name="hgrn2",
    )(jnp.asarray(ncat, jnp.bfloat16), jnp.asarray(cumsum, jnp.bfloat16), jnp.asarray(level),
      gain.reshape(1, d_rnn), proj, proj, glo, proj, proj)


def _out_proj_kernel(a1_ref, a2_ref, w_ref, x_ref, g_ref, o_ref, acc_ref, *, nk1, nk):
    kk = pl.program_id(1)
    d = acc_ref.shape[1]

    def accumulate(a_ref, first=False):
        for n0 in range(0, d, OUT_COL_CHUNK):
            cols = slice(n0, n0 + OUT_COL_CHUNK)
            part = jnp.dot(a_ref[...], w_ref[:, cols], preferred_element_type=jnp.float32)
            acc_ref[:, cols] = part if first else acc_ref[:, cols] + part

    @pl.when(kk == 0)
    def _():
        accumulate(a1_ref, first=True)

    @pl.when((kk > 0) & (kk < nk1))
    def _():
        accumulate(a1_ref)

    @pl.when((kk >= nk1) & (kk < nk))
    def _():
        accumulate(a2_ref)

    @pl.when(kk >= nk)
    def _():
        rows_per_step = o_ref.shape[0]
        base = (kk - nk) * rows_per_step

        def row_body(ri, carry):
            off = pl.multiple_of(ri * OUT_ROW_CHUNK, OUT_ROW_CHUNK)
            y = acc_ref[pl.ds(pl.multiple_of(base + off, OUT_ROW_CHUNK), OUT_ROW_CHUNK), :]
            ms = jnp.mean(y * y, axis=-1, keepdims=True)
            rows = pl.ds(off, OUT_ROW_CHUNK)
            o_ref[rows, :] = x_ref[rows, :] + y * lax.rsqrt(ms + NORM_EPS) * g_ref[...]
            return carry

        lax.fori_loop(0, rows_per_step // OUT_ROW_CHUNK, row_body, 0)


def _out_proj(a1, a2, w, x2d, gain, tm=1024, tk=1024):
    t, k1 = a1.shape
    k2 = a2.shape[1]
    d = w.shape[1]
    tm = min(tm, t)
    nk1, nk2 = k1 // tk, k2 // tk
    nk = nk1 + nk2
    te = min(OUT_EPILOGUE_ROWS, tm)
    ne = tm // te

    def epi(i, k):
        return i * ne + jnp.maximum(k - nk, 0)

    return pl.pallas_call(
        functools.partial(_out_proj_kernel, nk1=nk1, nk=nk),
        out_shape=jax.ShapeDtypeStruct((t, d), jnp.float32),
        grid=(t // tm, nk + ne),
        in_specs=[pl.BlockSpec((tm, tk), lambda i, k: (i, jnp.minimum(k, nk1 - 1))),
                  pl.BlockSpec((tm, tk), lambda i, k: (i, jnp.clip(k - nk1, 0, nk2 - 1))),
                  pl.BlockSpec((tk, d), lambda i, k: (jnp.minimum(k, nk - 1), 0)),
                  pl.BlockSpec((te, d), lambda i, k: (epi(i, k), 0)),
                  pl.BlockSpec((1, d), lambda i, k: (0, 0))],
        out_specs=pl.BlockSpec((te, d), lambda i, k: (epi(i, k), 0)),
        scratch_shapes=[pltpu.VMEM((tm, d), jnp.float32)],
        compiler_params=pltpu.CompilerParams(
            dimension_semantics=("parallel", "arbitrary"), vmem_limit_bytes=VMEM_LIMIT),
        name="out_proj",
    )(a1, a2, w, x2d, gain.reshape(1, d))


def kernel(x, w_in, attn_sinks, lb_logits, rnn_norm, w_out, pre_norm, post_norm):
    batch, seq, d_model = x.shape
    depth = w_in.shape[0]
    d_mix = w_out.shape[1]
    d_attn = d_mix // 2
    d_rnn = d_mix - d_attn
    x2d = x.reshape(batch * seq, d_model)
    for layer in range(depth):
        h = _prenorm(x2d, pre_norm[layer])
        proj, glo, w_out_bf = _in_proj(h, w_in[layer], w_out[layer], lb_logits, d_attn, d_rnn, layer)
        attn = _swa(proj, attn_sinks[layer], batch, seq, d_attn)
        rnn = _hgrn2(proj, glo, rnn_norm[layer], batch, seq, d_attn, d_rnn)
        x2d = _out_proj(attn, rnn, w_out_bf, x2d, post_norm[layer])
    return x2d.reshape(batch, seq, d_model)
```

```python
import functools

import jax
import jax.numpy as jnp
import numpy as np
from jax import lax
from jax.experimental import pallas as pl
from jax.experimental.pallas import tpu as pltpu

ATTN_HEAD_DIM = 64
GQA_GROUP = 8
WINDOW = 128
RNN_HEAD_DIM = 128
NORM_EPS = 1e-6

LANES = 128
SUBLANES = 8
CHUNK = 128
N_LEVELS = 7
SAFE_LOG2 = 100.0
LOG2_E = 1.4426950408889634
VMEM_LIMIT = 56 * 1024 * 1024
GATE_ROW_CHUNKS = 4
OUT_COL_CHUNK = 1024
OUT_ROW_CHUNK = 32
OUT_EPILOGUE_ROWS = 128

_NT = (((1,), (1,)), ((), ()))
_TN = (((0,), (0,)), ((), ()))


def _skewed(n_items, stages, gap=2):
    carry = [None] * n_items
    for step in range(n_items + gap * (len(stages) - 1)):
        for j in reversed(range(len(stages))):
            i = step - gap * j
            if 0 <= i < n_items:
                carry[i] = stages[j](i, carry[i])


def _silu(x):
    h = 0.5 * x
    return h + h * jnp.tanh(h)


def _prenorm_kernel(x_ref, g_ref, o_ref):
    x = x_ref[...]
    ms = jnp.mean(x * x, axis=-1, keepdims=True)
    o_ref[...] = (x * lax.rsqrt(ms + NORM_EPS) * g_ref[...]).astype(o_ref.dtype)


def _prenorm(x2d, gain, rows=512):
    t, d = x2d.shape
    rows = min(rows, t)
    return pl.pallas_call(
        _prenorm_kernel,
        out_shape=jax.ShapeDtypeStruct((t, d), jnp.bfloat16),
        grid=(t // rows,),
        in_specs=[pl.BlockSpec((rows, d), lambda i: (i, 0)),
                  pl.BlockSpec((1, d), lambda i: (0, 0))],
        out_specs=pl.BlockSpec((rows, d), lambda i: (i, 0)),
        compiler_params=pltpu.CompilerParams(
            dimension_semantics=("parallel",), vmem_limit_bytes=VMEM_LIMIT),
        name="prenorm",
    )(x2d, gain.reshape(1, d))


def _in_proj_kernel(h_ref, wchunk_ref, w2_ref, lbl_ref, o_ref, glo_ref, w2o_ref, wbuf_ref,
                    *, silu_tiles, gate_tiles, layer):
    jo = pl.program_id(0)
    i = pl.program_id(1)
    rows = wchunk_ref.shape[0]
    chunk = pl.ds(pl.multiple_of(i * rows, rows), rows)
    tile = jo - 1

    def in_ranges(ranges):
        hits = [(tile >= lo) & (tile < hi) for lo, hi in ranges]
        return functools.reduce(jnp.logical_or, hits)

    is_silu = in_ranges(silu_tiles)
    is_gate = in_ranges(gate_tiles)

    def casts_and_dot(slot):
        wbuf_ref[slot, chunk, :] = wchunk_ref[...].astype(wbuf_ref.dtype)
        w2o_ref[...] = w2_ref[...].astype(w2o_ref.dtype)
        return jnp.dot(h_ref[...], wbuf_ref[1 - slot], preferred_element_type=jnp.float32)

    @pl.when(jo == 0)
    def _():
        wbuf_ref[0, chunk, :] = wchunk_ref[...].astype(wbuf_ref.dtype)
        w2o_ref[...] = w2_ref[...].astype(w2o_ref.dtype)

    @pl.when((jo > 0) & jnp.logical_not(is_silu | is_gate))
    def _():
        o_ref[...] = casts_and_dot(jo % 2).astype(o_ref.dtype)

    @pl.when(is_silu)
    def _():
        o_ref[...] = _silu(casts_and_dot(jo % 2)).astype(o_ref.dtype)

    @pl.when(is_gate)
    def _():
        lbl = lbl_ref[...]
        e = jnp.exp(lbl - jnp.max(lbl, axis=0, keepdims=True))
        lb = jnp.sum(e[:layer + 1], axis=0, keepdims=True) / jnp.sum(e, axis=0, keepdims=True)
        fa = 0.5 * (1.0 + lb)
        fb = 0.5 * (1.0 - lb)
        slot = jo % 2
        wbuf_ref[slot, chunk, :] = wchunk_ref[...].astype(wbuf_ref.dtype)
        w2o_ref[...] = w2_ref[...].astype(w2o_ref.dtype)
        rc = o_ref.shape[0] // GATE_ROW_CHUNKS
        for r in range(GATE_ROW_CHUNKS):
            rs = slice(r * rc, (r + 1) * rc)
            acc = jnp.dot(h_ref[rs, :], wbuf_ref[1 - slot], preferred_element_type=jnp.float32)
            g2 = jnp.log(fa + fb * jnp.tanh(0.5 * acc)) * LOG2_E
            g_hi = g2.astype(o_ref.dtype)
            o_ref[rs, :] = g_hi
            glo_ref[rs, :] = (g2 - g_hi.astype(jnp.float32)).astype(glo_ref.dtype)


def _tile_ranges(col_ranges, tn):
    assert all(lo % tn == 0 and hi % tn == 0 for lo, hi in col_ranges)
    return tuple((lo // tn, hi // tn) for lo, hi in col_ranges)


def _in_proj(h, w, w2, lb_logits, d_attn, d_rnn, layer, tm=1024, tn=1024, w2_col_blocks=16):
    t, d = h.shape
    n = w.shape[1]
    k2, d2 = w2.shape
    tm = min(tm, t)
    ni, nj = t // tm, n // tn
    assert d % ni == 0 and (d // ni) % 16 == 0 and k2 % ni == 0 and nj + 1 >= w2_col_blocks
    w2_blk = (k2 // ni, d2 // w2_col_blocks)
    rnn0 = d_attn + d_attn // 4 + d_attn
    assert rnn0 + 4 * d_rnn == n
    silu_tiles = _tile_ranges([(rnn0 - d_attn, rnn0), (rnn0, rnn0 + d_rnn),
                               (rnn0 + 3 * d_rnn, n)], tn)
    gate_tiles = _tile_ranges([(rnn0 + d_rnn, rnn0 + 2 * d_rnn)], tn)
    g0, g1 = gate_tiles[0]

    def w2_index(jo, i):
        done = jo >= w2_col_blocks
        return (jnp.where(done, ni - 1, i), jnp.where(done, w2_col_blocks - 1, jo))

    def glo_index(jo, i):
        tile = jo - 1
        row = jnp.where(tile < g0, 0, jnp.where(tile >= g1, ni - 1, i))
        return (row, jnp.clip(tile - g0, 0, g1 - g0 - 1))

    return pl.pallas_call(
        functools.partial(_in_proj_kernel, silu_tiles=silu_tiles, gate_tiles=gate_tiles,
                          layer=layer),
        out_shape=(jax.ShapeDtypeStruct((t, n), jnp.bfloat16),
                   jax.ShapeDtypeStruct((t, d_rnn), jnp.bfloat16),
                   jax.ShapeDtypeStruct((k2, d2), jnp.bfloat16)),
        grid=(nj + 1, ni),
        in_specs=[pl.BlockSpec((tm, d), lambda jo, i: (jnp.where(jo == 0, 0, i), 0)),
                  pl.BlockSpec((d // ni, tn), lambda jo, i: (i, jnp.minimum(jo, nj - 1))),
                  pl.BlockSpec(w2_blk, w2_index),
                  pl.BlockSpec((lb_logits.shape[0], tn),
                               lambda jo, i: (0, jnp.clip(jo - 1 - g0, 0, g1 - g0 - 1)))],
        out_specs=(pl.BlockSpec((tm, tn),
                                lambda jo, i: (jnp.where(jo == 0, 0, i), jnp.maximum(jo - 1, 0))),
                   pl.BlockSpec((tm, tn), glo_index),
                   pl.BlockSpec(w2_blk, w2_index)),
        scratch_shapes=[pltpu.VMEM((2, d, tn), jnp.bfloat16)],
        compiler_params=pltpu.CompilerParams(
            dimension_semantics=("arbitrary", "arbitrary"), vmem_limit_bytes=VMEM_LIMIT),
        name="in_proj",
    )(h, w, w2, lb_logits)


def _lane_halves(slab, head_in_high_half):
    lane = lax.broadcasted_iota(jnp.int32, slab.shape, 1)
    swapped = pltpu.roll(slab, 64, axis=1)
    zero = jnp.zeros_like(slab)
    if head_in_high_half:
        lo, hi = swapped, slab
    else:
        lo, hi = slab, swapped
    return jnp.where(lane < 64, lo, zero), jnp.where(lane >= 64, hi, zero)


def _swa_kernel(sink_ref, q_ref, kvp_ref, kvc_ref, g0_ref, g1_ref, g2_ref, g3_ref, o_ref,
                *, n_kv_heads, d_kv):
    n = pl.program_id(1)
    gate_refs = (g0_ref, g1_ref, g2_ref, g3_ref)
    w = WINDOW
    qi = lax.broadcasted_iota(jnp.int32, (w, w), 0)
    kj = lax.broadcasted_iota(jnp.int32, (w, w), 1)
    lower = kj <= qi
    prev_bias = jnp.where(n > 0, 0.0, -jnp.inf)
    lane = lax.broadcasted_iota(jnp.int32, (w, LANES), 1)
    scale = ATTN_HEAD_DIM ** -0.5 * LOG2_E
    pairs_per_kv = GQA_GROUP // 2

    kv_cache = {}

    def kv_operands(h):
        if h not in kv_cache:
            c0 = (h // 2) * LANES
            high = (h % 2) == 1
            k_slab = jnp.concatenate([kvp_ref[:, c0:c0 + LANES], kvc_ref[:, c0:c0 + LANES]],
                                     axis=0).astype(jnp.float32)
            v_slab = jnp.concatenate([kvp_ref[:, d_kv + c0:d_kv + c0 + LANES],
                                      kvc_ref[:, d_kv + c0:d_kv + c0 + LANES]],
                                     axis=0).astype(jnp.float32)
            k_lo, k_hi = _lane_halves(k_slab * scale, high)
            v_lo, v_hi = _lane_halves(v_slab, high)
            kv_cache[h] = (jnp.concatenate([k_lo, k_hi], axis=0).astype(jnp.bfloat16),
                           jnp.concatenate([v_lo, v_hi], axis=0).astype(jnp.bfloat16))
        return kv_cache[h]

    def score_stage(p, _):
        qp = q_ref[:, p * LANES:(p + 1) * LANES]
        kcat = kv_operands(p // pairs_per_kv)[0]
        return lax.dot_general(qp, kcat, _NT, preferred_element_type=jnp.float32)

    def max_stage(p, s):
        merged, mx = [], []
        for hh in range(2):
            prev = s[:, hh * 2 * w:hh * 2 * w + w] + prev_bias
            cur = s[:, hh * 2 * w + w:(hh + 1) * 2 * w]
            sh = jnp.where(lower, cur, prev)
            merged.append(sh)
            mx.append(jnp.maximum(jnp.max(sh, axis=-1, keepdims=True), sink_ref[2 * p + hh] * LOG2_E))
        return merged, mx

    def exp_stage(p, carry):
        merged, mx = carry
        probs, denom = [], []
        for hh in range(2):
            e = jnp.exp2(merged[hh] - mx[hh])
            denom.append(jnp.sum(e, axis=-1, keepdims=True)
                         + jnp.exp2(sink_ref[2 * p + hh] * LOG2_E - mx[hh]))
            e = e.astype(jnp.bfloat16)
            zero = jnp.zeros_like(e)
            probs += [jnp.where(lower, zero, e), jnp.where(lower, e, zero)]
        return jnp.concatenate(probs, axis=1), denom

    def value_stage(p, carry):
        pcat, denom = carry
        vcat = kv_operands(p // pairs_per_kv)[1]
        return jnp.dot(pcat, vcat, preferred_element_type=jnp.float32), denom

    def out_stage(p, carry):
        o, denom = carry
        o = o * jnp.where(lane < 64, 1.0 / denom[0], 1.0 / denom[1])
        gref = gate_refs[p // 8]
        gate = gref[:, (p % 8) * LANES:(p % 8 + 1) * LANES].astype(jnp.float32)
        o_ref[:, p * LANES:(p + 1) * LANES] = (o * gate).astype(o_ref.dtype)

    _skewed(n_kv_heads * pairs_per_kv,
            [score_stage, max_stage, exp_stage, value_stage, out_stage])


def _swa(proj, sinks, batch, seq, d_attn):
    t = proj.shape[0]
    w = WINDOW
    nb = seq // w
    n_q = d_attn // ATTN_HEAD_DIM
    n_kv = n_q // GQA_GROUP
    d_kv = n_kv * ATTN_HEAD_DIM
    kvw = 2 * d_kv
    assert d_attn == 4 * kvw
    kv_blk = d_attn // kvw
    gate_blk0 = (d_attn + kvw) // kvw

    def row(b, n):
        return b * nb + n

    in_specs = [
        pl.BlockSpec(memory_space=pltpu.SMEM),
        pl.BlockSpec((w, d_attn), lambda b, n: (row(b, n), 0)),
        pl.BlockSpec((w, kvw), lambda b, n: (row(b, jnp.maximum(n - 1, 0)), kv_blk)),
        pl.BlockSpec((w, kvw), lambda b, n: (row(b, n), kv_blk)),
    ] + [
        pl.BlockSpec((w, kvw), functools.partial(lambda b, n, j: (row(b, n), gate_blk0 + j), j=j))
        for j in range(4)
    ]
    return pl.pallas_call(
        functools.partial(_swa_kernel, n_kv_heads=n_kv, d_kv=d_kv),
        out_shape=jax.ShapeDtypeStruct((t, d_attn), jnp.bfloat16),
        grid=(batch, nb),
        in_specs=in_specs,
        out_specs=pl.BlockSpec((w, d_attn), lambda b, n: (row(b, n), 0)),
        compiler_params=pltpu.CompilerParams(
            dimension_semantics=("parallel", "parallel"), vmem_limit_bytes=VMEM_LIMIT),
        name="swa",
    )(sinks, proj, proj, proj, proj, proj, proj, proj)


def _decay_tables():
    c = CHUNK
    n = np.zeros((N_LEVELS + 1, c, c), np.float32)
    r = np.arange(c)
    for l in range(1, N_LEVELS):
        half = 1 << l
        m = 2 * half
        for t in range(c):
            start = (t // m) * m
            mid = start + half - 1
            if t > mid:
                n[l - 1, t] = (r > mid) & (r <= t)
            else:
                n[l - 1, t] = (r > t) & (r <= mid)
    n[N_LEVELS - 1] = r[None, :] <= r[:, None]
    n[N_LEVELS] = r[None, :] > r[:, None]
    cumsum = np.concatenate([n[N_LEVELS - 1], n[N_LEVELS - 1]], axis=1)
    n = n.reshape((N_LEVELS + 1) * c, c)
    ncat = np.concatenate([n, n], axis=1)
    tt, ss = np.meshgrid(r, r, indexing="ij")
    x = tt ^ ss
    level = np.full((c, c), -1, np.int32)
    for l in range(N_LEVELS):
        level[(ss < tt) & (x >= (1 << l)) & (x < (2 << l))] = l
    return ncat, cumsum, level


def _interleave_rows(k, q, half):
    pieces = []
    for r0 in range(0, k.shape[0], half):
        src = q if (r0 // half) % 2 else k
        pieces.append(src[r0:r0 + half])
    return jnp.concatenate(pieces, axis=0)


def _hgrn2_kernel(ncat_ref, cum_ref, level_ref, gain_ref, q_ref, ghi_ref, glo_ref, v_ref, gate_ref,
                  o_ref, state_ref, k_s, *, heads, chunks):
    c = CHUNK
    pw = 2 * LANES
    n_pairs = heads // 2

    @pl.when(pl.program_id(2) == 0)
    def _():
        state_ref[...] = jnp.zeros_like(state_ref)

    worst = jnp.zeros((1, pw), jnp.float32)
    for ci in range(chunks):
        rows = slice(ci * c, (ci + 1) * c)
        for pr in range(n_pairs):
            cols = slice(pr * pw, (pr + 1) * pw)
            g2 = ghi_ref[rows, cols].astype(jnp.float32) + glo_ref[rows, cols].astype(jnp.float32)
            k_s[rows, cols] = 1.0 - jnp.exp2(g2)
            for half in range(2):
                hsum = jnp.sum(g2[half * (c // 2):(half + 1) * (c // 2)], axis=0, keepdims=True)
                worst = jnp.minimum(worst, hsum)
    mild = jnp.min(worst) > -SAFE_LOG2

    def head_matmuls(a_bf, qi, kl, v_bf, idx):
        intra = jnp.dot(a_bf, v_bf, preferred_element_type=jnp.float32)
        st = state_ref[idx]
        inter = lax.dot_general(qi, st.astype(jnp.bfloat16), _NT,
                                preferred_element_type=jnp.float32)
        upd = lax.dot_general(v_bf, kl, _TN, preferred_element_type=jnp.float32)
        return inter + intra, st, upd

    def head_tail(a_bf, qi, kl, egl, v_bf, idx):
        o, st, upd = head_matmuls(a_bf, qi, kl, v_bf, idx)
        state_ref[idx] = egl * st + upd
        return o

    def write_out(outs, rows, cols):
        normed = []
        for o in outs:
            ms = jnp.mean(o * o, axis=-1, keepdims=True)
            normed.append(o * lax.rsqrt(ms + NORM_EPS))
        o2 = jnp.concatenate(normed, axis=1) * gain_ref[:, cols]
        gate = gate_ref[rows, cols].astype(jnp.float32)
        o_ref[rows, cols] = (o2 * gate).astype(o_ref.dtype)

    @pl.when(mild)
    def _():
        ti = lax.broadcasted_iota(jnp.int32, (c, c), 0)
        si = lax.broadcasted_iota(jnp.int32, (c, c), 1)
        causal = si <= ti
        halves = [slice(hh * LANES, (hh + 1) * LANES) for hh in range(2)]

        def where(item):
            ci, pr = divmod(item, n_pairs)
            return slice(ci * c, (ci + 1) * c), slice(pr * pw, (pr + 1) * pw), pr

        def cumsum_stage(item, _):
            rows, cols, _ = where(item)
            gcat = jnp.concatenate([ghi_ref[rows, cols], glo_ref[rows, cols]], axis=0)
            return jnp.dot(cum_ref[...], gcat, preferred_element_type=jnp.float32)

        def scale_stage(item, gsum):
            rows, cols, _ = where(item)
            ref = gsum[c // 2 - 1:c // 2, :]
            last = gsum[c - 1:c, :]
            d = gsum - ref
            qp = q_ref[rows, cols].astype(jnp.float32) * jnp.exp2(d)
            kp = k_s[rows, cols] * jnp.exp2(-d)
            qi = (qp * jnp.exp2(ref)).astype(jnp.bfloat16)
            kl = (kp * jnp.exp2(last - ref)).astype(jnp.bfloat16)
            qp = qp.astype(jnp.bfloat16)
            kp = kp.astype(jnp.bfloat16)
            a = [lax.dot_general(qp[:, hs], kp[:, hs], _NT, preferred_element_type=jnp.float32)
                 for hs in halves]
            return a, qi, kl, jnp.exp2(last)

        def mix_stage(item, carry):
            a, qi, kl, egl = carry
            rows, cols, pr = where(item)
            v_bf = v_ref[rows, cols]
            return [head_matmuls(jnp.where(causal, a[hh], 0.0).astype(jnp.bfloat16),
                                 qi[:, hs], kl[:, hs], v_bf[:, hs], 2 * pr + hh)
                    for hh, hs in enumerate(halves)], egl

        def out_stage(item, carry):
            heads_out, egl = carry
            rows, cols, pr = where(item)
            for hh, hs in enumerate(halves):
                _, st, upd = heads_out[hh]
                state_ref[2 * pr + hh] = egl[:, hs] * st + upd
            write_out([o for o, _, _ in heads_out], rows, cols)

        _skewed(chunks * n_pairs, [cumsum_stage, scale_stage, mix_stage, out_stage])

    @pl.when(jnp.logical_not(mild))
    def _():
        level = level_ref[...]
        trow = lax.broadcasted_iota(jnp.int32, (c, pw), 0)

        def chunk_body(ci, carry):
            rows = pl.ds(pl.multiple_of(ci * c, c), c)
            for pr in range(n_pairs):
                cols = slice(pr * pw, (pr + 1) * pw)
                q = q_ref[rows, cols].astype(jnp.float32)
                k = k_s[rows, cols]
                g_hi = ghi_ref[rows, cols]
                g_lo = glo_ref[rows, cols]
                f = jnp.exp2(g_hi.astype(jnp.float32) + g_lo.astype(jnp.float32))
                gcat = jnp.concatenate([g_hi, g_lo], axis=0)
                ex = jnp.dot(ncat_ref[...], gcat, preferred_element_type=jnp.float32)

                z = [jnp.where((trow & 1) == 1, q * f, k).astype(jnp.bfloat16)]
                for l in range(1, N_LEVELS):
                    half = 1 << l
                    if half < SUBLANES:
                        qk = jnp.where((trow & half) != 0, q, k)
                    else:
                        qk = _interleave_rows(k, q, half)
                    z.append((qk * jnp.exp2(ex[(l - 1) * c:l * c])).astype(jnp.bfloat16))
                eg = jnp.exp2(ex[(N_LEVELS - 1) * c:N_LEVELS * c])
                erest = jnp.exp2(ex[N_LEVELS * c:(N_LEVELS + 1) * c])
                qi = (q * eg).astype(jnp.bfloat16)
                kl = (k * erest).astype(jnp.bfloat16)
                qk_diag = q * k
                v_bf = v_ref[rows, cols]
                outs = []
                for hh in range(2):
                    hs = slice(hh * LANES, (hh + 1) * LANES)
                    attn = jnp.zeros((c, c), jnp.float32)
                    for l in range(N_LEVELS):
                        zl = z[l][:, hs]
                        a = lax.dot_general(zl, zl, _NT, preferred_element_type=jnp.float32)
                        attn = jnp.where(level == l, a, attn)
                    diag = jnp.sum(qk_diag[:, hs], axis=-1, keepdims=True)
                    o = head_tail(attn.astype(jnp.bfloat16), qi[:, hs], kl[:, hs],
                                  eg[c - 1:c, hs], v_bf[:, hs], 2 * pr + hh)
                    outs.append(o + diag * v_bf[:, hs].astype(jnp.float32))
                write_out(outs, rows, cols)
            return carry

        lax.fori_loop(0, chunks, chunk_body, 0)


def _hgrn2(proj, glo, gain, batch, seq, d_attn, d_rnn, heads=8, chunks=8):
    t = proj.shape[0]
    rows = chunks * CHUNK
    if seq % rows:
        chunks, rows = 1, CHUNK
    nl = seq // rows
    wb = heads * RNN_HEAD_DIM
    n_hg = d_rnn // wb
    kvw = d_attn // 4
    base = d_attn + kvw + d_attn
    assert base % wb == 0 and heads % 2 == 0
    ncat, cumsum, level = _decay_tables()

    def seg(i):
        blk0 = (base + i * d_rnn) // wb
        return pl.BlockSpec((rows, wb), lambda b, hg, l: (b * nl + l, blk0 + hg))

    const = lambda shape: pl.BlockSpec(shape, lambda b, hg, l: (0, 0))
    own = pl.BlockSpec((rows, wb), lambda b, hg, l: (b * nl + l, hg))
    return pl.pallas_call(
        functools.partial(_hgrn2_kernel, heads=heads, chunks=chunks),
        out_shape=jax.ShapeDtypeStruct((t, d_rnn), jnp.bfloat16),
        grid=(batch, n_hg, nl),
        in_specs=[
            const(ncat.shape), const(cumsum.shape), const(level.shape),
            pl.BlockSpec((1, wb), lambda b, hg, l: (0, hg)),
            seg(0), seg(1), own, seg(2), seg(3),
        ],
        out_specs=own,
        scratch_shapes=[pltpu.VMEM((heads, RNN_HEAD_DIM, RNN_HEAD_DIM), jnp.float32),
                        pltpu.VMEM((rows, wb), jnp.float32)],
        compiler_params=pltpu.CompilerParams(
            dimension_semantics=("parallel", "parallel", "arbitrary"),
            vmem_limit_bytes=VMEM_LIMIT),
        name="hgrn2",
    )(jnp.asarray(ncat, jnp.bfloat16), jnp.asarray(cumsum, jnp.bfloat16), jnp.asarray(level),
      gain.reshape(1, d_rnn), proj, proj, glo, proj, proj)


def _out_proj_kernel(a1_ref, a2_ref, w_ref, x_ref, g_ref, o_ref, acc_ref, normed_ref,
                     *, nk1, n_tiles):
    r = pl.program_id(0)
    kk = pl.program_id(1)
    nk = pl.num_programs(1)
    d = acc_ref.shape[1]
    te = o_ref.shape[0]

    def residual_chunk():
        rows = pl.ds(pl.multiple_of(kk * te, te), te)
        o_ref[...] = x_ref[...] + normed_ref[rows, :].astype(jnp.float32)

    def accumulate():
        a = jnp.where(kk < nk1, a1_ref[...], a2_ref[...])
        for n0 in range(0, d, OUT_COL_CHUNK):
            cols = slice(n0, n0 + OUT_COL_CHUNK)
            part = jnp.dot(a, w_ref[:, cols], preferred_element_type=jnp.float32)
            acc_ref[:, cols] = jnp.where(kk == 0, part, acc_ref[:, cols] + part)

    @pl.when(r == 0)
    def _():
        accumulate()

    @pl.when((r > 0) & (r < n_tiles))
    def _():
        residual_chunk()
        accumulate()

    @pl.when(r == n_tiles)
    def _():
        residual_chunk()

    @pl.when((r < n_tiles) & (kk == nk - 1))
    def _():
        def row_body(ri, carry):
            rows = pl.ds(pl.multiple_of(ri * OUT_ROW_CHUNK, OUT_ROW_CHUNK), OUT_ROW_CHUNK)
            y = acc_ref[rows, :]
            ms = jnp.mean(y * y, axis=-1, keepdims=True)
            normed_ref[rows, :] = (y * lax.rsqrt(ms + NORM_EPS) * g_ref[...]).astype(normed_ref.dtype)
            return carry

        lax.fori_loop(0, acc_ref.shape[0] // OUT_ROW_CHUNK, row_body, 0, unroll=4)


def _out_proj(a1, a2, w, x2d, gain, tm=1024, tk=512):
    t, k1 = a1.shape
    k2 = a2.shape[1]
    d = w.shape[1]
    tm = min(tm, t)
    n_tiles = t // tm
    nk1, nk2 = k1 // tk, k2 // tk
    nk = nk1 + nk2
    te = tm // nk
    assert tm % nk == 0 and te % SUBLANES == 0

    def a_row(r):
        return jnp.minimum(r, n_tiles - 1)

    def residual_block(r, k):
        return (jnp.where(r == 0, 0, (r - 1) * nk + k), 0)

    return pl.pallas_call(
        functools.partial(_out_proj_kernel, nk1=nk1, n_tiles=n_tiles),
        out_shape=jax.ShapeDtypeStruct((t, d), jnp.float32),
        grid=(n_tiles + 1, nk),
        in_specs=[pl.BlockSpec((tm, tk), lambda r, k: (a_row(r), jnp.minimum(k, nk1 - 1))),
                  pl.BlockSpec((tm, tk), lambda r, k: (a_row(r), jnp.clip(k - nk1, 0, nk2 - 1))),
                  pl.BlockSpec((tk, d), lambda r, k: (jnp.where(r == n_tiles, nk - 1, k), 0)),
                  pl.BlockSpec((te, d), residual_block),
                  pl.BlockSpec((1, d), lambda r, k: (0, 0))],
        out_specs=pl.BlockSpec((te, d), residual_block),
        scratch_shapes=[pltpu.VMEM((tm, d), jnp.float32),
                        pltpu.VMEM((tm, d), jnp.bfloat16)],
        compiler_params=pltpu.CompilerParams(
            dimension_semantics=("arbitrary", "arbitrary"), vmem_limit_bytes=VMEM_LIMIT),
        name="out_proj",
    )(a1, a2, w, x2d, gain.reshape(1, d))


def kernel(x, w_in, attn_sinks, lb_logits, rnn_norm, w_out, pre_norm, post_norm):
    batch, seq, d_model = x.shape
    depth = w_in.shape[0]
    d_mix = w_out.shape[1]
    d_attn = d_mix // 2
    d_rnn = d_mix - d_attn
    x2d = x.reshape(batch * seq, d_model)
    for layer in range(depth):
        h = _prenorm(x2d, pre_norm[layer])
        proj, glo, w_out_bf = _in_proj(h, w_in[layer], w_out[layer], lb_logits, d_attn, d_rnn, layer)
        attn = _swa(proj, attn_sinks[layer], batch, seq, d_attn)
        rnn = _hgrn2(proj, glo, rnn_norm[layer], batch, seq, d_attn, d_rnn)
        x2d = _out_proj(attn, rnn, w_out_bf, x2d, post_norm[layer])
    return x2d.reshape(batch, seq, d_model)
```

```python
import functools

import jax
import jax.numpy as jnp
import numpy as np
from jax import lax
from jax.experimental import pallas as pl
from jax.experimental.pallas import tpu as pltpu

ATTN_HEAD_DIM = 64
GQA_GROUP = 8
WINDOW = 128
RNN_HEAD_DIM = 128
NORM_EPS = 1e-6

LANES = 128
SUBLANES = 8
CHUNK = 128
N_LEVELS = 7
SAFE_LOG2 = 100.0
LOG2_E = 1.4426950408889634
VMEM_LIMIT = 56 * 1024 * 1024
GATE_ROW_CHUNKS = 4
OUT_COL_CHUNK = 1024

_NT = (((1,), (1,)), ((), ()))
_TN = (((0,), (0,)), ((), ()))


def _skewed(n_items, stages, gap=2):
    carry = [None] * n_items
    for step in range(n_items + gap * (len(stages) - 1)):
        for j in reversed(range(len(stages))):
            i = step - gap * j
            if 0 <= i < n_items:
                carry[i] = stages[j](i, carry[i])


def _silu(x):
    h = 0.5 * x
    return h + h * jnp.tanh(h)


def _prenorm_kernel(x_ref, g_ref, o_ref):
    x = x_ref[...]
    ms = jnp.mean(x * x, axis=-1, keepdims=True)
    o_ref[...] = (x * lax.rsqrt(ms + NORM_EPS) * g_ref[...]).astype(o_ref.dtype)


def _prenorm(x2d, gain, rows=512):
    t, d = x2d.shape
    rows = min(rows, t)
    return pl.pallas_call(
        _prenorm_kernel,
        out_shape=jax.ShapeDtypeStruct((t, d), jnp.bfloat16),
        grid=(t // rows,),
        in_specs=[pl.BlockSpec((rows, d), lambda i: (i, 0)),
                  pl.BlockSpec((1, d), lambda i: (0, 0))],
        out_specs=pl.BlockSpec((rows, d), lambda i: (i, 0)),
        compiler_params=pltpu.CompilerParams(
            dimension_semantics=("parallel",), vmem_limit_bytes=VMEM_LIMIT),
        name="prenorm",
    )(x2d, gain.reshape(1, d))


def _in_proj_kernel(h_ref, wchunk_ref, w2_ref, lbl_ref, o_ref, glo_ref, w2o_ref, wbuf_ref,
                    *, silu_tiles, gate_tiles, layer):
    jo = pl.program_id(0)
    i = pl.program_id(1)
    rows = wchunk_ref.shape[0]
    chunk = pl.ds(pl.multiple_of(i * rows, rows), rows)
    tile = jo - 1

    def in_ranges(ranges):
        hits = [(tile >= lo) & (tile < hi) for lo, hi in ranges]
        return functools.reduce(jnp.logical_or, hits)

    is_silu = in_ranges(silu_tiles)
    is_gate = in_ranges(gate_tiles)

    def casts_and_dot(slot):
        wbuf_ref[slot, chunk, :] = wchunk_ref[...].astype(wbuf_ref.dtype)
        w2o_ref[...] = w2_ref[...].astype(w2o_ref.dtype)
        return jnp.dot(h_ref[...], wbuf_ref[1 - slot], preferred_element_type=jnp.float32)

    @pl.when(jo == 0)
    def _():
        wbuf_ref[0, chunk, :] = wchunk_ref[...].astype(wbuf_ref.dtype)
        w2o_ref[...] = w2_ref[...].astype(w2o_ref.dtype)

    @pl.when((jo > 0) & jnp.logical_not(is_silu | is_gate))
    def _():
        o_ref[...] = casts_and_dot(jo % 2).astype(o_ref.dtype)

    @pl.when(is_silu)
    def _():
        o_ref[...] = _silu(casts_and_dot(jo % 2)).astype(o_ref.dtype)

    @pl.when(is_gate)
    def _():
        lbl = lbl_ref[...]
        e = jnp.exp(lbl - jnp.max(lbl, axis=0, keepdims=True))
        lb = jnp.sum(e[:layer + 1], axis=0, keepdims=True) / jnp.sum(e, axis=0, keepdims=True)
        fa = 0.5 * (1.0 + lb)
        fb = 0.5 * (1.0 - lb)
        slot = jo % 2
        wbuf_ref[slot, chunk, :] = wchunk_ref[...].astype(wbuf_ref.dtype)
        w2o_ref[...] = w2_ref[...].astype(w2o_ref.dtype)
        rc = o_ref.shape[0] // GATE_ROW_CHUNKS
        for r in range(GATE_ROW_CHUNKS):
            rs = slice(r * rc, (r + 1) * rc)
            acc = jnp.dot(h_ref[rs, :], wbuf_ref[1 - slot], preferred_element_type=jnp.float32)
            g2 = jnp.log(fa + fb * jnp.tanh(0.5 * acc)) * LOG2_E
            g_hi = g2.astype(o_ref.dtype)
            o_ref[rs, :] = g_hi
            glo_ref[rs, :] = (g2 - g_hi.astype(jnp.float32)).astype(glo_ref.dtype)


def _tile_ranges(col_ranges, tn):
    assert all(lo % tn == 0 and hi % tn == 0 for lo, hi in col_ranges)
    return tuple((lo // tn, hi // tn) for lo, hi in col_ranges)


def _in_proj(h, w, w2, lb_logits, d_attn, d_rnn, layer, tm=1024, tn=1024, w2_col_blocks=16):
    t, d = h.shape
    n = w.shape[1]
    k2, d2 = w2.shape
    tm = min(tm, t)
    ni, nj = t // tm, n // tn
    assert d % ni == 0 and (d // ni) % 16 == 0 and k2 % ni == 0 and nj + 1 >= w2_col_blocks
    w2_blk = (k2 // ni, d2 // w2_col_blocks)
    rnn0 = d_attn + d_attn // 4 + d_attn
    assert rnn0 + 4 * d_rnn == n
    silu_tiles = _tile_ranges([(rnn0 - d_attn, rnn0), (rnn0, rnn0 + d_rnn),
                               (rnn0 + 3 * d_rnn, n)], tn)
    gate_tiles = _tile_ranges([(rnn0 + d_rnn, rnn0 + 2 * d_rnn)], tn)
    g0, g1 = gate_tiles[0]

    def w2_index(jo, i):
        done = jo >= w2_col_blocks
        return (jnp.where(done, ni - 1, i), jnp.where(done, w2_col_blocks - 1, jo))

    def glo_index(jo, i):
        tile = jo - 1
        row = jnp.where(tile < g0, 0, jnp.where(tile >= g1, ni - 1, i))
        return (row, jnp.clip(tile - g0, 0, g1 - g0 - 1))

    return pl.pallas_call(
        functools.partial(_in_proj_kernel, silu_tiles=silu_tiles, gate_tiles=gate_tiles,
                          layer=layer),
        out_shape=(jax.ShapeDtypeStruct((t, n), jnp.bfloat16),
                   jax.ShapeDtypeStruct((t, d_rnn), jnp.bfloat16),
                   jax.ShapeDtypeStruct((k2, d2), jnp.bfloat16)),
        grid=(nj + 1, ni),
        in_specs=[pl.BlockSpec((tm, d), lambda jo, i: (jnp.where(jo == 0, 0, i), 0)),
                  pl.BlockSpec((d // ni, tn), lambda jo, i: (i, jnp.minimum(jo, nj - 1))),
                  pl.BlockSpec(w2_blk, w2_index),
                  pl.BlockSpec((lb_logits.shape[0], tn),
                               lambda jo, i: (0, jnp.clip(jo - 1 - g0, 0, g1 - g0 - 1)))],
        out_specs=(pl.BlockSpec((tm, tn),
                                lambda jo, i: (jnp.where(jo == 0, 0, i), jnp.maximum(jo - 1, 0))),
                   pl.BlockSpec((tm, tn), glo_index),
                   pl.BlockSpec(w2_blk, w2_index)),
        scratch_shapes=[pltpu.VMEM((2, d, tn), jnp.bfloat16)],
        compiler_params=pltpu.CompilerParams(
            dimension_semantics=("arbitrary", "arbitrary"), vmem_limit_bytes=VMEM_LIMIT),
        name="in_proj",
    )(h, w, w2, lb_logits)


def _lane_halves(slab, head_in_high_half):
    lane = lax.broadcasted_iota(jnp.int32, slab.shape, 1)
    swapped = pltpu.roll(slab, 64, axis=1)
    zero = jnp.zeros_like(slab)
    if head_in_high_half:
        lo, hi = swapped, slab
    else:
        lo, hi = slab, swapped
    return jnp.where(lane < 64, lo, zero), jnp.where(lane >= 64, hi, zero)


def _swa_kernel(sink_ref, q_ref, kvp_ref, kvc_ref, g0_ref, g1_ref, g2_ref, g3_ref, o_ref,
                *, n_kv_heads, d_kv):
    n = pl.program_id(1)
    gate_refs = (g0_ref, g1_ref, g2_ref, g3_ref)
    w = WINDOW
    qi = lax.broadcasted_iota(jnp.int32, (w, w), 0)
    kj = lax.broadcasted_iota(jnp.int32, (w, w), 1)
    lower = kj <= qi
    prev_bias = jnp.where(n > 0, 0.0, -jnp.inf)
    lane = lax.broadcasted_iota(jnp.int32, (w, LANES), 1)
    scale = ATTN_HEAD_DIM ** -0.5 * LOG2_E
    pairs_per_kv = GQA_GROUP // 2

    kv_cache = {}

    def kv_operands(h):
        if h not in kv_cache:
            c0 = (h // 2) * LANES
            high = (h % 2) == 1
            k_slab = jnp.concatenate([kvp_ref[:, c0:c0 + LANES], kvc_ref[:, c0:c0 + LANES]],
                                     axis=0).astype(jnp.float32)
            v_slab = jnp.concatenate([kvp_ref[:, d_kv + c0:d_kv + c0 + LANES],
                                      kvc_ref[:, d_kv + c0:d_kv + c0 + LANES]],
                                     axis=0).astype(jnp.float32)
            k_lo, k_hi = _lane_halves(k_slab * scale, high)
            v_lo, v_hi = _lane_halves(v_slab, high)
            kv_cache[h] = (jnp.concatenate([k_lo, k_hi], axis=0).astype(jnp.bfloat16),
                           jnp.concatenate([v_lo, v_hi], axis=0).astype(jnp.bfloat16))
        return kv_cache[h]

    def score_stage(p, _):
        qp = q_ref[:, p * LANES:(p + 1) * LANES]
        kcat = kv_operands(p // pairs_per_kv)[0]
        return lax.dot_general(qp, kcat, _NT, preferred_element_type=jnp.float32)

    def max_stage(p, s):
        merged, mx = [], []
        for hh in range(2):
            prev = s[:, hh * 2 * w:hh * 2 * w + w] + prev_bias
            cur = s[:, hh * 2 * w + w:(hh + 1) * 2 * w]
            sh = jnp.where(lower, cur, prev)
            merged.append(sh)
            mx.append(jnp.maximum(jnp.max(sh, axis=-1, keepdims=True), sink_ref[2 * p + hh] * LOG2_E))
        return merged, mx

    def exp_stage(p, carry):
        merged, mx = carry
        probs, denom = [], []
        for hh in range(2):
            e = jnp.exp2(merged[hh] - mx[hh])
            denom.append(jnp.sum(e, axis=-1, keepdims=True)
                         + jnp.exp2(sink_ref[2 * p + hh] * LOG2_E - mx[hh]))
            e = e.astype(jnp.bfloat16)
            zero = jnp.zeros_like(e)
            probs += [jnp.where(lower, zero, e), jnp.where(lower, e, zero)]
        return jnp.concatenate(probs, axis=1), denom

    def value_stage(p, carry):
        pcat, denom = carry
        vcat = kv_operands(p // pairs_per_kv)[1]
        return jnp.dot(pcat, vcat, preferred_element_type=jnp.float32), denom

    def out_stage(p, carry):
        o, denom = carry
        o = o * jnp.where(lane < 64, 1.0 / denom[0], 1.0 / denom[1])
        gref = gate_refs[p // 8]
        gate = gref[:, (p % 8) * LANES:(p % 8 + 1) * LANES].astype(jnp.float32)
        o_ref[:, p * LANES:(p + 1) * LANES] = (o * gate).astype(o_ref.dtype)

    _skewed(n_kv_heads * pairs_per_kv,
            [score_stage, max_stage, exp_stage, value_stage, out_stage])


def _swa(proj, sinks, batch, seq, d_attn):
    t = proj.shape[0]
    w = WINDOW
    nb = seq // w
    n_q = d_attn // ATTN_HEAD_DIM
    n_kv = n_q // GQA_GROUP
    d_kv = n_kv * ATTN_HEAD_DIM
    kvw = 2 * d_kv
    assert d_attn == 4 * kvw
    kv_blk = d_attn // kvw
    gate_blk0 = (d_attn + kvw) // kvw

    def row(b, n):
        return b * nb + n

    in_specs = [
        pl.BlockSpec(memory_space=pltpu.SMEM),
        pl.BlockSpec((w, d_attn), lambda b, n: (row(b, n), 0)),
        pl.BlockSpec((w, kvw), lambda b, n: (row(b, jnp.maximum(n - 1, 0)), kv_blk)),
        pl.BlockSpec((w, kvw), lambda b, n: (row(b, n), kv_blk)),
    ] + [
        pl.BlockSpec((w, kvw), functools.partial(lambda b, n, j: (row(b, n), gate_blk0 + j), j=j))
        for j in range(4)
    ]
    return pl.pallas_call(
        functools.partial(_swa_kernel, n_kv_heads=n_kv, d_kv=d_kv),
        out_shape=jax.ShapeDtypeStruct((t, d_attn), jnp.bfloat16),
        grid=(batch, nb),
        in_specs=in_specs,
        out_specs=pl.BlockSpec((w, d_attn), lambda b, n: (row(b, n), 0)),
        compiler_params=pltpu.CompilerParams(
            dimension_semantics=("parallel", "parallel"), vmem_limit_bytes=VMEM_LIMIT),
        name="swa",
    )(sinks, proj, proj, proj, proj, proj, proj, proj)


def _decay_tables():
    c = CHUNK
    n = np.zeros((N_LEVELS + 1, c, c), np.float32)
    r = np.arange(c)
    for l in range(1, N_LEVELS):
        half = 1 << l
        m = 2 * half
        for t in range(c):
            start = (t // m) * m
            mid = start + half - 1
            if t > mid:
                n[l - 1, t] = (r > mid) & (r <= t)
            else:
                n[l - 1, t] = (r > t) & (r <= mid)
    n[N_LEVELS - 1] = r[None, :] <= r[:, None]
    n[N_LEVELS] = r[None, :] > r[:, None]
    cumsum = np.concatenate([n[N_LEVELS - 1], n[N_LEVELS - 1]], axis=1)
    n = n.reshape((N_LEVELS + 1) * c, c)
    ncat = np.concatenate([n, n], axis=1)
    tt, ss = np.meshgrid(r, r, indexing="ij")
    x = tt ^ ss
    level = np.full((c, c), -1, np.int32)
    for l in range(N_LEVELS):
        level[(ss < tt) & (x >= (1 << l)) & (x < (2 << l))] = l
    return ncat, cumsum, level


def _interleave_rows(k, q, half):
    pieces = []
    for r0 in range(0, k.shape[0], half):
        src = q if (r0 // half) % 2 else k
        pieces.append(src[r0:r0 + half])
    return jnp.concatenate(pieces, axis=0)


def _hgrn2_kernel(ncat_ref, cum_ref, level_ref, gain_ref, q_ref, ghi_ref, glo_ref, v_ref, gate_ref,
                  o_ref, state_ref, k_s, *, heads, chunks):
    c = CHUNK
    pw = 2 * LANES
    n_pairs = heads // 2

    @pl.when(pl.program_id(2) == 0)
    def _():
        state_ref[...] = jnp.zeros_like(state_ref)

    worst = jnp.zeros((1, pw), jnp.float32)
    for ci in range(chunks):
        rows = slice(ci * c, (ci + 1) * c)
        for pr in range(n_pairs):
            cols = slice(pr * pw, (pr + 1) * pw)
            g2 = ghi_ref[rows, cols].astype(jnp.float32) + glo_ref[rows, cols].astype(jnp.float32)
            k_s[rows, cols] = 1.0 - jnp.exp2(g2)
            for half in range(2):
                hsum = jnp.sum(g2[half * (c // 2):(half + 1) * (c // 2)], axis=0, keepdims=True)
                worst = jnp.minimum(worst, hsum)
    mild = jnp.min(worst) > -SAFE_LOG2

    def head_matmuls(a_bf, qi, kl, v_bf, idx):
        intra = jnp.dot(a_bf, v_bf, preferred_element_type=jnp.float32)
        st = state_ref[idx]
        inter = lax.dot_general(qi, st.astype(jnp.bfloat16), _NT,
                                preferred_element_type=jnp.float32)
        upd = lax.dot_general(v_bf, kl, _TN, preferred_element_type=jnp.float32)
        return inter + intra, st, upd

    def head_tail(a_bf, qi, kl, egl, v_bf, idx):
        o, st, upd = head_matmuls(a_bf, qi, kl, v_bf, idx)
        state_ref[idx] = egl * st + upd
        return o

    def write_out(outs, rows, cols):
        normed = []
        for o in outs:
            ms = jnp.mean(o * o, axis=-1, keepdims=True)
            normed.append(o * lax.rsqrt(ms + NORM_EPS))
        o2 = jnp.concatenate(normed, axis=1) * gain_ref[:, cols]
        gate = gate_ref[rows, cols].astype(jnp.float32)
        o_ref[rows, cols] = (o2 * gate).astype(o_ref.dtype)

    @pl.when(mild)
    def _():
        ti = lax.broadcasted_iota(jnp.int32, (c, c), 0)
        si = lax.broadcasted_iota(jnp.int32, (c, c), 1)
        causal = si <= ti
        halves = [slice(hh * LANES, (hh + 1) * LANES) for hh in range(2)]

        def where(item):
            ci, pr = divmod(item, n_pairs)
            return slice(ci * c, (ci + 1) * c), slice(pr * pw, (pr + 1) * pw), pr

        def cumsum_stage(item, _):
            rows, cols, _ = where(item)
            gcat = jnp.concatenate([ghi_ref[rows, cols], glo_ref[rows, cols]], axis=0)
            return jnp.dot(cum_ref[...], gcat, preferred_element_type=jnp.float32)

        def scale_stage(item, gsum):
            rows, cols, _ = where(item)
            ref = gsum[c // 2 - 1:c // 2, :]
            last = gsum[c - 1:c, :]
            d = gsum - ref
            qp = q_ref[rows, cols].astype(jnp.float32) * jnp.exp2(d)
            kp = k_s[rows, cols] * jnp.exp2(-d)
            qi = (qp * jnp.exp2(ref)).astype(jnp.bfloat16)
            kl = (kp * jnp.exp2(last - ref)).astype(jnp.bfloat16)
            qp = qp.astype(jnp.bfloat16)
            kp = kp.astype(jnp.bfloat16)
            a = [lax.dot_general(qp[:, hs], kp[:, hs], _NT, preferred_element_type=jnp.float32)
                 for hs in halves]
            return a, qi, kl, jnp.exp2(last)

        def mix_stage(item, carry):
            a, qi, kl, egl = carry
            rows, cols, pr = where(item)
            v_bf = v_ref[rows, cols]
            return [head_matmuls(jnp.where(causal, a[hh], 0.0).astype(jnp.bfloat16),
                                 qi[:, hs], kl[:, hs], v_bf[:, hs], 2 * pr + hh)
                    for hh, hs in enumerate(halves)], egl

        def out_stage(item, carry):
            heads_out, egl = carry
            rows, cols, pr = where(item)
            for hh, hs in enumerate(halves):
                _, st, upd = heads_out[hh]
                state_ref[2 * pr + hh] = egl[:, hs] * st + upd
            write_out([o for o, _, _ in heads_out], rows, cols)

        _skewed(chunks * n_pairs, [cumsum_stage, scale_stage, mix_stage, out_stage])

    @pl.when(jnp.logical_not(mild))
    def _():
        level = level_ref[...]
        trow = lax.broadcasted_iota(jnp.int32, (c, pw), 0)

        def chunk_body(ci, carry):
            rows = pl.ds(pl.multiple_of(ci * c, c), c)
            for pr in range(n_pairs):
                cols = slice(pr * pw, (pr + 1) * pw)
                q = q_ref[rows, cols].astype(jnp.float32)
                k = k_s[rows, cols]
                g_hi = ghi_ref[rows, cols]
                g_lo = glo_ref[rows, cols]
                f = jnp.exp2(g_hi.astype(jnp.float32) + g_lo.astype(jnp.float32))
                gcat = jnp.concatenate([g_hi, g_lo], axis=0)
                ex = jnp.dot(ncat_ref[...], gcat, preferred_element_type=jnp.float32)

                z = [jnp.where((trow & 1) == 1, q * f, k).astype(jnp.bfloat16)]
                for l in range(1, N_LEVELS):
                    half = 1 << l
                    if half < SUBLANES:
                        qk = jnp.where((trow & half) != 0, q, k)
                    else:
                        qk = _interleave_rows(k, q, half)
                    z.append((qk * jnp.exp2(ex[(l - 1) * c:l * c])).astype(jnp.bfloat16))
                eg = jnp.exp2(ex[(N_LEVELS - 1) * c:N_LEVELS * c])
                erest = jnp.exp2(ex[N_LEVELS * c:(N_LEVELS + 1) * c])
                qi = (q * eg).astype(jnp.bfloat16)
                kl = (k * erest).astype(jnp.bfloat16)
                qk_diag = q * k
                v_bf = v_ref[rows, cols]
                outs = []
                for hh in range(2):
                    hs = slice(hh * LANES, (hh + 1) * LANES)
                    attn = jnp.zeros((c, c), jnp.float32)
                    for l in range(N_LEVELS):
                        zl = z[l][:, hs]
                        a = lax.dot_general(zl, zl, _NT, preferred_element_type=jnp.float32)
                        attn = jnp.where(level == l, a, attn)
                    diag = jnp.sum(qk_diag[:, hs], axis=-1, keepdims=True)
                    o = head_tail(attn.astype(jnp.bfloat16), qi[:, hs], kl[:, hs],
                                  eg[c - 1:c, hs], v_bf[:, hs], 2 * pr + hh)
                    outs.append(o + diag * v_bf[:, hs].astype(jnp.float32))
                write_out(outs, rows, cols)
            return carry

        lax.fori_loop(0, chunks, chunk_body, 0)


def _hgrn2(proj, glo, gain, batch, seq, d_attn, d_rnn, heads=8, chunks=8):
    t = proj.shape[0]
    rows = chunks * CHUNK
    if seq % rows:
        chunks, rows = 1, CHUNK
    nl = seq // rows
    wb = heads * RNN_HEAD_DIM
    n_hg = d_rnn // wb
    kvw = d_attn // 4
    base = d_attn + kvw + d_attn
    assert base % wb == 0 and heads % 2 == 0
    ncat, cumsum, level = _decay_tables()

    def seg(i):
        blk0 = (base + i * d_rnn) // wb
        return pl.BlockSpec((rows, wb), lambda b, hg, l: (b * nl + l, blk0 + hg))

    const = lambda shape: pl.BlockSpec(shape, lambda b, hg, l: (0, 0))
    own = pl.BlockSpec((rows, wb), lambda b, hg, l: (b * nl + l, hg))
    return pl.pallas_call(
        functools.partial(_hgrn2_kernel, heads=heads, chunks=chunks),
        out_shape=jax.ShapeDtypeStruct((t, d_rnn), jnp.bfloat16),
        grid=(batch, n_hg, nl),
        in_specs=[
            const(ncat.shape), const(cumsum.shape), const(level.shape),
            pl.BlockSpec((1, wb), lambda b, hg, l: (0, hg)),
            seg(0), seg(1), own, seg(2), seg(3),
        ],
        out_specs=own,
        scratch_shapes=[pltpu.VMEM((heads, RNN_HEAD_DIM, RNN_HEAD_DIM), jnp.float32),
                        pltpu.VMEM((rows, wb), jnp.float32)],
        compiler_params=pltpu.CompilerParams(
            dimension_semantics=("parallel", "parallel", "arbitrary"),
            vmem_limit_bytes=VMEM_LIMIT),
        name="hgrn2",
    )(jnp.asarray(ncat, jnp.bfloat16), jnp.asarray(cumsum, jnp.bfloat16), jnp.asarray(level),
      gain.reshape(1, d_rnn), proj, proj, glo, proj, proj)


def _out_proj_kernel(a1_ref, a2_ref, w_ref, x_ref, g_ref, o_ref, acc_ref, y_ref, ssq_ref,
                     *, nk1, n_tiles):
    r = pl.program_id(0)
    kk = pl.program_id(1)
    nk = pl.num_programs(1)
    d = acc_ref.shape[1]
    te = o_ref.shape[0]

    def residual_chunk():
        rows = pl.ds(pl.multiple_of(kk * te, te), te)
        inv = lax.rsqrt(ssq_ref[rows, :] * (1.0 / d) + NORM_EPS)
        o_ref[...] = x_ref[...] + y_ref[rows, :].astype(jnp.float32) * inv * g_ref[...]

    def accumulate(last):
        a = jnp.where(kk < nk1, a1_ref[...], a2_ref[...])
        ssq = None
        for n0 in range(0, d, OUT_COL_CHUNK):
            cols = slice(n0, n0 + OUT_COL_CHUNK)
            part = jnp.dot(a, w_ref[:, cols], preferred_element_type=jnp.float32)
            y = jnp.where(kk == 0, part, acc_ref[:, cols] + part)
            if last:
                y_ref[:, cols] = y.astype(y_ref.dtype)
                s = jnp.sum(y * y, axis=-1, keepdims=True)
                ssq = s if ssq is None else ssq + s
            else:
                acc_ref[:, cols] = y
        if last:
            ssq_ref[...] = ssq

    first, drain = r == 0, r == n_tiles
    mid = jnp.logical_not(first | drain)
    last_slab = kk == nk - 1

    @pl.when(first & jnp.logical_not(last_slab))
    def _():
        accumulate(False)

    @pl.when(first & last_slab)
    def _():
        accumulate(True)

    @pl.when(mid & jnp.logical_not(last_slab))
    def _():
        residual_chunk()
        accumulate(False)

    @pl.when(mid & last_slab)
    def _():
        residual_chunk()
        accumulate(True)

    @pl.when(drain)
    def _():
        residual_chunk()


def _out_proj(a1, a2, w, x2d, gain, tm=1024, tk=512):
    t, k1 = a1.shape
    k2 = a2.shape[1]
    d = w.shape[1]
    tm = min(tm, t)
    n_tiles = t // tm
    nk1, nk2 = k1 // tk, k2 // tk
    nk = nk1 + nk2
    te = tm // nk
    assert tm % nk == 0 and te % SUBLANES == 0

    def a_row(r):
        return jnp.minimum(r, n_tiles - 1)

    def residual_block(r, k):
        return (jnp.where(r == 0, 0, (r - 1) * nk + k), 0)

    return pl.pallas_call(
        functools.partial(_out_proj_kernel, nk1=nk1, n_tiles=n_tiles),
        out_shape=jax.ShapeDtypeStruct((t, d), jnp.float32),
        grid=(n_tiles + 1, nk),
        in_specs=[pl.BlockSpec((tm, tk), lambda r, k: (a_row(r), jnp.minimum(k, nk1 - 1))),
                  pl.BlockSpec((tm, tk), lambda r, k: (a_row(r), jnp.clip(k - nk1, 0, nk2 - 1))),
                  pl.BlockSpec((tk, d), lambda r, k: (jnp.where(r == n_tiles, nk - 1, k), 0)),
                  pl.BlockSpec((te, d), residual_block),
                  pl.BlockSpec((1, d), lambda r, k: (0, 0))],
        out_specs=pl.BlockSpec((te, d), residual_block),
        scratch_shapes=[pltpu.VMEM((tm, d), jnp.float32),
                        pltpu.VMEM((tm, d), jnp.bfloat16),
                        pltpu.VMEM((tm, 1), jnp.float32)],
        compiler_params=pltpu.CompilerParams(
            dimension_semantics=("arbitrary", "arbitrary"), vmem_limit_bytes=VMEM_LIMIT),
        name="out_proj",
    )(a1, a2, w, x2d, gain.reshape(1, d))


def kernel(x, w_in, attn_sinks, lb_logits, rnn_norm, w_out, pre_norm, post_norm):
    batch, seq, d_model = x.shape
    depth = w_in.shape[0]
    d_mix = w_out.shape[1]
    d_attn = d_mix // 2
    d_rnn = d_mix - d_attn
    x2d = x.reshape(batch * seq, d_model)
    for layer in range(depth):
        h = _prenorm(x2d, pre_norm[layer])
        proj, glo, w_out_bf = _in_proj(h, w_in[layer], w_out[layer], lb_logits, d_attn, d_rnn, layer)
        attn = _swa(proj, attn_sinks[layer], batch, seq, d_attn)
        rnn = _hgrn2(proj, glo, rnn_norm[layer], batch, seq, d_attn, d_rnn)
        x2d = _out_proj(attn, rnn, w_out_bf, x2d, post_norm[layer])
    return x2d.reshape(batch, seq, d_model)
```

```python
import functools

import jax
import jax.numpy as jnp
import numpy as np
from jax import lax
from jax.experimental import pallas as pl
from jax.experimental.pallas import tpu as pltpu

ATTN_HEAD_DIM = 64
GQA_GROUP = 8
WINDOW = 128
RNN_HEAD_DIM = 128
NORM_EPS = 1e-6

LANES = 128
SUBLANES = 8
BF16_ROWS = 2 * SUBLANES
CHUNK = 128
N_LEVELS = 7
SAFE_LOG2 = 100.0
LOG2_E = 1.4426950408889634
VMEM_LIMIT = 56 * 1024 * 1024
SKEW_GAP = 3
GATE_ROW_CHUNKS = 4
OUT_COL_CHUNK = 1024

_NT = (((1,), (1,)), ((), ()))
_TN = (((0,), (0,)), ((), ()))


def _skewed(n_items, stages, gap=SKEW_GAP):
    carry = [None] * n_items
    for step in range(n_items + gap * (len(stages) - 1)):
        for j in reversed(range(len(stages))):
            i = step - gap * j
            if 0 <= i < n_items:
                carry[i] = stages[j](i, carry[i])


def _silu(x):
    h = 0.5 * x
    return h + h * jnp.tanh(h)


def _prenorm_kernel(x_ref, g_ref, o_ref):
    x = x_ref[...]
    ms = jnp.mean(x * x, axis=-1, keepdims=True)
    o_ref[...] = (x * lax.rsqrt(ms + NORM_EPS) * g_ref[...]).astype(o_ref.dtype)


def _prenorm(x2d, gain, rows=512):
    t, d = x2d.shape
    rows = min(rows, t)
    return pl.pallas_call(
        _prenorm_kernel,
        out_shape=jax.ShapeDtypeStruct((t, d), jnp.bfloat16),
        grid=(t // rows,),
        in_specs=[pl.BlockSpec((rows, d), lambda i: (i, 0)),
                  pl.BlockSpec((1, d), lambda i: (0, 0))],
        out_specs=pl.BlockSpec((rows, d), lambda i: (i, 0)),
        compiler_params=pltpu.CompilerParams(
            dimension_semantics=("parallel",), vmem_limit_bytes=VMEM_LIMIT),
        name="prenorm",
    )(x2d, gain.reshape(1, d))


def _in_proj_kernel(h_ref, wchunk_ref, w2_ref, lbl_ref, o_ref, glo_ref, w2o_ref, wbuf_ref,
                    *, silu_tiles, gate_tiles, layer):
    jo = pl.program_id(0)
    i = pl.program_id(1)
    rows = wchunk_ref.shape[0]
    chunk = pl.ds(pl.multiple_of(i * rows, rows), rows)
    tile = jo - 1

    def in_ranges(ranges):
        hits = [(tile >= lo) & (tile < hi) for lo, hi in ranges]
        return functools.reduce(jnp.logical_or, hits)

    is_silu = in_ranges(silu_tiles)
    is_gate = in_ranges(gate_tiles)

    def casts_and_dot(slot):
        wbuf_ref[slot, chunk, :] = wchunk_ref[...].astype(wbuf_ref.dtype)
        w2o_ref[...] = w2_ref[...].astype(w2o_ref.dtype)
        return jnp.dot(h_ref[...], wbuf_ref[1 - slot], preferred_element_type=jnp.float32)

    @pl.when(jo == 0)
    def _():
        wbuf_ref[0, chunk, :] = wchunk_ref[...].astype(wbuf_ref.dtype)
        w2o_ref[...] = w2_ref[...].astype(w2o_ref.dtype)

    @pl.when((jo > 0) & jnp.logical_not(is_silu | is_gate))
    def _():
        o_ref[...] = casts_and_dot(jo % 2).astype(o_ref.dtype)

    @pl.when(is_silu)
    def _():
        o_ref[...] = _silu(casts_and_dot(jo % 2)).astype(o_ref.dtype)

    @pl.when(is_gate)
    def _():
        lbl = lbl_ref[...]
        e = jnp.exp(lbl - jnp.max(lbl, axis=0, keepdims=True))
        lb = jnp.sum(e[:layer + 1], axis=0, keepdims=True) / jnp.sum(e, axis=0, keepdims=True)
        fa = 0.5 * (1.0 + lb)
        fb = 0.5 * (1.0 - lb)
        slot = jo % 2
        wbuf_ref[slot, chunk, :] = wchunk_ref[...].astype(wbuf_ref.dtype)
        w2o_ref[...] = w2_ref[...].astype(w2o_ref.dtype)
        rc = o_ref.shape[0] // GATE_ROW_CHUNKS
        for r in range(GATE_ROW_CHUNKS):
            rs = slice(r * rc, (r + 1) * rc)
            acc = jnp.dot(h_ref[rs, :], wbuf_ref[1 - slot], preferred_element_type=jnp.float32)
            g2 = jnp.log(fa + fb * jnp.tanh(0.5 * acc)) * LOG2_E
            g_hi = g2.astype(o_ref.dtype)
            o_ref[rs, :] = g_hi
            glo_ref[rs, :] = (g2 - g_hi.astype(jnp.float32)).astype(glo_ref.dtype)


def _tile_ranges(col_ranges, tn):
    assert all(lo % tn == 0 and hi % tn == 0 for lo, hi in col_ranges)
    return tuple((lo // tn, hi // tn) for lo, hi in col_ranges)


def _in_proj(h, w, w2, lb_logits, d_attn, d_rnn, layer, tm=1024, tn=1024, w2_col_blocks=16):
    t, d = h.shape
    n = w.shape[1]
    k2, d2 = w2.shape
    tm = min(tm, t)
    ni, nj = t // tm, n // tn
    assert d % ni == 0 and (d // ni) % BF16_ROWS == 0 and k2 % ni == 0 and nj + 1 >= w2_col_blocks
    w2_blk = (k2 // ni, d2 // w2_col_blocks)
    rnn0 = d_attn + d_attn // 4 + d_attn
    assert rnn0 + 4 * d_rnn == n
    silu_tiles = _tile_ranges([(rnn0 - d_attn, rnn0), (rnn0, rnn0 + d_rnn),
                               (rnn0 + 3 * d_rnn, n)], tn)
    gate_tiles = _tile_ranges([(rnn0 + d_rnn, rnn0 + 2 * d_rnn)], tn)
    g0, g1 = gate_tiles[0]

    def w2_index(jo, i):
        done = jo >= w2_col_blocks
        return (jnp.where(done, ni - 1, i), jnp.where(done, w2_col_blocks - 1, jo))

    def glo_index(jo, i):
        tile = jo - 1
        row = jnp.where(tile < g0, 0, jnp.where(tile >= g1, ni - 1, i))
        return (row, jnp.clip(tile - g0, 0, g1 - g0 - 1))

    return pl.pallas_call(
        functools.partial(_in_proj_kernel, silu_tiles=silu_tiles, gate_tiles=gate_tiles,
                          layer=layer),
        out_shape=(jax.ShapeDtypeStruct((t, n), jnp.bfloat16),
                   jax.ShapeDtypeStruct((t, d_rnn), jnp.bfloat16),
                   jax.ShapeDtypeStruct((k2, d2), jnp.bfloat16)),
        grid=(nj + 1, ni),
        in_specs=[pl.BlockSpec((tm, d), lambda jo, i: (jnp.where(jo == 0, 0, i), 0)),
                  pl.BlockSpec((d // ni, tn), lambda jo, i: (i, jnp.minimum(jo, nj - 1))),
                  pl.BlockSpec(w2_blk, w2_index),
                  pl.BlockSpec((lb_logits.shape[0], tn),
                               lambda jo, i: (0, jnp.clip(jo - 1 - g0, 0, g1 - g0 - 1)))],
        out_specs=(pl.BlockSpec((tm, tn),
                                lambda jo, i: (jnp.where(jo == 0, 0, i), jnp.maximum(jo - 1, 0))),
                   pl.BlockSpec((tm, tn), glo_index),
                   pl.BlockSpec(w2_blk, w2_index)),
        scratch_shapes=[pltpu.VMEM((2, d, tn), jnp.bfloat16)],
        compiler_params=pltpu.CompilerParams(
            dimension_semantics=("arbitrary", "arbitrary"), vmem_limit_bytes=VMEM_LIMIT),
        name="in_proj",
    )(h, w, w2, lb_logits)


def _lane_halves(slab, head_in_high_half):
    lane = lax.broadcasted_iota(jnp.int32, slab.shape, 1)
    swapped = pltpu.roll(slab, ATTN_HEAD_DIM, axis=1)
    zero = jnp.zeros_like(slab)
    if head_in_high_half:
        lo, hi = swapped, slab
    else:
        lo, hi = slab, swapped
    return (jnp.where(lane < ATTN_HEAD_DIM, lo, zero), jnp.where(lane >= ATTN_HEAD_DIM, hi, zero))


def _swa_kernel(sink_ref, q_ref, kvp_ref, kvc_ref, g0_ref, g1_ref, g2_ref, g3_ref, o_ref,
                *, n_kv_heads, d_kv):
    n = pl.program_id(1)
    gate_refs = (g0_ref, g1_ref, g2_ref, g3_ref)
    w = WINDOW
    qi = lax.broadcasted_iota(jnp.int32, (w, w), 0)
    kj = lax.broadcasted_iota(jnp.int32, (w, w), 1)
    lower = kj <= qi
    prev_bias = jnp.where(n > 0, 0.0, -jnp.inf)
    lane = lax.broadcasted_iota(jnp.int32, (w, LANES), 1)
    scale = ATTN_HEAD_DIM ** -0.5 * LOG2_E
    pairs_per_kv = GQA_GROUP // 2

    kv_cache = {}

    def kv_operands(h):
        if h not in kv_cache:
            c0 = (h // 2) * LANES
            high = (h % 2) == 1
            k_slab = jnp.concatenate([kvp_ref[:, c0:c0 + LANES], kvc_ref[:, c0:c0 + LANES]],
                                     axis=0).astype(jnp.float32)
            v_slab = jnp.concatenate([kvp_ref[:, d_kv + c0:d_kv + c0 + LANES],
                                      kvc_ref[:, d_kv + c0:d_kv + c0 + LANES]],
                                     axis=0).astype(jnp.float32)
            k_lo, k_hi = _lane_halves(k_slab * scale, high)
            v_lo, v_hi = _lane_halves(v_slab, high)
            kv_cache[h] = (jnp.concatenate([k_lo, k_hi], axis=0).astype(jnp.bfloat16),
                           jnp.concatenate([v_lo, v_hi], axis=0).astype(jnp.bfloat16))
        return kv_cache[h]

    def score_stage(p, _):
        qp = q_ref[:, p * LANES:(p + 1) * LANES]
        kcat = kv_operands(p // pairs_per_kv)[0]
        return lax.dot_general(qp, kcat, _NT, preferred_element_type=jnp.float32)

    def max_stage(p, s):
        merged, mx = [], []
        for hh in range(2):
            prev = s[:, hh * 2 * w:hh * 2 * w + w] + prev_bias
            cur = s[:, hh * 2 * w + w:(hh + 1) * 2 * w]
            sh = jnp.where(lower, cur, prev)
            merged.append(sh)
            mx.append(jnp.maximum(jnp.max(sh, axis=-1, keepdims=True), sink_ref[2 * p + hh] * LOG2_E))
        return merged, mx

    def exp_stage(p, carry):
        merged, mx = carry
        probs, denom = [], []
        for hh in range(2):
            e = jnp.exp2(merged[hh] - mx[hh])
            denom.append(jnp.sum(e, axis=-1, keepdims=True)
                         + jnp.exp2(sink_ref[2 * p + hh] * LOG2_E - mx[hh]))
            e = e.astype(jnp.bfloat16)
            zero = jnp.zeros_like(e)
            probs += [jnp.where(lower, zero, e), jnp.where(lower, e, zero)]
        return jnp.concatenate(probs, axis=1), denom

    def value_stage(p, carry):
        pcat, denom = carry
        vcat = kv_operands(p // pairs_per_kv)[1]
        return jnp.dot(pcat, vcat, preferred_element_type=jnp.float32), denom

    def out_stage(p, carry):
        o, denom = carry
        o = o * jnp.where(lane < ATTN_HEAD_DIM, 1.0 / denom[0], 1.0 / denom[1])
        pairs_per_gate_ref = n_kv_heads * pairs_per_kv // len(gate_refs)
        gi, gp = divmod(p, pairs_per_gate_ref)
        gate = gate_refs[gi][:, gp * LANES:(gp + 1) * LANES].astype(jnp.float32)
        o_ref[:, p * LANES:(p + 1) * LANES] = (o * gate).astype(o_ref.dtype)

    _skewed(n_kv_heads * pairs_per_kv,
            [score_stage, max_stage, exp_stage, value_stage, out_stage])


def _swa(proj, sinks, batch, seq, d_attn):
    t = proj.shape[0]
    w = WINDOW
    nb = seq // w
    n_q = d_attn // ATTN_HEAD_DIM
    n_kv = n_q // GQA_GROUP
    d_kv = n_kv * ATTN_HEAD_DIM
    kvw = 2 * d_kv
    assert d_attn == 4 * kvw
    kv_blk = d_attn // kvw
    gate_blk0 = (d_attn + kvw) // kvw

    def row(b, n):
        return b * nb + n

    in_specs = [
        pl.BlockSpec(memory_space=pltpu.SMEM),
        pl.BlockSpec((w, d_attn), lambda b, n: (row(b, n), 0)),
        pl.BlockSpec((w, kvw), lambda b, n: (row(b, jnp.maximum(n - 1, 0)), kv_blk)),
        pl.BlockSpec((w, kvw), lambda b, n: (row(b, n), kv_blk)),
    ] + [
        pl.BlockSpec((w, kvw), functools.partial(lambda b, n, j: (row(b, n), gate_blk0 + j), j=j))
        for j in range(4)
    ]
    return pl.pallas_call(
        functools.partial(_swa_kernel, n_kv_heads=n_kv, d_kv=d_kv),
        out_shape=jax.ShapeDtypeStruct((t, d_attn), jnp.bfloat16),
        grid=(batch, nb),
        in_specs=in_specs,
        out_specs=pl.BlockSpec((w, d_attn), lambda b, n: (row(b, n), 0)),
        compiler_params=pltpu.CompilerParams(
            dimension_semantics=("parallel", "parallel"), vmem_limit_bytes=VMEM_LIMIT),
        name="swa",
    )(sinks, proj, proj, proj, proj, proj, proj, proj)


def _decay_tables():
    c = CHUNK
    n = np.zeros((N_LEVELS + 1, c, c), np.float32)
    r = np.arange(c)
    for l in range(1, N_LEVELS):
        half = 1 << l
        m = 2 * half
        for t in range(c):
            start = (t // m) * m
            mid = start + half - 1
            if t > mid:
                n[l - 1, t] = (r > mid) & (r <= t)
            else:
                n[l - 1, t] = (r > t) & (r <= mid)
    n[N_LEVELS - 1] = r[None, :] <= r[:, None]
    n[N_LEVELS] = r[None, :] > r[:, None]
    cumsum = np.concatenate([n[N_LEVELS - 1], n[N_LEVELS - 1]], axis=1)
    n = n.reshape((N_LEVELS + 1) * c, c)
    ncat = np.concatenate([n, n], axis=1)
    tt, ss = np.meshgrid(r, r, indexing="ij")
    x = tt ^ ss
    level = np.full((c, c), -1, np.int32)
    for l in range(N_LEVELS):
        level[(ss < tt) & (x >= (1 << l)) & (x < (2 << l))] = l
    return ncat, cumsum, level


def _interleave_rows(k, q, half):
    pieces = []
    for r0 in range(0, k.shape[0], half):
        src = q if (r0 // half) % 2 else k
        pieces.append(src[r0:r0 + half])
    return jnp.concatenate(pieces, axis=0)


def _hgrn2_kernel(ncat_ref, cum_ref, level_ref, gain_ref, q_ref, ghi_ref, glo_ref, v_ref, gate_ref,
                  o_ref, state_ref, k_s, *, heads, chunks):
    c = CHUNK
    pw = 2 * LANES
    n_pairs = heads // 2

    @pl.when(pl.program_id(2) == 0)
    def _():
        state_ref[...] = jnp.zeros_like(state_ref)

    worst = jnp.zeros((1, pw), jnp.float32)
    for ci in range(chunks):
        rows = slice(ci * c, (ci + 1) * c)
        for pr in range(n_pairs):
            cols = slice(pr * pw, (pr + 1) * pw)
            g2 = ghi_ref[rows, cols].astype(jnp.float32) + glo_ref[rows, cols].astype(jnp.float32)
            k_s[rows, cols] = 1.0 - jnp.exp2(g2)
            for half in range(2):
                hsum = jnp.sum(g2[half * (c // 2):(half + 1) * (c // 2)], axis=0, keepdims=True)
                worst = jnp.minimum(worst, hsum)
    mild = jnp.min(worst) > -SAFE_LOG2

    def head_matmuls(a_bf, qi, kl, v_bf, idx):
        intra = jnp.dot(a_bf, v_bf, preferred_element_type=jnp.float32)
        st = state_ref[idx]
        inter = lax.dot_general(qi, st.astype(jnp.bfloat16), _NT,
                                preferred_element_type=jnp.float32)
        upd = lax.dot_general(v_bf, kl, _TN, preferred_element_type=jnp.float32)
        return inter + intra, st, upd

    def head_tail(a_bf, qi, kl, egl, v_bf, idx):
        o, st, upd = head_matmuls(a_bf, qi, kl, v_bf, idx)
        state_ref[idx] = egl * st + upd
        return o

    def write_out(outs, rows, cols):
        normed = []
        for o in outs:
            ms = jnp.mean(o * o, axis=-1, keepdims=True)
            normed.append(o * lax.rsqrt(ms + NORM_EPS))
        o2 = jnp.concatenate(normed, axis=1) * gain_ref[:, cols]
        gate = gate_ref[rows, cols].astype(jnp.float32)
        o_ref[rows, cols] = (o2 * gate).astype(o_ref.dtype)

    @pl.when(mild)
    def _():
        ti = lax.broadcasted_iota(jnp.int32, (c, c), 0)
        si = lax.broadcasted_iota(jnp.int32, (c, c), 1)
        causal = si <= ti
        halves = [slice(hh * LANES, (hh + 1) * LANES) for hh in range(2)]

        def where(item):
            ci, pr = divmod(item, n_pairs)
            return slice(ci * c, (ci + 1) * c), slice(pr * pw, (pr + 1) * pw), pr

        def cumsum_stage(item, _):
            rows, cols, _ = where(item)
            gcat = jnp.concatenate([ghi_ref[rows, cols], glo_ref[rows, cols]], axis=0)
            return jnp.dot(cum_ref[...], gcat, preferred_element_type=jnp.float32)

        def scale_stage(item, gsum):
            rows, cols, _ = where(item)
            ref = gsum[c // 2 - 1:c // 2, :]
            last = gsum[c - 1:c, :]
            d = gsum - ref
            qp = q_ref[rows, cols].astype(jnp.float32) * jnp.exp2(d)
            kp = k_s[rows, cols] * jnp.exp2(-d)
            qi = (qp * jnp.exp2(ref)).astype(jnp.bfloat16)
            kl = (kp * jnp.exp2(last - ref)).astype(jnp.bfloat16)
            qp = qp.astype(jnp.bfloat16)
            kp = kp.astype(jnp.bfloat16)
            a = [lax.dot_general(qp[:, hs], kp[:, hs], _NT, preferred_element_type=jnp.float32)
                 for hs in halves]
            return a, qi, kl, jnp.exp2(last)

        def mix_stage(item, carry):
            a, qi, kl, egl = carry
            rows, cols, pr = where(item)
            v_bf = v_ref[rows, cols]
            return [head_matmuls(jnp.where(causal, a[hh], 0.0).astype(jnp.bfloat16),
                                 qi[:, hs], kl[:, hs], v_bf[:, hs], 2 * pr + hh)
                    for hh, hs in enumerate(halves)], egl

        def out_stage(item, carry):
            heads_out, egl = carry
            rows, cols, pr = where(item)
            for hh, hs in enumerate(halves):
                _, st, upd = heads_out[hh]
                state_ref[2 * pr + hh] = egl[:, hs] * st + upd
            write_out([o for o, _, _ in heads_out], rows, cols)

        _skewed(chunks * n_pairs, [cumsum_stage, scale_stage, mix_stage, out_stage],
                gap=min(SKEW_GAP, n_pairs))

    @pl.when(jnp.logical_not(mild))
    def _():
        level = level_ref[...]
        trow = lax.broadcasted_iota(jnp.int32, (c, pw), 0)

        def chunk_body(ci, carry):
            rows = pl.ds(pl.multiple_of(ci * c, c), c)
            for pr in range(n_pairs):
                cols = slice(pr * pw, (pr + 1) * pw)
                q = q_ref[rows, cols].astype(jnp.float32)
                k = k_s[rows, cols]
                g_hi = ghi_ref[rows, cols]
                g_lo = glo_ref[rows, cols]
                f = jnp.exp2(g_hi.astype(jnp.float32) + g_lo.astype(jnp.float32))
                gcat = jnp.concatenate([g_hi, g_lo], axis=0)
                ex = jnp.dot(ncat_ref[...], gcat, preferred_element_type=jnp.float32)

                z = [jnp.where((trow & 1) == 1, q * f, k).astype(jnp.bfloat16)]
                for l in range(1, N_LEVELS):
                    half = 1 << l
                    if half < SUBLANES:
                        qk = jnp.where((trow & half) != 0, q, k)
                    else:
                        qk = _interleave_rows(k, q, half)
                    z.append((qk * jnp.exp2(ex[(l - 1) * c:l * c])).astype(jnp.bfloat16))
                eg = jnp.exp2(ex[(N_LEVELS - 1) * c:N_LEVELS * c])
                erest = jnp.exp2(ex[N_LEVELS * c:(N_LEVELS + 1) * c])
                qi = (q * eg).astype(jnp.bfloat16)
                kl = (k * erest).astype(jnp.bfloat16)
                qk_diag = q * k
                v_bf = v_ref[rows, cols]
                outs = []
                for hh in range(2):
                    hs = slice(hh * LANES, (hh + 1) * LANES)
                    attn = jnp.zeros((c, c), jnp.float32)
                    for l in range(N_LEVELS):
                        zl = z[l][:, hs]
                        a = lax.dot_general(zl, zl, _NT, preferred_element_type=jnp.float32)
                        attn = jnp.where(level == l, a, attn)
                    diag = jnp.sum(qk_diag[:, hs], axis=-1, keepdims=True)
                    o = head_tail(attn.astype(jnp.bfloat16), qi[:, hs], kl[:, hs],
                                  eg[c - 1:c, hs], v_bf[:, hs], 2 * pr + hh)
                    outs.append(o + diag * v_bf[:, hs].astype(jnp.float32))
                write_out(outs, rows, cols)
            return carry

        lax.fori_loop(0, chunks, chunk_body, 0)


def _hgrn2(proj, glo, gain, batch, seq, d_attn, d_rnn, heads=8, chunks=8):
    t = proj.shape[0]
    rows = chunks * CHUNK
    if seq % rows:
        chunks, rows = 1, CHUNK
    nl = seq // rows
    wb = heads * RNN_HEAD_DIM
    n_hg = d_rnn // wb
    kvw = d_attn // 4
    base = d_attn + kvw + d_attn
    assert base % wb == 0 and heads % 2 == 0
    ncat, cumsum, level = _decay_tables()

    def seg(i):
        blk0 = (base + i * d_rnn) // wb
        return pl.BlockSpec((rows, wb), lambda b, hg, l: (b * nl + l, blk0 + hg))

    const = lambda shape: pl.BlockSpec(shape, lambda b, hg, l: (0, 0))
    own = pl.BlockSpec((rows, wb), lambda b, hg, l: (b * nl + l, hg))
    return pl.pallas_call(
        functools.partial(_hgrn2_kernel, heads=heads, chunks=chunks),
        out_shape=jax.ShapeDtypeStruct((t, d_rnn), jnp.bfloat16),
        grid=(batch, n_hg, nl),
        in_specs=[
            const(ncat.shape), const(cumsum.shape), const(level.shape),
            pl.BlockSpec((1, wb), lambda b, hg, l: (0, hg)),
            seg(0), seg(1), own, seg(2), seg(3),
        ],
        out_specs=own,
        scratch_shapes=[pltpu.VMEM((heads, RNN_HEAD_DIM, RNN_HEAD_DIM), jnp.float32),
                        pltpu.VMEM((rows, wb), jnp.float32)],
        compiler_params=pltpu.CompilerParams(
            dimension_semantics=("parallel", "parallel", "arbitrary"),
            vmem_limit_bytes=VMEM_LIMIT),
        name="hgrn2",
    )(jnp.asarray(ncat, jnp.bfloat16), jnp.asarray(cumsum, jnp.bfloat16), jnp.asarray(level),
      gain.reshape(1, d_rnn), proj, proj, glo, proj, proj)


def _out_proj_kernel(a1_ref, a2_ref, w_ref, x_ref, g_ref, o_ref, acc_ref, y_ref, ssq_ref,
                     *, nk1, n_tiles):
    r = pl.program_id(0)
    kk = pl.program_id(1)
    nk = pl.num_programs(1)
    d = acc_ref.shape[1]
    te = o_ref.shape[0]

    def residual_chunk():
        rows = pl.ds(pl.multiple_of(kk * te, te), te)
        inv = lax.rsqrt(ssq_ref[rows, :] * (1.0 / d) + NORM_EPS)
        o_ref[...] = x_ref[...] + y_ref[rows, :].astype(jnp.float32) * inv * g_ref[...]

    def accumulate(last):
        a = jnp.where(kk < nk1, a1_ref[...], a2_ref[...])
        ssq = None
        for n0 in range(0, d, OUT_COL_CHUNK):
            cols = slice(n0, n0 + OUT_COL_CHUNK)
            part = jnp.dot(a, w_ref[:, cols], preferred_element_type=jnp.float32)
            y = jnp.where(kk == 0, part, acc_ref[:, cols] + part)
            if last:
                y_ref[:, cols] = y.astype(y_ref.dtype)
                s = jnp.sum(y * y, axis=-1, keepdims=True)
                ssq = s if ssq is None else ssq + s
            else:
                acc_ref[:, cols] = y
        if last:
            ssq_ref[...] = ssq

    first, drain = r == 0, r == n_tiles
    mid = jnp.logical_not(first | drain)
    last_slab = kk == nk - 1

    @pl.when(first & (kk == 0))
    def _():
        acc_ref[...] = jnp.zeros_like(acc_ref)

    @pl.when(first & jnp.logical_not(last_slab))
    def _():
        accumulate(False)

    @pl.when(first & last_slab)
    def _():
        accumulate(True)

    @pl.when(mid & jnp.logical_not(last_slab))
    def _():
        residual_chunk()
        accumulate(False)

    @pl.when(mid & last_slab)
    def _():
        residual_chunk()
        accumulate(True)

    @pl.when(drain)
    def _():
        residual_chunk()


def _out_proj(a1, a2, w, x2d, gain, tm=1024, tk=512):
    t, k1 = a1.shape
    k2 = a2.shape[1]
    d = w.shape[1]
    tm = min(tm, t)
    n_tiles = t // tm
    nk1, nk2 = k1 // tk, k2 // tk
    nk = nk1 + nk2
    te = tm // nk
    assert tm % nk == 0 and te % SUBLANES == 0

    def a_row(r):
        return jnp.minimum(r, n_tiles - 1)

    def residual_block(r, k):
        return (jnp.where(r == 0, 0, (r - 1) * nk + k), 0)

    return pl.pallas_call(
        functools.partial(_out_proj_kernel, nk1=nk1, n_tiles=n_tiles),
        out_shape=jax.ShapeDtypeStruct((t, d), jnp.float32),
        grid=(n_tiles + 1, nk),
        in_specs=[pl.BlockSpec((tm, tk), lambda r, k: (a_row(r), jnp.minimum(k, nk1 - 1))),
                  pl.BlockSpec((tm, tk), lambda r, k: (a_row(r), jnp.clip(k - nk1, 0, nk2 - 1))),
                  pl.BlockSpec((tk, d), lambda r, k: (jnp.where(r == n_tiles, nk - 1, k), 0)),
                  pl.BlockSpec((te, d), residual_block),
                  pl.BlockSpec((1, d), lambda r, k: (0, 0))],
        out_specs=pl.BlockSpec((te, d), residual_block),
        scratch_shapes=[pltpu.VMEM((tm, d), jnp.float32),
                        pltpu.VMEM((tm, d), jnp.bfloat16),
                        pltpu.VMEM((tm, 1), jnp.float32)],
        compiler_params=pltpu.CompilerParams(
            dimension_semantics=("arbitrary", "arbitrary"), vmem_limit_bytes=VMEM_LIMIT),
        name="out_proj",
    )(a1, a2, w, x2d, gain.reshape(1, d))


def kernel(x, w_in, attn_sinks, lb_logits, rnn_norm, w_out, pre_norm, post_norm):
    batch, seq, d_model = x.shape
    depth = w_in.shape[0]
    d_mix = w_out.shape[1]
    d_attn = d_mix // 2
    d_rnn = d_mix - d_attn
    x2d = x.reshape(batch * seq, d_model)
    for layer in range(depth):
        h = _prenorm(x2d, pre_norm[layer])
        proj, glo, w_out_bf = _in_proj(h, w_in[layer], w_out[layer], lb_logits, d_attn, d_rnn, layer)
        attn = _swa(proj, attn_sinks[layer], batch, seq, d_attn)
        rnn = _hgrn2(proj, glo, rnn_norm[layer], batch, seq, d_attn, d_rnn)
        x2d = _out_proj(attn, rnn, w_out_bf, x2d, post_norm[layer])
    return x2d.reshape(batch, seq, d_model)
```

```python
import functools

import jax
import jax.numpy as jnp
import numpy as np
from jax import lax
from jax.experimental import pallas as pl
from jax.experimental.pallas import tpu as pltpu

ATTN_HEAD_DIM = 64
GQA_GROUP = 8
WINDOW = 128
RNN_HEAD_DIM = 128
NORM_EPS = 1e-6

LANES = 128
SUBLANES = 8
BF16_ROWS = 2 * SUBLANES
CHUNK = 128
N_LEVELS = 7
SAFE_LOG2 = 100.0
LOG2_E = 1.4426950408889634
VMEM_LIMIT = 56 * 1024 * 1024
SKEW_GAP = 3
PROJ_TM = 1024
PROJ_TN = 1024
GATE_ROW_CHUNKS = 4
OUT_COL_CHUNK = 1024

_NT = (((1,), (1,)), ((), ()))
_TN = (((0,), (0,)), ((), ()))


def _skewed(n_items, stages, gap=SKEW_GAP):
    carry = [None] * n_items
    for step in range(n_items + gap * (len(stages) - 1)):
        for j in reversed(range(len(stages))):
            i = step - gap * j
            if 0 <= i < n_items:
                carry[i] = stages[j](i, carry[i])


def _silu(x):
    h = 0.5 * x
    return h + h * jnp.tanh(h)


def _prenorm_kernel(x_ref, g_ref, o_ref):
    x = x_ref[...]
    ms = jnp.mean(x * x, axis=-1, keepdims=True)
    o_ref[...] = (x * lax.rsqrt(ms + NORM_EPS) * g_ref[...]).astype(o_ref.dtype)


def _prenorm(x2d, gain, rows=512):
    t, d = x2d.shape
    rows = min(rows, t)
    return pl.pallas_call(
        _prenorm_kernel,
        out_shape=jax.ShapeDtypeStruct((t, d), jnp.bfloat16),
        grid=(t // rows,),
        in_specs=[pl.BlockSpec((rows, d), lambda i: (i, 0)),
                  pl.BlockSpec((1, d), lambda i: (0, 0))],
        out_specs=pl.BlockSpec((rows, d), lambda i: (i, 0)),
        compiler_params=pltpu.CompilerParams(
            dimension_semantics=("parallel",), vmem_limit_bytes=VMEM_LIMIT),
        name="prenorm",
    )(x2d, gain.reshape(1, d))


def _in_proj_kernel(h_ref, wchunk_ref, w2_ref, lbl_ref, o_ref, glo_ref, worst_ref, w2o_ref, wbuf_ref,
                    *, silu_tiles, gate_tiles, layer):
    jo = pl.program_id(0)
    i = pl.program_id(1)
    rows = wchunk_ref.shape[0]
    chunk = pl.ds(pl.multiple_of(i * rows, rows), rows)
    tile = jo - 1

    def in_ranges(ranges):
        hits = [(tile >= lo) & (tile < hi) for lo, hi in ranges]
        return functools.reduce(jnp.logical_or, hits)

    is_silu = in_ranges(silu_tiles)
    is_gate = in_ranges(gate_tiles)

    def casts_and_dot(slot):
        wbuf_ref[slot, chunk, :] = wchunk_ref[...].astype(wbuf_ref.dtype)
        w2o_ref[...] = w2_ref[...].astype(w2o_ref.dtype)
        return jnp.dot(h_ref[...], wbuf_ref[1 - slot], preferred_element_type=jnp.float32)

    @pl.when(jo == 0)
    def _():
        wbuf_ref[0, chunk, :] = wchunk_ref[...].astype(wbuf_ref.dtype)
        w2o_ref[...] = w2_ref[...].astype(w2o_ref.dtype)

    @pl.when((jo > 0) & jnp.logical_not(is_silu | is_gate))
    def _():
        o_ref[...] = casts_and_dot(jo % 2).astype(o_ref.dtype)

    @pl.when(is_silu)
    def _():
        o_ref[...] = _silu(casts_and_dot(jo % 2)).astype(o_ref.dtype)

    @pl.when(is_gate)
    def _():
        lbl = lbl_ref[...]
        e = jnp.exp(lbl - jnp.max(lbl, axis=0, keepdims=True))
        lb = jnp.sum(e[:layer + 1], axis=0, keepdims=True) / jnp.sum(e, axis=0, keepdims=True)
        fa = 0.5 * (1.0 + lb)
        fb = 0.5 * (1.0 - lb)
        slot = jo % 2
        wbuf_ref[slot, chunk, :] = wchunk_ref[...].astype(wbuf_ref.dtype)
        w2o_ref[...] = w2_ref[...].astype(w2o_ref.dtype)
        rc = o_ref.shape[0] // GATE_ROW_CHUNKS
        half = CHUNK // 2
        worst = None
        for r in range(GATE_ROW_CHUNKS):
            rs = slice(r * rc, (r + 1) * rc)
            acc = jnp.dot(h_ref[rs, :], wbuf_ref[1 - slot], preferred_element_type=jnp.float32)
            g2 = jnp.log(fa + fb * jnp.tanh(0.5 * acc)) * LOG2_E
            g_hi = g2.astype(o_ref.dtype)
            o_ref[rs, :] = g_hi
            glo_ref[rs, :] = (g2 - g_hi.astype(jnp.float32)).astype(glo_ref.dtype)
            sums = jnp.sum(g2.reshape(rc // half, half, g2.shape[1]), axis=1)
            low = jnp.min(sums, axis=0, keepdims=True)
            worst = low if worst is None else jnp.minimum(worst, low)
        worst_ref[...] = jnp.broadcast_to(jnp.min(worst, axis=1, keepdims=True), worst_ref.shape)


def _tile_ranges(col_ranges, tn):
    assert all(lo % tn == 0 and hi % tn == 0 for lo, hi in col_ranges)
    return tuple((lo // tn, hi // tn) for lo, hi in col_ranges)


def _in_proj(h, w, w2, lb_logits, d_attn, d_rnn, layer, tm, tn=PROJ_TN, w2_col_blocks=16):
    t, d = h.shape
    n = w.shape[1]
    k2, d2 = w2.shape
    ni, nj = t // tm, n // tn
    assert d % ni == 0 and (d // ni) % BF16_ROWS == 0 and k2 % ni == 0 and nj + 1 >= w2_col_blocks
    w2_blk = (k2 // ni, d2 // w2_col_blocks)
    rnn0 = d_attn + d_attn // 4 + d_attn
    assert rnn0 + 4 * d_rnn == n
    silu_tiles = _tile_ranges([(rnn0 - d_attn, rnn0), (rnn0, rnn0 + d_rnn),
                               (rnn0 + 3 * d_rnn, n)], tn)
    gate_tiles = _tile_ranges([(rnn0 + d_rnn, rnn0 + 2 * d_rnn)], tn)
    g0, g1 = gate_tiles[0]

    def w2_index(jo, i):
        done = jo >= w2_col_blocks
        return (jnp.where(done, ni - 1, i), jnp.where(done, w2_col_blocks - 1, jo))

    def glo_index(jo, i):
        tile = jo - 1
        row = jnp.where(tile < g0, 0, jnp.where(tile >= g1, ni - 1, i))
        return (row, jnp.clip(tile - g0, 0, g1 - g0 - 1))

    return pl.pallas_call(
        functools.partial(_in_proj_kernel, silu_tiles=silu_tiles, gate_tiles=gate_tiles,
                          layer=layer),
        out_shape=(jax.ShapeDtypeStruct((t, n), jnp.bfloat16),
                   jax.ShapeDtypeStruct((t, d_rnn), jnp.bfloat16),
                   jax.ShapeDtypeStruct((ni * SUBLANES, (g1 - g0) * LANES), jnp.float32),
                   jax.ShapeDtypeStruct((k2, d2), jnp.bfloat16)),
        grid=(nj + 1, ni),
        in_specs=[pl.BlockSpec((tm, d), lambda jo, i: (jnp.where(jo == 0, 0, i), 0)),
                  pl.BlockSpec((d // ni, tn), lambda jo, i: (i, jnp.minimum(jo, nj - 1))),
                  pl.BlockSpec(w2_blk, w2_index),
                  pl.BlockSpec((lb_logits.shape[0], tn),
                               lambda jo, i: (0, jnp.clip(jo - 1 - g0, 0, g1 - g0 - 1)))],
        out_specs=(pl.BlockSpec((tm, tn),
                                lambda jo, i: (jnp.where(jo == 0, 0, i), jnp.maximum(jo - 1, 0))),
                   pl.BlockSpec((tm, tn), glo_index),
                   pl.BlockSpec((SUBLANES, LANES), glo_index),
                   pl.BlockSpec(w2_blk, w2_index)),
        scratch_shapes=[pltpu.VMEM((2, d, tn), jnp.bfloat16)],
        compiler_params=pltpu.CompilerParams(
            dimension_semantics=("arbitrary", "arbitrary"), vmem_limit_bytes=VMEM_LIMIT),
        name="in_proj",
    )(h, w, w2, lb_logits)


def _lane_halves(slab, head_in_high_half):
    lane = lax.broadcasted_iota(jnp.int32, slab.shape, 1)
    swapped = pltpu.roll(slab, ATTN_HEAD_DIM, axis=1)
    zero = jnp.zeros_like(slab)
    if head_in_high_half:
        lo, hi = swapped, slab
    else:
        lo, hi = slab, swapped
    return (jnp.where(lane < ATTN_HEAD_DIM, lo, zero), jnp.where(lane >= ATTN_HEAD_DIM, hi, zero))


def _swa_kernel(sink_ref, q_ref, kvp_ref, kvc_ref, g0_ref, g1_ref, g2_ref, g3_ref, o_ref,
                *, n_kv_heads, d_kv):
    n = pl.program_id(1)
    gate_refs = (g0_ref, g1_ref, g2_ref, g3_ref)
    w = WINDOW
    qi = lax.broadcasted_iota(jnp.int32, (w, w), 0)
    kj = lax.broadcasted_iota(jnp.int32, (w, w), 1)
    lower = kj <= qi
    prev_bias = jnp.where(n > 0, 0.0, -jnp.inf)
    lane = lax.broadcasted_iota(jnp.int32, (w, LANES), 1)
    scale = ATTN_HEAD_DIM ** -0.5 * LOG2_E
    pairs_per_kv = GQA_GROUP // 2

    kv_cache = {}

    def kv_operands(h):
        if h not in kv_cache:
            c0 = (h // 2) * LANES
            high = (h % 2) == 1
            k_slab = jnp.concatenate([kvp_ref[:, c0:c0 + LANES], kvc_ref[:, c0:c0 + LANES]],
                                     axis=0).astype(jnp.float32)
            v_slab = jnp.concatenate([kvp_ref[:, d_kv + c0:d_kv + c0 + LANES],
                                      kvc_ref[:, d_kv + c0:d_kv + c0 + LANES]],
                                     axis=0).astype(jnp.float32)
            k_lo, k_hi = _lane_halves(k_slab * scale, high)
            v_lo, v_hi = _lane_halves(v_slab, high)
            kv_cache[h] = (jnp.concatenate([k_lo, k_hi], axis=0).astype(jnp.bfloat16),
                           jnp.concatenate([v_lo, v_hi], axis=0).astype(jnp.bfloat16))
        return kv_cache[h]

    def score_stage(p, _):
        qp = q_ref[:, p * LANES:(p + 1) * LANES]
        kcat = kv_operands(p // pairs_per_kv)[0]
        return lax.dot_general(qp, kcat, _NT, preferred_element_type=jnp.float32)

    def max_stage(p, s):
        merged, mx = [], []
        for hh in range(2):
            prev = s[:, hh * 2 * w:hh * 2 * w + w] + prev_bias
            cur = s[:, hh * 2 * w + w:(hh + 1) * 2 * w]
            sh = jnp.where(lower, cur, prev)
            merged.append(sh)
            mx.append(jnp.maximum(jnp.max(sh, axis=-1, keepdims=True), sink_ref[2 * p + hh] * LOG2_E))
        return merged, mx

    def exp_stage(p, carry):
        merged, mx = carry
        probs, denom = [], []
        for hh in range(2):
            e = jnp.exp2(merged[hh] - mx[hh])
            denom.append(jnp.sum(e, axis=-1, keepdims=True)
                         + jnp.exp2(sink_ref[2 * p + hh] * LOG2_E - mx[hh]))
            e = e.astype(jnp.bfloat16)
            zero = jnp.zeros_like(e)
            probs += [jnp.where(lower, zero, e), jnp.where(lower, e, zero)]
        return jnp.concatenate(probs, axis=1), denom

    def value_stage(p, carry):
        pcat, denom = carry
        vcat = kv_operands(p // pairs_per_kv)[1]
        return jnp.dot(pcat, vcat, preferred_element_type=jnp.float32), denom

    def out_stage(p, carry):
        o, denom = carry
        o = o * jnp.where(lane < ATTN_HEAD_DIM, 1.0 / denom[0], 1.0 / denom[1])
        pairs_per_gate_ref = n_kv_heads * pairs_per_kv // len(gate_refs)
        gi, gp = divmod(p, pairs_per_gate_ref)
        gate = gate_refs[gi][:, gp * LANES:(gp + 1) * LANES].astype(jnp.float32)
        o_ref[:, p * LANES:(p + 1) * LANES] = (o * gate).astype(o_ref.dtype)

    _skewed(n_kv_heads * pairs_per_kv,
            [score_stage, max_stage, exp_stage, value_stage, out_stage])


def _swa(proj, sinks, batch, seq, d_attn):
    t = proj.shape[0]
    w = WINDOW
    nb = seq // w
    n_q = d_attn // ATTN_HEAD_DIM
    n_kv = n_q // GQA_GROUP
    d_kv = n_kv * ATTN_HEAD_DIM
    kvw = 2 * d_kv
    assert d_attn == 4 * kvw
    kv_blk = d_attn // kvw
    gate_blk0 = (d_attn + kvw) // kvw

    def row(b, n):
        return b * nb + n

    in_specs = [
        pl.BlockSpec(memory_space=pltpu.SMEM),
        pl.BlockSpec((w, d_attn), lambda b, n: (row(b, n), 0)),
        pl.BlockSpec((w, kvw), lambda b, n: (row(b, jnp.maximum(n - 1, 0)), kv_blk)),
        pl.BlockSpec((w, kvw), lambda b, n: (row(b, n), kv_blk)),
    ] + [
        pl.BlockSpec((w, kvw), functools.partial(lambda b, n, j: (row(b, n), gate_blk0 + j), j=j))
        for j in range(4)
    ]
    return pl.pallas_call(
        functools.partial(_swa_kernel, n_kv_heads=n_kv, d_kv=d_kv),
        out_shape=jax.ShapeDtypeStruct((t, d_attn), jnp.bfloat16),
        grid=(batch, nb),
        in_specs=in_specs,
        out_specs=pl.BlockSpec((w, d_attn), lambda b, n: (row(b, n), 0)),
        compiler_params=pltpu.CompilerParams(
            dimension_semantics=("parallel", "parallel"), vmem_limit_bytes=VMEM_LIMIT),
        name="swa",
    )(sinks, proj, proj, proj, proj, proj, proj, proj)


def _decay_tables():
    c = CHUNK
    n = np.zeros((N_LEVELS + 1, c, c), np.float32)
    r = np.arange(c)
    for l in range(1, N_LEVELS):
        half = 1 << l
        m = 2 * half
        for t in range(c):
            start = (t // m) * m
            mid = start + half - 1
            if t > mid:
                n[l - 1, t] = (r > mid) & (r <= t)
            else:
                n[l - 1, t] = (r > t) & (r <= mid)
    n[N_LEVELS - 1] = r[None, :] <= r[:, None]
    n[N_LEVELS] = r[None, :] > r[:, None]
    cumsum = np.concatenate([n[N_LEVELS - 1], n[N_LEVELS - 1]], axis=1)
    n = n.reshape((N_LEVELS + 1) * c, c)
    ncat = np.concatenate([n, n], axis=1)
    tt, ss = np.meshgrid(r, r, indexing="ij")
    x = tt ^ ss
    level = np.full((c, c), -1, np.int32)
    for l in range(N_LEVELS):
        level[(ss < tt) & (x >= (1 << l)) & (x < (2 << l))] = l
    return ncat, cumsum, level


def _interleave_rows(k, q, half):
    pieces = []
    for r0 in range(0, k.shape[0], half):
        src = q if (r0 // half) % 2 else k
        pieces.append(src[r0:r0 + half])
    return jnp.concatenate(pieces, axis=0)


def _hgrn2_kernel(ncat_ref, cum_ref, level_ref, gain_ref, worst_ref, q_ref, ghi_ref, glo_ref, v_ref,
                  gate_ref, o_ref, state_ref, *, heads, chunks):
    c = CHUNK
    pw = 2 * LANES
    n_pairs = heads // 2

    @pl.when(pl.program_id(2) == 0)
    def _():
        state_ref[...] = jnp.zeros_like(state_ref)

    mild = jnp.min(worst_ref[...]) > -SAFE_LOG2

    def key_gate(rows, cols):
        g2 = ghi_ref[rows, cols].astype(jnp.float32) + glo_ref[rows, cols].astype(jnp.float32)
        return 1.0 - jnp.exp2(g2)

    def head_matmuls(a_bf, qi, kl, v_bf, idx):
        intra = jnp.dot(a_bf, v_bf, preferred_element_type=jnp.float32)
        st = state_ref[idx]
        inter = lax.dot_general(qi, st.astype(jnp.bfloat16), _NT,
                                preferred_element_type=jnp.float32)
        upd = lax.dot_general(v_bf, kl, _TN, preferred_element_type=jnp.float32)
        return inter + intra, st, upd

    def head_tail(a_bf, qi, kl, egl, v_bf, idx):
        o, st, upd = head_matmuls(a_bf, qi, kl, v_bf, idx)
        state_ref[idx] = egl * st + upd
        return o

    def write_out(outs, rows, cols):
        normed = []
        for o in outs:
            ms = jnp.mean(o * o, axis=-1, keepdims=True)
            normed.append(o * lax.rsqrt(ms + NORM_EPS))
        o2 = jnp.concatenate(normed, axis=1) * gain_ref[:, cols]
        gate = gate_ref[rows, cols].astype(jnp.float32)
        o_ref[rows, cols] = (o2 * gate).astype(o_ref.dtype)

    @pl.when(mild)
    def _():
        ti = lax.broadcasted_iota(jnp.int32, (c, c), 0)
        si = lax.broadcasted_iota(jnp.int32, (c, c), 1)
        causal = si <= ti
        halves = [slice(hh * LANES, (hh + 1) * LANES) for hh in range(2)]

        def where(item):
            ci, pr = divmod(item, n_pairs)
            return slice(ci * c, (ci + 1) * c), slice(pr * pw, (pr + 1) * pw), pr

        def cumsum_stage(item, _):
            rows, cols, _ = where(item)
            gcat = jnp.concatenate([ghi_ref[rows, cols], glo_ref[rows, cols]], axis=0)
            return jnp.dot(cum_ref[...], gcat, preferred_element_type=jnp.float32)

        def scale_stage(item, gsum):
            rows, cols, _ = where(item)
            ref = gsum[c // 2 - 1:c // 2, :]
            last = gsum[c - 1:c, :]
            d = gsum - ref
            qp = q_ref[rows, cols].astype(jnp.float32) * jnp.exp2(d)
            kp = key_gate(rows, cols) * jnp.exp2(-d)
            qi = (qp * jnp.exp2(ref)).astype(jnp.bfloat16)
            kl = (kp * jnp.exp2(last - ref)).astype(jnp.bfloat16)
            qp = qp.astype(jnp.bfloat16)
            kp = kp.astype(jnp.bfloat16)
            a = [lax.dot_general(qp[:, hs], kp[:, hs], _NT, preferred_element_type=jnp.float32)
                 for hs in halves]
            return a, qi, kl, jnp.exp2(last)

        def mix_stage(item, carry):
            a, qi, kl, egl = carry
            rows, cols, pr = where(item)
            v_bf = v_ref[rows, cols]
            return [head_matmuls(jnp.where(causal, a[hh], 0.0).astype(jnp.bfloat16),
                                 qi[:, hs], kl[:, hs], v_bf[:, hs], 2 * pr + hh)
                    for hh, hs in enumerate(halves)], egl

        def out_stage(item, carry):
            heads_out, egl = carry
            rows, cols, pr = where(item)
            for hh, hs in enumerate(halves):
                _, st, upd = heads_out[hh]
                state_ref[2 * pr + hh] = egl[:, hs] * st + upd
            write_out([o for o, _, _ in heads_out], rows, cols)

        _skewed(chunks * n_pairs, [cumsum_stage, scale_stage, mix_stage, out_stage],
                gap=min(SKEW_GAP, n_pairs))

    @pl.when(jnp.logical_not(mild))
    def _():
        level = level_ref[...]
        trow = lax.broadcasted_iota(jnp.int32, (c, pw), 0)

        def chunk_body(ci, carry):
            rows = pl.ds(pl.multiple_of(ci * c, c), c)
            for pr in range(n_pairs):
                cols = slice(pr * pw, (pr + 1) * pw)
                q = q_ref[rows, cols].astype(jnp.float32)
                g_hi = ghi_ref[rows, cols]
                g_lo = glo_ref[rows, cols]
                f = jnp.exp2(g_hi.astype(jnp.float32) + g_lo.astype(jnp.float32))
                k = 1.0 - f
                gcat = jnp.concatenate([g_hi, g_lo], axis=0)
                ex = jnp.dot(ncat_ref[...], gcat, preferred_element_type=jnp.float32)

                z = [jnp.where((trow & 1) == 1, q * f, k).astype(jnp.bfloat16)]
                for l in range(1, N_LEVELS):
                    half = 1 << l
                    if half < SUBLANES:
                        qk = jnp.where((trow & half) != 0, q, k)
                    else:
                        qk = _interleave_rows(k, q, half)
                    z.append((qk * jnp.exp2(ex[(l - 1) * c:l * c])).astype(jnp.bfloat16))
                eg = jnp.exp2(ex[(N_LEVELS - 1) * c:N_LEVELS * c])
                erest = jnp.exp2(ex[N_LEVELS * c:(N_LEVELS + 1) * c])
                qi = (q * eg).astype(jnp.bfloat16)
                kl = (k * erest).astype(jnp.bfloat16)
                qk_diag = q * k
                v_bf = v_ref[rows, cols]
                outs = []
                for hh in range(2):
                    hs = slice(hh * LANES, (hh + 1) * LANES)
                    attn = jnp.zeros((c, c), jnp.float32)
                    for l in range(N_LEVELS):
                        zl = z[l][:, hs]
                        a = lax.dot_general(zl, zl, _NT, preferred_element_type=jnp.float32)
                        attn = jnp.where(level == l, a, attn)
                    diag = jnp.sum(qk_diag[:, hs], axis=-1, keepdims=True)
                    o = head_tail(attn.astype(jnp.bfloat16), qi[:, hs], kl[:, hs],
                                  eg[c - 1:c, hs], v_bf[:, hs], 2 * pr + hh)
                    outs.append(o + diag * v_bf[:, hs].astype(jnp.float32))
                write_out(outs, rows, cols)
            return carry

        lax.fori_loop(0, chunks, chunk_body, 0)


def _hgrn2(proj, glo, worst, gain, batch, seq, d_attn, d_rnn, rows, wb=PROJ_TN):
    t = proj.shape[0]
    assert seq % rows == 0 and rows % CHUNK == 0 and wb % (2 * RNN_HEAD_DIM) == 0
    chunks = rows // CHUNK
    heads = wb // RNN_HEAD_DIM
    nl = seq // rows
    n_hg = d_rnn // wb
    kvw = d_attn // 4
    base = d_attn + kvw + d_attn
    assert base % wb == 0
    ncat, cumsum, level = _decay_tables()

    def seg(i):
        blk0 = (base + i * d_rnn) // wb
        return pl.BlockSpec((rows, wb), lambda b, hg, l: (b * nl + l, blk0 + hg))

    const = lambda shape: pl.BlockSpec(shape, lambda b, hg, l: (0, 0))
    own = pl.BlockSpec((rows, wb), lambda b, hg, l: (b * nl + l, hg))
    return pl.pallas_call(
        functools.partial(_hgrn2_kernel, heads=heads, chunks=chunks),
        out_shape=jax.ShapeDtypeStruct((t, d_rnn), jnp.bfloat16),
        grid=(batch, n_hg, nl),
        in_specs=[
            const(ncat.shape), const(cumsum.shape), const(level.shape),
            pl.BlockSpec((1, wb), lambda b, hg, l: (0, hg)),
            pl.BlockSpec((SUBLANES, LANES), lambda b, hg, l: (b * nl + l, hg)),
            seg(0), seg(1), own, seg(2), seg(3),
        ],
        out_specs=own,
        scratch_shapes=[pltpu.VMEM((heads, RNN_HEAD_DIM, RNN_HEAD_DIM), jnp.float32)],
        compiler_params=pltpu.CompilerParams(
            dimension_semantics=("parallel", "parallel", "arbitrary"),
            vmem_limit_bytes=VMEM_LIMIT),
        name="hgrn2",
    )(jnp.asarray(ncat, jnp.bfloat16), jnp.asarray(cumsum, jnp.bfloat16), jnp.asarray(level),
      gain.reshape(1, d_rnn), worst, proj, proj, glo, proj, proj)


def _out_proj_kernel(a1_ref, a2_ref, w_ref, x_ref, g_ref, o_ref, acc_ref, y_ref, ssq_ref,
                     *, nk1, n_tiles):
    r = pl.program_id(0)
    kk = pl.program_id(1)
    nk = pl.num_programs(1)
    d = acc_ref.shape[1]
    te = o_ref.shape[0]

    def residual_chunk():
        rows = pl.ds(pl.multiple_of(kk * te, te), te)
        inv = lax.rsqrt(ssq_ref[rows, :] * (1.0 / d) + NORM_EPS)
        o_ref[...] = x_ref[...] + y_ref[rows, :].astype(jnp.float32) * inv * g_ref[...]

    def accumulate(last):
        a = jnp.where(kk < nk1, a1_ref[...], a2_ref[...])
        ssq = None
        for n0 in range(0, d, OUT_COL_CHUNK):
            cols = slice(n0, n0 + OUT_COL_CHUNK)
            part = jnp.dot(a, w_ref[:, cols], preferred_element_type=jnp.float32)
            y = jnp.where(kk == 0, part, acc_ref[:, cols] + part)
            if last:
                y_ref[:, cols] = y.astype(y_ref.dtype)
                s = jnp.sum(y * y, axis=-1, keepdims=True)
                ssq = s if ssq is None else ssq + s
            else:
                acc_ref[:, cols] = y
        if last:
            ssq_ref[...] = ssq

    first, drain = r == 0, r == n_tiles
    mid = jnp.logical_not(first | drain)
    last_slab = kk == nk - 1

    @pl.when(first & (kk == 0))
    def _():
        acc_ref[...] = jnp.zeros_like(acc_ref)

    @pl.when(first & jnp.logical_not(last_slab))
    def _():
        accumulate(False)

    @pl.when(first & last_slab)
    def _():
        accumulate(True)

    @pl.when(mid & jnp.logical_not(last_slab))
    def _():
        residual_chunk()
        accumulate(False)

    @pl.when(mid & last_slab)
    def _():
        residual_chunk()
        accumulate(True)

    @pl.when(drain)
    def _():
        residual_chunk()


def _out_proj(a1, a2, w, x2d, gain, tm=1024, tk=512):
    t, k1 = a1.shape
    k2 = a2.shape[1]
    d = w.shape[1]
    tm = min(tm, t)
    n_tiles = t // tm
    nk1, nk2 = k1 // tk, k2 // tk
    nk = nk1 + nk2
    te = tm // nk
    assert tm % nk == 0 and te % SUBLANES == 0

    def a_row(r):
        return jnp.minimum(r, n_tiles - 1)

    def residual_block(r, k):
        return (jnp.where(r == 0, 0, (r - 1) * nk + k), 0)

    return pl.pallas_call(
        functools.partial(_out_proj_kernel, nk1=nk1, n_tiles=n_tiles),
        out_shape=jax.ShapeDtypeStruct((t, d), jnp.float32),
        grid=(n_tiles + 1, nk),
        in_specs=[pl.BlockSpec((tm, tk), lambda r, k: (a_row(r), jnp.minimum(k, nk1 - 1))),
                  pl.BlockSpec((tm, tk), lambda r, k: (a_row(r), jnp.clip(k - nk1, 0, nk2 - 1))),
                  pl.BlockSpec((tk, d), lambda r, k: (jnp.where(r == n_tiles, nk - 1, k), 0)),
                  pl.BlockSpec((te, d), residual_block),
                  pl.BlockSpec((1, d), lambda r, k: (0, 0))],
        out_specs=pl.BlockSpec((te, d), residual_block),
        scratch_shapes=[pltpu.VMEM((tm, d), jnp.float32),
                        pltpu.VMEM((tm, d), jnp.bfloat16),
                        pltpu.VMEM((tm, 1), jnp.float32)],
        compiler_params=pltpu.CompilerParams(
            dimension_semantics=("arbitrary", "arbitrary"), vmem_limit_bytes=VMEM_LIMIT),
        name="out_proj",
    )(a1, a2, w, x2d, gain.reshape(1, d))


def kernel(x, w_in, attn_sinks, lb_logits, rnn_norm, w_out, pre_norm, post_norm):
    batch, seq, d_model = x.shape
    depth = w_in.shape[0]
    d_mix = w_out.shape[1]
    d_attn = d_mix // 2
    d_rnn = d_mix - d_attn
    x2d = x.reshape(batch * seq, d_model)
    tm = min(PROJ_TM, seq)
    for layer in range(depth):
        h = _prenorm(x2d, pre_norm[layer])
        proj, glo, worst, w_out_bf = _in_proj(h, w_in[layer], w_out[layer], lb_logits,
                                              d_attn, d_rnn, layer, tm)
        attn = _swa(proj, attn_sinks[layer], batch, seq, d_attn)
        rnn = _hgrn2(proj, glo, worst, rnn_norm[layer], batch, seq, d_attn, d_rnn, rows=tm)
        x2d = _out_proj(attn, rnn, w_out_bf, x2d, post_norm[layer])
    return x2d.reshape(batch, seq, d_model)
```

```python
import functools

import jax
import jax.numpy as jnp
import numpy as np
from jax import lax
from jax.experimental import pallas as pl
from jax.experimental.pallas import tpu as pltpu

ATTN_HEAD_DIM = 64
GQA_GROUP = 8
WINDOW = 128
RNN_HEAD_DIM = 128
NORM_EPS = 1e-6

LANES = 128
SUBLANES = 8
BF16_ROWS = 2 * SUBLANES
CHUNK = 128
N_LEVELS = 7
SAFE_LOG2 = 100.0
LOG2_E = 1.4426950408889634
VMEM_LIMIT = 56 * 1024 * 1024
SWA_BLOCKS = 4
SKEW_GAP = 3
PROJ_TM = 1024
PROJ_TN = 1024
GATE_ROW_CHUNKS = 8
OUT_COL_CHUNK = 1024

_NT = (((1,), (1,)), ((), ()))
_TN = (((0,), (0,)), ((), ()))


def _skewed(n_items, stages, gap=SKEW_GAP):
    carry = [None] * n_items
    for step in range(n_items + gap * (len(stages) - 1)):
        for j in reversed(range(len(stages))):
            i = step - gap * j
            if 0 <= i < n_items:
                carry[i] = stages[j](i, carry[i])


def _silu(x):
    h = 0.5 * x
    return h + h * jnp.tanh(h)


def _prenorm_kernel(x_ref, g_ref, o_ref):
    x = x_ref[...]
    ms = jnp.mean(x * x, axis=-1, keepdims=True)
    o_ref[...] = (x * lax.rsqrt(ms + NORM_EPS) * g_ref[...]).astype(o_ref.dtype)


def _prenorm(x2d, gain, rows=512):
    t, d = x2d.shape
    rows = min(rows, t)
    return pl.pallas_call(
        _prenorm_kernel,
        out_shape=jax.ShapeDtypeStruct((t, d), jnp.bfloat16),
        grid=(t // rows,),
        in_specs=[pl.BlockSpec((rows, d), lambda i: (i, 0)),
                  pl.BlockSpec((1, d), lambda i: (0, 0))],
        out_specs=pl.BlockSpec((rows, d), lambda i: (i, 0)),
        compiler_params=pltpu.CompilerParams(
            dimension_semantics=("parallel",), vmem_limit_bytes=VMEM_LIMIT),
        name="prenorm",
    )(x2d, gain.reshape(1, d))


def _in_proj_kernel(h_ref, wchunk_ref, w2_ref, lbl_ref, o_ref, glo_ref, worst_ref, w2o_ref, wbuf_ref,
                    *, silu_tiles, gate_tiles, layer):
    jo = pl.program_id(0)
    i = pl.program_id(1)
    rows = wchunk_ref.shape[0]
    chunk = pl.ds(pl.multiple_of(i * rows, rows), rows)
    tile = jo - 1

    def in_ranges(ranges):
        hits = [(tile >= lo) & (tile < hi) for lo, hi in ranges]
        return functools.reduce(jnp.logical_or, hits)

    is_silu = in_ranges(silu_tiles)
    is_gate = in_ranges(gate_tiles)

    def casts_and_dot(slot):
        wbuf_ref[slot, chunk, :] = wchunk_ref[...].astype(wbuf_ref.dtype)
        w2o_ref[...] = w2_ref[...].astype(w2o_ref.dtype)
        return jnp.dot(h_ref[...], wbuf_ref[1 - slot], preferred_element_type=jnp.float32)

    @pl.when(jo == 0)
    def _():
        wbuf_ref[0, chunk, :] = wchunk_ref[...].astype(wbuf_ref.dtype)
        w2o_ref[...] = w2_ref[...].astype(w2o_ref.dtype)

    @pl.when((jo > 0) & jnp.logical_not(is_silu | is_gate))
    def _():
        o_ref[...] = casts_and_dot(jo % 2).astype(o_ref.dtype)

    @pl.when(is_silu)
    def _():
        o_ref[...] = _silu(casts_and_dot(jo % 2)).astype(o_ref.dtype)

    @pl.when(is_gate)
    def _():
        lbl = lbl_ref[...]
        e = jnp.exp(lbl - jnp.max(lbl, axis=0, keepdims=True))
        lb = jnp.sum(e[:layer + 1], axis=0, keepdims=True) / jnp.sum(e, axis=0, keepdims=True)
        fa = 0.5 * (1.0 + lb)
        fb = 0.5 * (1.0 - lb)
        slot = jo % 2
        wbuf_ref[slot, chunk, :] = wchunk_ref[...].astype(wbuf_ref.dtype)
        w2o_ref[...] = w2_ref[...].astype(w2o_ref.dtype)
        rc = o_ref.shape[0] // GATE_ROW_CHUNKS
        half = CHUNK // 2
        worst = None
        for r in range(GATE_ROW_CHUNKS):
            rs = slice(r * rc, (r + 1) * rc)
            acc = jnp.dot(h_ref[rs, :], wbuf_ref[1 - slot], preferred_element_type=jnp.float32)
            g2 = jnp.log(fa + fb * jnp.tanh(0.5 * acc)) * LOG2_E
            g_hi = g2.astype(o_ref.dtype)
            o_ref[rs, :] = g_hi
            glo_ref[rs, :] = (g2 - g_hi.astype(jnp.float32)).astype(glo_ref.dtype)
            sums = jnp.sum(g2.reshape(rc // half, half, g2.shape[1]), axis=1)
            low = jnp.min(sums, axis=0, keepdims=True)
            worst = low if worst is None else jnp.minimum(worst, low)
        worst_ref[...] = jnp.broadcast_to(jnp.min(worst, axis=1, keepdims=True), worst_ref.shape)


def _tile_ranges(col_ranges, tn):
    assert all(lo % tn == 0 and hi % tn == 0 for lo, hi in col_ranges)
    return tuple((lo // tn, hi // tn) for lo, hi in col_ranges)


def _in_proj(h, w, w2, lb_logits, d_attn, d_rnn, layer, tm, tn=PROJ_TN, w2_col_blocks=16):
    t, d = h.shape
    n = w.shape[1]
    k2, d2 = w2.shape
    ni, nj = t // tm, n // tn
    assert d % ni == 0 and (d // ni) % BF16_ROWS == 0 and k2 % ni == 0 and nj + 1 >= w2_col_blocks
    w2_blk = (k2 // ni, d2 // w2_col_blocks)
    rnn0 = d_attn + d_attn // 4 + d_attn
    assert rnn0 + 4 * d_rnn == n
    silu_tiles = _tile_ranges([(rnn0 - d_attn, rnn0), (rnn0, rnn0 + d_rnn),
                               (rnn0 + 3 * d_rnn, n)], tn)
    gate_tiles = _tile_ranges([(rnn0 + d_rnn, rnn0 + 2 * d_rnn)], tn)
    g0, g1 = gate_tiles[0]

    def w2_index(jo, i):
        done = jo >= w2_col_blocks
        return (jnp.where(done, ni - 1, i), jnp.where(done, w2_col_blocks - 1, jo))

    def glo_index(jo, i):
        tile = jo - 1
        row = jnp.where(tile < g0, 0, jnp.where(tile >= g1, ni - 1, i))
        return (row, jnp.clip(tile - g0, 0, g1 - g0 - 1))

    return pl.pallas_call(
        functools.partial(_in_proj_kernel, silu_tiles=silu_tiles, gate_tiles=gate_tiles,
                          layer=layer),
        out_shape=(jax.ShapeDtypeStruct((t, n), jnp.bfloat16),
                   jax.ShapeDtypeStruct((t, d_rnn), jnp.bfloat16),
                   jax.ShapeDtypeStruct((ni * SUBLANES, (g1 - g0) * LANES), jnp.float32),
                   jax.ShapeDtypeStruct((k2, d2), jnp.bfloat16)),
        grid=(nj + 1, ni),
        in_specs=[pl.BlockSpec((tm, d), lambda jo, i: (jnp.where(jo == 0, 0, i), 0)),
                  pl.BlockSpec((d // ni, tn), lambda jo, i: (i, jnp.minimum(jo, nj - 1))),
                  pl.BlockSpec(w2_blk, w2_index),
                  pl.BlockSpec((lb_logits.shape[0], tn),
                               lambda jo, i: (0, jnp.clip(jo - 1 - g0, 0, g1 - g0 - 1)))],
        out_specs=(pl.BlockSpec((tm, tn),
                                lambda jo, i: (jnp.where(jo == 0, 0, i), jnp.maximum(jo - 1, 0))),
                   pl.BlockSpec((tm, tn), glo_index),
                   pl.BlockSpec((SUBLANES, LANES), glo_index),
                   pl.BlockSpec(w2_blk, w2_index)),
        scratch_shapes=[pltpu.VMEM((2, d, tn), jnp.bfloat16)],
        compiler_params=pltpu.CompilerParams(
            dimension_semantics=("arbitrary", "arbitrary"), vmem_limit_bytes=VMEM_LIMIT),
        name="in_proj",
    )(h, w, w2, lb_logits)


def _lane_halves(slab, head_in_high_half):
    lane = lax.broadcasted_iota(jnp.int32, slab.shape, 1)
    swapped = pltpu.roll(slab, ATTN_HEAD_DIM, axis=1)
    zero = jnp.zeros_like(slab)
    if head_in_high_half:
        lo, hi = swapped, slab
    else:
        lo, hi = slab, swapped
    return (jnp.where(lane < ATTN_HEAD_DIM, lo, zero), jnp.where(lane >= ATTN_HEAD_DIM, hi, zero))


def _swa_kernel(sink_ref, q_ref, kvp_ref, kvc_ref, g0_ref, g1_ref, g2_ref, g3_ref, o_ref,
                *, n_kv_heads, d_kv, blocks):
    n = pl.program_id(1)
    gate_refs = (g0_ref, g1_ref, g2_ref, g3_ref)
    w = WINDOW
    qi = lax.broadcasted_iota(jnp.int32, (w, w), 0)
    kj = lax.broadcasted_iota(jnp.int32, (w, w), 1)
    lower = kj <= qi
    first_bias = jnp.where(n > 0, 0.0, -jnp.inf)
    lane = lax.broadcasted_iota(jnp.int32, (w, LANES), 1)
    scale = ATTN_HEAD_DIM ** -0.5 * LOG2_E
    pairs_per_kv = GQA_GROUP // 2
    n_pairs = n_kv_heads * pairs_per_kv
    pairs_per_gate_ref = n_pairs // len(gate_refs)

    kv_cache = {}

    def kv_operands(j, h):
        if (j, h) not in kv_cache:
            c0 = (h // 2) * LANES
            high = (h % 2) == 1
            cur = slice(j * w, (j + 1) * w)

            def slab(col):
                prev = (kvp_ref[:, col:col + LANES] if j == 0
                        else kvc_ref[(j - 1) * w:j * w, col:col + LANES])
                return jnp.concatenate([prev, kvc_ref[cur, col:col + LANES]],
                                       axis=0).astype(jnp.float32)

            k_lo, k_hi = _lane_halves(slab(c0) * scale, high)
            v_lo, v_hi = _lane_halves(slab(d_kv + c0), high)
            kv_cache[(j, h)] = (jnp.concatenate([k_lo, k_hi], axis=0).astype(jnp.bfloat16),
                                jnp.concatenate([v_lo, v_hi], axis=0).astype(jnp.bfloat16))
        return kv_cache[(j, h)]

    def where(item):
        j, p = divmod(item, n_pairs)
        return j, p, slice(j * w, (j + 1) * w)

    def score_stage(item, _):
        j, p, rows = where(item)
        qp = q_ref[rows, p * LANES:(p + 1) * LANES]
        kcat = kv_operands(j, p // pairs_per_kv)[0]
        return lax.dot_general(qp, kcat, _NT, preferred_element_type=jnp.float32)

    def max_stage(item, s):
        j, p, _ = where(item)
        merged, mx = [], []
        for hh in range(2):
            prev = s[:, hh * 2 * w:hh * 2 * w + w]
            if j == 0:
                prev = prev + first_bias
            cur = s[:, hh * 2 * w + w:(hh + 1) * 2 * w]
            sh = jnp.where(lower, cur, prev)
            merged.append(sh)
            mx.append(jnp.maximum(jnp.max(sh, axis=-1, keepdims=True), sink_ref[2 * p + hh] * LOG2_E))
        return merged, mx

    def exp_stage(item, carry):
        _, p, _ = where(item)
        merged, mx = carry
        probs, denom = [], []
        for hh in range(2):
            e = jnp.exp2(merged[hh] - mx[hh])
            denom.append(jnp.sum(e, axis=-1, keepdims=True)
                         + jnp.exp2(sink_ref[2 * p + hh] * LOG2_E - mx[hh]))
            e = e.astype(jnp.bfloat16)
            zero = jnp.zeros_like(e)
            probs += [jnp.where(lower, zero, e), jnp.where(lower, e, zero)]
        return jnp.concatenate(probs, axis=1), denom

    def value_stage(item, carry):
        j, p, _ = where(item)
        pcat, denom = carry
        vcat = kv_operands(j, p // pairs_per_kv)[1]
        return jnp.dot(pcat, vcat, preferred_element_type=jnp.float32), denom

    def out_stage(item, carry):
        _, p, rows = where(item)
        o, denom = carry
        o = o * jnp.where(lane < ATTN_HEAD_DIM, 1.0 / denom[0], 1.0 / denom[1])
        gi, gp = divmod(p, pairs_per_gate_ref)
        gate = gate_refs[gi][rows, gp * LANES:(gp + 1) * LANES].astype(jnp.float32)
        o_ref[rows, p * LANES:(p + 1) * LANES] = (o * gate).astype(o_ref.dtype)

    _skewed(blocks * n_pairs, [score_stage, max_stage, exp_stage, value_stage, out_stage])


def _swa(proj, sinks, batch, seq, d_attn, blocks=SWA_BLOCKS):
    t = proj.shape[0]
    w = WINDOW
    nb = seq // w
    blocks = min(blocks, nb)
    assert nb % blocks == 0
    nbs = nb // blocks
    n_q = d_attn // ATTN_HEAD_DIM
    n_kv = n_q // GQA_GROUP
    d_kv = n_kv * ATTN_HEAD_DIM
    kvw = 2 * d_kv
    assert d_attn == 4 * kvw
    kv_blk = d_attn // kvw
    gate_blk0 = (d_attn + kvw) // kvw

    def rows(b, n):
        return b * nbs + n

    def prev_block(b, n):
        return b * nb + jnp.maximum(n * blocks - 1, 0)

    in_specs = [
        pl.BlockSpec(memory_space=pltpu.SMEM),
        pl.BlockSpec((blocks * w, d_attn), lambda b, n: (rows(b, n), 0)),
        pl.BlockSpec((w, kvw), lambda b, n: (prev_block(b, n), kv_blk)),
        pl.BlockSpec((blocks * w, kvw), lambda b, n: (rows(b, n), kv_blk)),
    ] + [
        pl.BlockSpec((blocks * w, kvw),
                     functools.partial(lambda b, n, j: (rows(b, n), gate_blk0 + j), j=j))
        for j in range(4)
    ]
    return pl.pallas_call(
        functools.partial(_swa_kernel, n_kv_heads=n_kv, d_kv=d_kv, blocks=blocks),
        out_shape=jax.ShapeDtypeStruct((t, d_attn), jnp.bfloat16),
        grid=(batch, nbs),
        in_specs=in_specs,
        out_specs=pl.BlockSpec((blocks * w, d_attn), lambda b, n: (rows(b, n), 0)),
        compiler_params=pltpu.CompilerParams(
            dimension_semantics=("parallel", "parallel"), vmem_limit_bytes=VMEM_LIMIT),
        name="swa",
    )(sinks, proj, proj, proj, proj, proj, proj, proj)


def _decay_tables():
    c = CHUNK
    n = np.zeros((N_LEVELS + 1, c, c), np.float32)
    r = np.arange(c)
    for l in range(1, N_LEVELS):
        half = 1 << l
        m = 2 * half
        for t in range(c):
            start = (t // m) * m
            mid = start + half - 1
            if t > mid:
                n[l - 1, t] = (r > mid) & (r <= t)
            else:
                n[l - 1, t] = (r > t) & (r <= mid)
    n[N_LEVELS - 1] = r[None, :] <= r[:, None]
    n[N_LEVELS] = r[None, :] > r[:, None]
    cumsum = np.concatenate([n[N_LEVELS - 1], n[N_LEVELS - 1]], axis=1)
    n = n.reshape((N_LEVELS + 1) * c, c)
    ncat = np.concatenate([n, n], axis=1)
    tt, ss = np.meshgrid(r, r, indexing="ij")
    x = tt ^ ss
    level = np.full((c, c), -1, np.int32)
    for l in range(N_LEVELS):
        level[(ss < tt) & (x >= (1 << l)) & (x < (2 << l))] = l
    return ncat, cumsum, level


def _interleave_rows(k, q, half):
    pieces = []
    for r0 in range(0, k.shape[0], half):
        src = q if (r0 // half) % 2 else k
        pieces.append(src[r0:r0 + half])
    return jnp.concatenate(pieces, axis=0)


def _hgrn2_kernel(ncat_ref, cum_ref, level_ref, gain_ref, worst_ref, q_ref, ghi_ref, glo_ref, v_ref,
                  gate_ref, o_ref, state_ref, *, heads, chunks):
    c = CHUNK
    pw = 2 * LANES
    n_pairs = heads // 2

    @pl.when(pl.program_id(2) == 0)
    def _():
        state_ref[...] = jnp.zeros_like(state_ref)

    mild = jnp.min(worst_ref[...]) > -SAFE_LOG2

    def key_gate(rows, cols):
        g2 = ghi_ref[rows, cols].astype(jnp.float32) + glo_ref[rows, cols].astype(jnp.float32)
        return 1.0 - jnp.exp2(g2)

    def head_matmuls(a_bf, qi, kl, v_bf, idx):
        intra = jnp.dot(a_bf, v_bf, preferred_element_type=jnp.float32)
        st = state_ref[idx]
        inter = lax.dot_general(qi, st.astype(jnp.bfloat16), _NT,
                                preferred_element_type=jnp.float32)
        upd = lax.dot_general(v_bf, kl, _TN, preferred_element_type=jnp.float32)
        return inter + intra, st, upd

    def head_tail(a_bf, qi, kl, egl, v_bf, idx):
        o, st, upd = head_matmuls(a_bf, qi, kl, v_bf, idx)
        state_ref[idx] = egl * st + upd
        return o

    def write_out(outs, rows, cols):
        normed = []
        for o in outs:
            ms = jnp.mean(o * o, axis=-1, keepdims=True)
            normed.append(o * lax.rsqrt(ms + NORM_EPS))
        o2 = jnp.concatenate(normed, axis=1) * gain_ref[:, cols]
        gate = gate_ref[rows, cols].astype(jnp.float32)
        o_ref[rows, cols] = (o2 * gate).astype(o_ref.dtype)

    @pl.when(mild)
    def _():
        ti = lax.broadcasted_iota(jnp.int32, (c, c), 0)
        si = lax.broadcasted_iota(jnp.int32, (c, c), 1)
        causal = si <= ti
        halves = [slice(hh * LANES, (hh + 1) * LANES) for hh in range(2)]

        def where(item):
            ci, pr = divmod(item, n_pairs)
            return slice(ci * c, (ci + 1) * c), slice(pr * pw, (pr + 1) * pw), pr

        def cumsum_stage(item, _):
            rows, cols, _ = where(item)
            gcat = jnp.concatenate([ghi_ref[rows, cols], glo_ref[rows, cols]], axis=0)
            return jnp.dot(cum_ref[...], gcat, preferred_element_type=jnp.float32)

        def scale_stage(item, gsum):
            rows, cols, _ = where(item)
            ref = gsum[c // 2 - 1:c // 2, :]
            last = gsum[c - 1:c, :]
            d = gsum - ref
            qp = q_ref[rows, cols].astype(jnp.float32) * jnp.exp2(d)
            kp = key_gate(rows, cols) * jnp.exp2(-d)
            qi = (qp * jnp.exp2(ref)).astype(jnp.bfloat16)
            kl = (kp * jnp.exp2(last - ref)).astype(jnp.bfloat16)
            qp = qp.astype(jnp.bfloat16)
            kp = kp.astype(jnp.bfloat16)
            a = [lax.dot_general(qp[:, hs], kp[:, hs], _NT, preferred_element_type=jnp.float32)
                 for hs in halves]
            return a, qi, kl, jnp.exp2(last)

        def mix_stage(item, carry):
            a, qi, kl, egl = carry
            rows, cols, pr = where(item)
            v_bf = v_ref[rows, cols]
            return [head_matmuls(jnp.where(causal, a[hh], 0.0).astype(jnp.bfloat16),
                                 qi[:, hs], kl[:, hs], v_bf[:, hs], 2 * pr + hh)
                    for hh, hs in enumerate(halves)], egl

        def out_stage(item, carry):
            heads_out, egl = carry
            rows, cols, pr = where(item)
            for hh, hs in enumerate(halves):
                _, st, upd = heads_out[hh]
                state_ref[2 * pr + hh] = egl[:, hs] * st + upd
            write_out([o for o, _, _ in heads_out], rows, cols)

        _skewed(chunks * n_pairs, [cumsum_stage, scale_stage, mix_stage, out_stage],
                gap=min(SKEW_GAP, n_pairs))

    @pl.when(jnp.logical_not(mild))
    def _():
        level = level_ref[...]
        trow = lax.broadcasted_iota(jnp.int32, (c, pw), 0)

        def chunk_body(ci, carry):
            rows = pl.ds(pl.multiple_of(ci * c, c), c)
            for pr in range(n_pairs):
                cols = slice(pr * pw, (pr + 1) * pw)
                q = q_ref[rows, cols].astype(jnp.float32)
                g_hi = ghi_ref[rows, cols]
                g_lo = glo_ref[rows, cols]
                f = jnp.exp2(g_hi.astype(jnp.float32) + g_lo.astype(jnp.float32))
                k = 1.0 - f
                gcat = jnp.concatenate([g_hi, g_lo], axis=0)
                ex = jnp.dot(ncat_ref[...], gcat, preferred_element_type=jnp.float32)

                z = [jnp.where((trow & 1) == 1, q * f, k).astype(jnp.bfloat16)]
                for l in range(1, N_LEVELS):
                    half = 1 << l
                    if half < SUBLANES:
                        qk = jnp.where((trow & half) != 0, q, k)
                    else:
                        qk = _interleave_rows(k, q, half)
                    z.append((qk * jnp.exp2(ex[(l - 1) * c:l * c])).astype(jnp.bfloat16))
                eg = jnp.exp2(ex[(N_LEVELS - 1) * c:N_LEVELS * c])
                erest = jnp.exp2(ex[N_LEVELS * c:(N_LEVELS + 1) * c])
                qi = (q * eg).astype(jnp.bfloat16)
                kl = (k * erest).astype(jnp.bfloat16)
                qk_diag = q * k
                v_bf = v_ref[rows, cols]
                outs = []
                for hh in range(2):
                    hs = slice(hh * LANES, (hh + 1) * LANES)
                    attn = jnp.zeros((c, c), jnp.float32)
                    for l in range(N_LEVELS):
                        zl = z[l][:, hs]
                        a = lax.dot_general(zl, zl, _NT, preferred_element_type=jnp.float32)
                        attn = jnp.where(level == l, a, attn)
                    diag = jnp.sum(qk_diag[:, hs], axis=-1, keepdims=True)
                    o = head_tail(attn.astype(jnp.bfloat16), qi[:, hs], kl[:, hs],
                                  eg[c - 1:c, hs], v_bf[:, hs], 2 * pr + hh)
                    outs.append(o + diag * v_bf[:, hs].astype(jnp.float32))
                write_out(outs, rows, cols)
            return carry

        lax.fori_loop(0, chunks, chunk_body, 0)


def _hgrn2(proj, glo, worst, gain, batch, seq, d_attn, d_rnn, rows, wb=PROJ_TN):
    t = proj.shape[0]
    assert seq % rows == 0 and rows % CHUNK == 0 and wb % (2 * RNN_HEAD_DIM) == 0
    chunks = rows // CHUNK
    heads = wb // RNN_HEAD_DIM
    nl = seq // rows
    n_hg = d_rnn // wb
    kvw = d_attn // 4
    base = d_attn + kvw + d_attn
    assert base % wb == 0
    ncat, cumsum, level = _decay_tables()

    def seg(i):
        blk0 = (base + i * d_rnn) // wb
        return pl.BlockSpec((rows, wb), lambda b, hg, l: (b * nl + l, blk0 + hg))

    const = lambda shape: pl.BlockSpec(shape, lambda b, hg, l: (0, 0))
    own = pl.BlockSpec((rows, wb), lambda b, hg, l: (b * nl + l, hg))
    return pl.pallas_call(
        functools.partial(_hgrn2_kernel, heads=heads, chunks=chunks),
        out_shape=jax.ShapeDtypeStruct((t, d_rnn), jnp.bfloat16),
        grid=(batch, n_hg, nl),
        in_specs=[
            const(ncat.shape), const(cumsum.shape), const(level.shape),
            pl.BlockSpec((1, wb), lambda b, hg, l: (0, hg)),
            pl.BlockSpec((SUBLANES, LANES), lambda b, hg, l: (b * nl + l, hg)),
            seg(0), seg(1), own, seg(2), seg(3),
        ],
        out_specs=own,
        scratch_shapes=[pltpu.VMEM((heads, RNN_HEAD_DIM, RNN_HEAD_DIM), jnp.float32)],
        compiler_params=pltpu.CompilerParams(
            dimension_semantics=("parallel", "parallel", "arbitrary"),
            vmem_limit_bytes=VMEM_LIMIT),
        name="hgrn2",
    )(jnp.asarray(ncat, jnp.bfloat16), jnp.asarray(cumsum, jnp.bfloat16), jnp.asarray(level),
      gain.reshape(1, d_rnn), worst, proj, proj, glo, proj, proj)


def _out_proj_kernel(a1_ref, a2_ref, w_ref, x_ref, g_ref, o_ref, acc_ref, y_ref, ssq_ref,
                     *, nk1, n_tiles):
    r = pl.program_id(0)
    kk = pl.program_id(1)
    nk = pl.num_programs(1)
    d = acc_ref.shape[1]
    te = o_ref.shape[0]

    def residual_chunk():
        rows = pl.ds(pl.multiple_of(kk * te, te), te)
        inv = lax.rsqrt(ssq_ref[rows, :] * (1.0 / d) + NORM_EPS)
        o_ref[...] = x_ref[...] + y_ref[rows, :].astype(jnp.float32) * inv * g_ref[...]

    def accumulate(last):
        a = jnp.where(kk < nk1, a1_ref[...], a2_ref[...])
        ssq = None
        for n0 in range(0, d, OUT_COL_CHUNK):
            cols = slice(n0, n0 + OUT_COL_CHUNK)
            part = jnp.dot(a, w_ref[:, cols], preferred_element_type=jnp.float32)
            y = jnp.where(kk == 0, part, acc_ref[:, cols] + part)
            if last:
                y_ref[:, cols] = y.astype(y_ref.dtype)
                s = jnp.sum(y * y, axis=-1, keepdims=True)
                ssq = s if ssq is None else ssq + s
            else:
                acc_ref[:, cols] = y
        if last:
            ssq_ref[...] = ssq

    first, drain = r == 0, r == n_tiles
    mid = jnp.logical_not(first | drain)
    last_slab = kk == nk - 1

    @pl.when(first & (kk == 0))
    def _():
        acc_ref[...] = jnp.zeros_like(acc_ref)

    @pl.when(first & jnp.logical_not(last_slab))
    def _():
        accumulate(False)

    @pl.when(first & last_slab)
    def _():
        accumulate(True)

    @pl.when(mid & jnp.logical_not(last_slab))
    def _():
        residual_chunk()
        accumulate(False)

    @pl.when(mid & last_slab)
    def _():
        residual_chunk()
        accumulate(True)

    @pl.when(drain)
    def _():
        residual_chunk()


def _out_proj(a1, a2, w, x2d, gain, tm=1024, tk=512):
    t, k1 = a1.shape
    k2 = a2.shape[1]
    d = w.shape[1]
    tm = min(tm, t)
    n_tiles = t // tm
    nk1, nk2 = k1 // tk, k2 // tk
    nk = nk1 + nk2
    te = tm // nk
    assert tm % nk == 0 and te % SUBLANES == 0

    def a_row(r):
        return jnp.minimum(r, n_tiles - 1)

    def residual_block(r, k):
        return (jnp.where(r == 0, 0, (r - 1) * nk + k), 0)

    return pl.pallas_call(
        functools.partial(_out_proj_kernel, nk1=nk1, n_tiles=n_tiles),
        out_shape=jax.ShapeDtypeStruct((t, d), jnp.float32),
        grid=(n_tiles + 1, nk),
        in_specs=[pl.BlockSpec((tm, tk), lambda r, k: (a_row(r), jnp.minimum(k, nk1 - 1))),
                  pl.BlockSpec((tm, tk), lambda r, k: (a_row(r), jnp.clip(k - nk1, 0, nk2 - 1))),
                  pl.BlockSpec((tk, d), lambda r, k: (jnp.where(r == n_tiles, nk - 1, k), 0)),
                  pl.BlockSpec((te, d), residual_block),
                  pl.BlockSpec((1, d), lambda r, k: (0, 0))],
        out_specs=pl.BlockSpec((te, d), residual_block),
        scratch_shapes=[pltpu.VMEM((tm, d), jnp.float32),
                        pltpu.VMEM((tm, d), jnp.bfloat16),
                        pltpu.VMEM((tm, 1), jnp.float32)],
        compiler_params=pltpu.CompilerParams(
            dimension_semantics=("arbitrary", "arbitrary"), vmem_limit_bytes=VMEM_LIMIT),
        name="out_proj",
    )(a1, a2, w, x2d, gain.reshape(1, d))


def kernel(x, w_in, attn_sinks, lb_logits, rnn_norm, w_out, pre_norm, post_norm):
    batch, seq, d_model = x.shape
    depth = w_in.shape[0]
    d_mix = w_out.shape[1]
    d_attn = d_mix // 2
    d_rnn = d_mix - d_attn
    x2d = x.reshape(batch * seq, d_model)
    tm = min(PROJ_TM, seq)
    for layer in range(depth):
        h = _prenorm(x2d, pre_norm[layer])
        proj, glo, worst, w_out_bf = _in_proj(h, w_in[layer], w_out[layer], lb_logits,
                                              d_attn, d_rnn, layer, tm)
        attn = _swa(proj, attn_sinks[layer], batch, seq, d_attn)
        rnn = _hgrn2(proj, glo, worst, rnn_norm[layer], batch, seq, d_attn, d_rnn, rows=tm)
        x2d = _out_proj(attn, rnn, w_out_bf, x2d, post_norm[layer])
    return x2d.reshape(batch, seq, d_model)
```

```python
import functools

import jax
import jax.numpy as jnp
import numpy as np
from jax import lax
from jax.experimental import pallas as pl
from jax.experimental.pallas import tpu as pltpu

ATTN_HEAD_DIM = 64
GQA_GROUP = 8
WINDOW = 128
RNN_HEAD_DIM = 128
NORM_EPS = 1e-6

LANES = 128
SUBLANES = 8
BF16_ROWS = 2 * SUBLANES
CHUNK = 128
N_LEVELS = 7
SAFE_LOG2 = 100.0
LOG2_E = 1.4426950408889634
VMEM_LIMIT = 56 * 1024 * 1024
SWA_BLOCKS = 4
SKEW_GAP = 3
PROJ_TM = 1024
PROJ_TN = 1024
GATE_ROW_CHUNKS = 8
W2_COL_BLOCKS = 16
PRENORM_ROWS = 512
OUT_TM = 1024
OUT_TK = 512
OUT_COL_CHUNK = 1024

_NT = (((1,), (1,)), ((), ()))
_TN = (((0,), (0,)), ((), ()))


def _skewed(n_items, stages, gap=SKEW_GAP):
    carry = [None] * n_items
    for step in range(n_items + gap * (len(stages) - 1)):
        for j in reversed(range(len(stages))):
            i = step - gap * j
            if 0 <= i < n_items:
                carry[i] = stages[j](i, carry[i])


def _silu(x):
    h = 0.5 * x
    return h + h * jnp.tanh(h)


def _prenorm_kernel(x_ref, g_ref, o_ref):
    x = x_ref[...]
    ms = jnp.mean(x * x, axis=-1, keepdims=True)
    o_ref[...] = (x * lax.rsqrt(ms + NORM_EPS) * g_ref[...]).astype(o_ref.dtype)


def _prenorm(x2d, gain, rows=PRENORM_ROWS):
    t, d = x2d.shape
    rows = min(rows, t)
    return pl.pallas_call(
        _prenorm_kernel,
        out_shape=jax.ShapeDtypeStruct((t, d), jnp.bfloat16),
        grid=(t // rows,),
        in_specs=[pl.BlockSpec((rows, d), lambda i: (i, 0)),
                  pl.BlockSpec((1, d), lambda i: (0, 0))],
        out_specs=pl.BlockSpec((rows, d), lambda i: (i, 0)),
        compiler_params=pltpu.CompilerParams(
            dimension_semantics=("parallel",), vmem_limit_bytes=VMEM_LIMIT),
        name="prenorm",
    )(x2d, gain.reshape(1, d))


def _in_proj_kernel(h_ref, wchunk_ref, w2_ref, lbl_ref, o_ref, glo_ref, worst_ref, w2o_ref, wbuf_ref,
                    *, silu_tiles, gate_tiles, layer):
    jo = pl.program_id(0)
    i = pl.program_id(1)
    rows = wchunk_ref.shape[0]
    chunk = pl.ds(pl.multiple_of(i * rows, rows), rows)
    tile = jo - 1

    def in_ranges(ranges):
        hits = [(tile >= lo) & (tile < hi) for lo, hi in ranges]
        return functools.reduce(jnp.logical_or, hits)

    is_silu = in_ranges(silu_tiles)
    is_gate = in_ranges(gate_tiles)

    def casts_and_dot(slot):
        wbuf_ref[slot, chunk, :] = wchunk_ref[...].astype(wbuf_ref.dtype)
        w2o_ref[...] = w2_ref[...].astype(w2o_ref.dtype)
        return jnp.dot(h_ref[...], wbuf_ref[1 - slot], preferred_element_type=jnp.float32)

    @pl.when(jo == 0)
    def _():
        wbuf_ref[0, chunk, :] = wchunk_ref[...].astype(wbuf_ref.dtype)
        w2o_ref[...] = w2_ref[...].astype(w2o_ref.dtype)

    @pl.when((jo > 0) & jnp.logical_not(is_silu | is_gate))
    def _():
        o_ref[...] = casts_and_dot(jo % 2).astype(o_ref.dtype)

    @pl.when(is_silu)
    def _():
        o_ref[...] = _silu(casts_and_dot(jo % 2)).astype(o_ref.dtype)

    @pl.when(is_gate)
    def _():
        lbl = lbl_ref[...]
        e = jnp.exp(lbl - jnp.max(lbl, axis=0, keepdims=True))
        lb = jnp.sum(e[:layer + 1], axis=0, keepdims=True) / jnp.sum(e, axis=0, keepdims=True)
        fa = 0.5 * (1.0 + lb)
        fb = 0.5 * (1.0 - lb)
        slot = jo % 2
        wbuf_ref[slot, chunk, :] = wchunk_ref[...].astype(wbuf_ref.dtype)
        w2o_ref[...] = w2_ref[...].astype(w2o_ref.dtype)
        rc = o_ref.shape[0] // GATE_ROW_CHUNKS
        half = CHUNK // 2
        worst = None
        for r in range(GATE_ROW_CHUNKS):
            rs = slice(r * rc, (r + 1) * rc)
            acc = jnp.dot(h_ref[rs, :], wbuf_ref[1 - slot], preferred_element_type=jnp.float32)
            g2 = jnp.log(fa + fb * jnp.tanh(0.5 * acc)) * LOG2_E
            g_hi = g2.astype(o_ref.dtype)
            o_ref[rs, :] = g_hi
            glo_ref[rs, :] = (g2 - g_hi.astype(jnp.float32)).astype(glo_ref.dtype)
            sums = jnp.sum(g2.reshape(rc // half, half, g2.shape[1]), axis=1)
            low = jnp.min(sums, axis=0, keepdims=True)
            worst = low if worst is None else jnp.minimum(worst, low)
        worst_ref[...] = jnp.broadcast_to(jnp.min(worst, axis=1, keepdims=True), worst_ref.shape)


def _tile_ranges(col_ranges, tn):
    assert all(lo % tn == 0 and hi % tn == 0 for lo, hi in col_ranges)
    return tuple((lo // tn, hi // tn) for lo, hi in col_ranges)


def _in_proj(h, w, w2, lb_logits, d_attn, d_rnn, layer, tm, tn=PROJ_TN,
             w2_col_blocks=W2_COL_BLOCKS):
    t, d = h.shape
    n = w.shape[1]
    k2, d2 = w2.shape
    ni, nj = t // tm, n // tn
    assert d % ni == 0 and (d // ni) % BF16_ROWS == 0 and k2 % ni == 0 and nj + 1 >= w2_col_blocks
    w2_blk = (k2 // ni, d2 // w2_col_blocks)
    rnn0 = d_attn + d_attn // 4 + d_attn
    assert rnn0 + 4 * d_rnn == n
    silu_tiles = _tile_ranges([(rnn0 - d_attn, rnn0), (rnn0, rnn0 + d_rnn),
                               (rnn0 + 3 * d_rnn, n)], tn)
    gate_tiles = _tile_ranges([(rnn0 + d_rnn, rnn0 + 2 * d_rnn)], tn)
    g0, g1 = gate_tiles[0]

    def w2_index(jo, i):
        done = jo >= w2_col_blocks
        return (jnp.where(done, ni - 1, i), jnp.where(done, w2_col_blocks - 1, jo))

    def glo_index(jo, i):
        tile = jo - 1
        row = jnp.where(tile < g0, 0, jnp.where(tile >= g1, ni - 1, i))
        return (row, jnp.clip(tile - g0, 0, g1 - g0 - 1))

    return pl.pallas_call(
        functools.partial(_in_proj_kernel, silu_tiles=silu_tiles, gate_tiles=gate_tiles,
                          layer=layer),
        out_shape=(jax.ShapeDtypeStruct((t, n), jnp.bfloat16),
                   jax.ShapeDtypeStruct((t, d_rnn), jnp.bfloat16),
                   jax.ShapeDtypeStruct((ni * SUBLANES, (g1 - g0) * LANES), jnp.float32),
                   jax.ShapeDtypeStruct((k2, d2), jnp.bfloat16)),
        grid=(nj + 1, ni),
        in_specs=[pl.BlockSpec((tm, d), lambda jo, i: (jnp.where(jo == 0, 0, i), 0)),
                  pl.BlockSpec((d // ni, tn), lambda jo, i: (i, jnp.minimum(jo, nj - 1))),
                  pl.BlockSpec(w2_blk, w2_index),
                  pl.BlockSpec((lb_logits.shape[0], tn),
                               lambda jo, i: (0, jnp.clip(jo - 1 - g0, 0, g1 - g0 - 1)))],
        out_specs=(pl.BlockSpec((tm, tn),
                                lambda jo, i: (jnp.where(jo == 0, 0, i), jnp.maximum(jo - 1, 0))),
                   pl.BlockSpec((tm, tn), glo_index),
                   pl.BlockSpec((SUBLANES, LANES), glo_index),
                   pl.BlockSpec(w2_blk, w2_index)),
        scratch_shapes=[pltpu.VMEM((2, d, tn), jnp.bfloat16)],
        compiler_params=pltpu.CompilerParams(
            dimension_semantics=("arbitrary", "arbitrary"), vmem_limit_bytes=VMEM_LIMIT),
        name="in_proj",
    )(h, w, w2, lb_logits)


def _lane_halves(slab, head_in_high_half):
    lane = lax.broadcasted_iota(jnp.int32, slab.shape, 1)
    swapped = pltpu.roll(slab, ATTN_HEAD_DIM, axis=1)
    zero = jnp.zeros_like(slab)
    if head_in_high_half:
        lo, hi = swapped, slab
    else:
        lo, hi = slab, swapped
    return (jnp.where(lane < ATTN_HEAD_DIM, lo, zero), jnp.where(lane >= ATTN_HEAD_DIM, hi, zero))


def _swa_kernel(sink_ref, q_ref, kvp_ref, kvc_ref, g0_ref, g1_ref, g2_ref, g3_ref, o_ref,
                *, n_kv_heads, d_kv, blocks):
    n = pl.program_id(1)
    gate_refs = (g0_ref, g1_ref, g2_ref, g3_ref)
    w = WINDOW
    qi = lax.broadcasted_iota(jnp.int32, (w, w), 0)
    kj = lax.broadcasted_iota(jnp.int32, (w, w), 1)
    lower = kj <= qi
    first_bias = jnp.where(n > 0, 0.0, -jnp.inf)
    lane = lax.broadcasted_iota(jnp.int32, (w, LANES), 1)
    scale = ATTN_HEAD_DIM ** -0.5 * LOG2_E
    pairs_per_kv = GQA_GROUP // 2
    n_pairs = n_kv_heads * pairs_per_kv
    pairs_per_gate_ref = n_pairs // len(gate_refs)

    kv_cache = {}

    def kv_operands(j, h):
        if (j, h) not in kv_cache:
            c0 = (h // 2) * LANES
            high = (h % 2) == 1
            cur = slice(j * w, (j + 1) * w)

            def slab(col):
                prev = (kvp_ref[:, col:col + LANES] if j == 0
                        else kvc_ref[(j - 1) * w:j * w, col:col + LANES])
                return jnp.concatenate([prev, kvc_ref[cur, col:col + LANES]],
                                       axis=0).astype(jnp.float32)

            k_lo, k_hi = _lane_halves(slab(c0) * scale, high)
            v_lo, v_hi = _lane_halves(slab(d_kv + c0), high)
            kv_cache[(j, h)] = (jnp.concatenate([k_lo, k_hi], axis=0).astype(jnp.bfloat16),
                                jnp.concatenate([v_lo, v_hi], axis=0).astype(jnp.bfloat16))
        return kv_cache[(j, h)]

    def where(item):
        j, p = divmod(item, n_pairs)
        return j, p, slice(j * w, (j + 1) * w)

    def score_stage(item, _):
        j, p, rows = where(item)
        qp = q_ref[rows, p * LANES:(p + 1) * LANES]
        kcat = kv_operands(j, p // pairs_per_kv)[0]
        return lax.dot_general(qp, kcat, _NT, preferred_element_type=jnp.float32)

    def max_stage(item, s):
        j, p, _ = where(item)
        merged, mx = [], []
        for hh in range(2):
            prev = s[:, hh * 2 * w:hh * 2 * w + w]
            if j == 0:
                prev = prev + first_bias
            cur = s[:, hh * 2 * w + w:(hh + 1) * 2 * w]
            sh = jnp.where(lower, cur, prev)
            merged.append(sh)
            mx.append(jnp.maximum(jnp.max(sh, axis=-1, keepdims=True), sink_ref[2 * p + hh] * LOG2_E))
        return merged, mx

    def exp_stage(item, carry):
        _, p, _ = where(item)
        merged, mx = carry
        probs, denom = [], []
        for hh in range(2):
            e = jnp.exp2(merged[hh] - mx[hh])
            denom.append(jnp.sum(e, axis=-1, keepdims=True)
                         + jnp.exp2(sink_ref[2 * p + hh] * LOG2_E - mx[hh]))
            e = e.astype(jnp.bfloat16)
            zero = jnp.zeros_like(e)
            probs += [jnp.where(lower, zero, e), jnp.where(lower, e, zero)]
        return jnp.concatenate(probs, axis=1), denom

    def value_stage(item, carry):
        j, p, _ = where(item)
        pcat, denom = carry
        vcat = kv_operands(j, p // pairs_per_kv)[1]
        return jnp.dot(pcat, vcat, preferred_element_type=jnp.float32), denom

    def out_stage(item, carry):
        _, p, rows = where(item)
        o, denom = carry
        o = o * jnp.where(lane < ATTN_HEAD_DIM, 1.0 / denom[0], 1.0 / denom[1])
        gi, gp = divmod(p, pairs_per_gate_ref)
        gate = gate_refs[gi][rows, gp * LANES:(gp + 1) * LANES].astype(jnp.float32)
        o_ref[rows, p * LANES:(p + 1) * LANES] = (o * gate).astype(o_ref.dtype)

    _skewed(blocks * n_pairs, [score_stage, max_stage, exp_stage, value_stage, out_stage])


def _swa(proj, sinks, batch, seq, d_attn, blocks=SWA_BLOCKS):
    t = proj.shape[0]
    w = WINDOW
    nb = seq // w
    blocks = min(blocks, nb)
    assert nb % blocks == 0
    nbs = nb // blocks
    n_q = d_attn // ATTN_HEAD_DIM
    n_kv = n_q // GQA_GROUP
    d_kv = n_kv * ATTN_HEAD_DIM
    kvw = 2 * d_kv
    assert d_attn == 4 * kvw
    kv_blk = d_attn // kvw
    gate_blk0 = (d_attn + kvw) // kvw

    def rows(b, n):
        return b * nbs + n

    def prev_block(b, n):
        return b * nb + jnp.maximum(n * blocks - 1, 0)

    in_specs = [
        pl.BlockSpec(memory_space=pltpu.SMEM),
        pl.BlockSpec((blocks * w, d_attn), lambda b, n: (rows(b, n), 0)),
        pl.BlockSpec((w, kvw), lambda b, n: (prev_block(b, n), kv_blk)),
        pl.BlockSpec((blocks * w, kvw), lambda b, n: (rows(b, n), kv_blk)),
    ] + [
        pl.BlockSpec((blocks * w, kvw),
                     functools.partial(lambda b, n, j: (rows(b, n), gate_blk0 + j), j=j))
        for j in range(4)
    ]
    return pl.pallas_call(
        functools.partial(_swa_kernel, n_kv_heads=n_kv, d_kv=d_kv, blocks=blocks),
        out_shape=jax.ShapeDtypeStruct((t, d_attn), jnp.bfloat16),
        grid=(batch, nbs),
        in_specs=in_specs,
        out_specs=pl.BlockSpec((blocks * w, d_attn), lambda b, n: (rows(b, n), 0)),
        compiler_params=pltpu.CompilerParams(
            dimension_semantics=("parallel", "parallel"), vmem_limit_bytes=VMEM_LIMIT),
        name="swa",
    )(sinks, proj, proj, proj, proj, proj, proj, proj)


def _decay_tables():
    c = CHUNK
    n = np.zeros((N_LEVELS + 1, c, c), np.float32)
    r = np.arange(c)
    for l in range(1, N_LEVELS):
        half = 1 << l
        m = 2 * half
        for t in range(c):
            start = (t // m) * m
            mid = start + half - 1
            if t > mid:
                n[l - 1, t] = (r > mid) & (r <= t)
            else:
                n[l - 1, t] = (r > t) & (r <= mid)
    n[N_LEVELS - 1] = r[None, :] <= r[:, None]
    n[N_LEVELS] = r[None, :] > r[:, None]
    cumsum = np.concatenate([n[N_LEVELS - 1], n[N_LEVELS - 1]], axis=1)
    n = n.reshape((N_LEVELS + 1) * c, c)
    ncat = np.concatenate([n, n], axis=1)
    tt, ss = np.meshgrid(r, r, indexing="ij")
    x = tt ^ ss
    level = np.full((c, c), -1, np.int32)
    for l in range(N_LEVELS):
        level[(ss < tt) & (x >= (1 << l)) & (x < (2 << l))] = l
    return ncat, cumsum, level


def _interleave_rows(k, q, half):
    pieces = []
    for r0 in range(0, k.shape[0], half):
        src = q if (r0 // half) % 2 else k
        pieces.append(src[r0:r0 + half])
    return jnp.concatenate(pieces, axis=0)


def _hgrn2_kernel(ncat_ref, cum_ref, level_ref, gain_ref, worst_ref, q_ref, ghi_ref, glo_ref, v_ref,
                  gate_ref, o_ref, state_ref, *, heads, chunks):
    c = CHUNK
    pw = 2 * LANES
    n_pairs = heads // 2

    @pl.when(pl.program_id(2) == 0)
    def _():
        state_ref[...] = jnp.zeros_like(state_ref)

    mild = jnp.min(worst_ref[...]) > -SAFE_LOG2

    def key_gate(rows, cols):
        g2 = ghi_ref[rows, cols].astype(jnp.float32) + glo_ref[rows, cols].astype(jnp.float32)
        return 1.0 - jnp.exp2(g2)

    def head_matmuls(a_bf, qi, kl, v_bf, idx):
        intra = jnp.dot(a_bf, v_bf, preferred_element_type=jnp.float32)
        st = state_ref[idx]
        inter = lax.dot_general(qi, st.astype(jnp.bfloat16), _NT,
                                preferred_element_type=jnp.float32)
        upd = lax.dot_general(v_bf, kl, _TN, preferred_element_type=jnp.float32)
        return inter + intra, st, upd

    def head_tail(a_bf, qi, kl, egl, v_bf, idx):
        o, st, upd = head_matmuls(a_bf, qi, kl, v_bf, idx)
        state_ref[idx] = egl * st + upd
        return o

    def write_out(outs, rows, cols):
        normed = []
        for o in outs:
            ms = jnp.mean(o * o, axis=-1, keepdims=True)
            normed.append(o * lax.rsqrt(ms + NORM_EPS))
        o2 = jnp.concatenate(normed, axis=1) * gain_ref[:, cols]
        gate = gate_ref[rows, cols].astype(jnp.float32)
        o_ref[rows, cols] = (o2 * gate).astype(o_ref.dtype)

    @pl.when(mild)
    def _():
        ti = lax.broadcasted_iota(jnp.int32, (c, c), 0)
        si = lax.broadcasted_iota(jnp.int32, (c, c), 1)
        causal = si <= ti
        halves = [slice(hh * LANES, (hh + 1) * LANES) for hh in range(2)]

        def where(item):
            ci, pr = divmod(item, n_pairs)
            return slice(ci * c, (ci + 1) * c), slice(pr * pw, (pr + 1) * pw), pr

        def cumsum_stage(item, _):
            rows, cols, _ = where(item)
            gcat = jnp.concatenate([ghi_ref[rows, cols], glo_ref[rows, cols]], axis=0)
            return jnp.dot(cum_ref[...], gcat, preferred_element_type=jnp.float32)

        def scale_stage(item, gsum):
            rows, cols, _ = where(item)
            ref = gsum[c // 2 - 1:c // 2, :]
            last = gsum[c - 1:c, :]
            d = gsum - ref
            qp = q_ref[rows, cols].astype(jnp.float32) * jnp.exp2(d)
            kp = key_gate(rows, cols) * jnp.exp2(-d)
            qi = (qp * jnp.exp2(ref)).astype(jnp.bfloat16)
            kl = (kp * jnp.exp2(last - ref)).astype(jnp.bfloat16)
            qp = qp.astype(jnp.bfloat16)
            kp = kp.astype(jnp.bfloat16)
            a = [lax.dot_general(qp[:, hs], kp[:, hs], _NT, preferred_element_type=jnp.float32)
                 for hs in halves]
            return a, qi, kl, jnp.exp2(last)

        def mix_stage(item, carry):
            a, qi, kl, egl = carry
            rows, cols, pr = where(item)
            v_bf = v_ref[rows, cols]
            return [head_matmuls(jnp.where(causal, a[hh], 0.0).astype(jnp.bfloat16),
                                 qi[:, hs], kl[:, hs], v_bf[:, hs], 2 * pr + hh)
                    for hh, hs in enumerate(halves)], egl

        def out_stage(item, carry):
            heads_out, egl = carry
            rows, cols, pr = where(item)
            for hh, hs in enumerate(halves):
                _, st, upd = heads_out[hh]
                state_ref[2 * pr + hh] = egl[:, hs] * st + upd
            write_out([o for o, _, _ in heads_out], rows, cols)

        _skewed(chunks * n_pairs, [cumsum_stage, scale_stage, mix_stage, out_stage],
                gap=min(SKEW_GAP, n_pairs))

    @pl.when(jnp.logical_not(mild))
    def _():
        level = level_ref[...]
        trow = lax.broadcasted_iota(jnp.int32, (c, pw), 0)

        def chunk_body(ci, carry):
            rows = pl.ds(pl.multiple_of(ci * c, c), c)
            for pr in range(n_pairs):
                cols = slice(pr * pw, (pr + 1) * pw)
                q = q_ref[rows, cols].astype(jnp.float32)
                g_hi = ghi_ref[rows, cols]
                g_lo = glo_ref[rows, cols]
                f = jnp.exp2(g_hi.astype(jnp.float32) + g_lo.astype(jnp.float32))
                k = 1.0 - f
                gcat = jnp.concatenate([g_hi, g_lo], axis=0)
                ex = jnp.dot(ncat_ref[...], gcat, preferred_element_type=jnp.float32)

                z = [jnp.where((trow & 1) == 1, q * f, k).astype(jnp.bfloat16)]
                for l in range(1, N_LEVELS):
                    half = 1 << l
                    if half < SUBLANES:
                        qk = jnp.where((trow & half) != 0, q, k)
                    else:
                        qk = _interleave_rows(k, q, half)
                    z.append((qk * jnp.exp2(ex[(l - 1) * c:l * c])).astype(jnp.bfloat16))
                eg = jnp.exp2(ex[(N_LEVELS - 1) * c:N_LEVELS * c])
                erest = jnp.exp2(ex[N_LEVELS * c:(N_LEVELS + 1) * c])
                qi = (q * eg).astype(jnp.bfloat16)
                kl = (k * erest).astype(jnp.bfloat16)
                qk_diag = q * k
                v_bf = v_ref[rows, cols]
                outs = []
                for hh in range(2):
                    hs = slice(hh * LANES, (hh + 1) * LANES)
                    attn = jnp.zeros((c, c), jnp.float32)
                    for l in range(N_LEVELS):
                        zl = z[l][:, hs]
                        a = lax.dot_general(zl, zl, _NT, preferred_element_type=jnp.float32)
                        attn = jnp.where(level == l, a, attn)
                    diag = jnp.sum(qk_diag[:, hs], axis=-1, keepdims=True)
                    o = head_tail(attn.astype(jnp.bfloat16), qi[:, hs], kl[:, hs],
                                  eg[c - 1:c, hs], v_bf[:, hs], 2 * pr + hh)
                    outs.append(o + diag * v_bf[:, hs].astype(jnp.float32))
                write_out(outs, rows, cols)
            return carry

        lax.fori_loop(0, chunks, chunk_body, 0)


def _hgrn2(proj, glo, worst, gain, batch, seq, d_attn, d_rnn, rows, wb=PROJ_TN):
    t = proj.shape[0]
    assert seq % rows == 0 and rows % CHUNK == 0 and wb % (2 * RNN_HEAD_DIM) == 0
    chunks = rows // CHUNK
    heads = wb // RNN_HEAD_DIM
    nl = seq // rows
    n_hg = d_rnn // wb
    kvw = d_attn // 4
    base = d_attn + kvw + d_attn
    assert base % wb == 0
    ncat, cumsum, level = _decay_tables()

    def seg(i):
        blk0 = (base + i * d_rnn) // wb
        return pl.BlockSpec((rows, wb), lambda b, hg, l: (b * nl + l, blk0 + hg))

    const = lambda shape: pl.BlockSpec(shape, lambda b, hg, l: (0, 0))
    own = pl.BlockSpec((rows, wb), lambda b, hg, l: (b * nl + l, hg))
    return pl.pallas_call(
        functools.partial(_hgrn2_kernel, heads=heads, chunks=chunks),
        out_shape=jax.ShapeDtypeStruct((t, d_rnn), jnp.bfloat16),
        grid=(batch, n_hg, nl),
        in_specs=[
            const(ncat.shape), const(cumsum.shape), const(level.shape),
            pl.BlockSpec((1, wb), lambda b, hg, l: (0, hg)),
            pl.BlockSpec((SUBLANES, LANES), lambda b, hg, l: (b * nl + l, hg)),
            seg(0), seg(1), own, seg(2), seg(3),
        ],
        out_specs=own,
        scratch_shapes=[pltpu.VMEM((heads, RNN_HEAD_DIM, RNN_HEAD_DIM), jnp.float32)],
        compiler_params=pltpu.CompilerParams(
            dimension_semantics=("parallel", "parallel", "arbitrary"),
            vmem_limit_bytes=VMEM_LIMIT),
        name="hgrn2",
    )(jnp.asarray(ncat, jnp.bfloat16), jnp.asarray(cumsum, jnp.bfloat16), jnp.asarray(level),
      gain.reshape(1, d_rnn), worst, proj, proj, glo, proj, proj)


def _out_proj_kernel(a1_ref, a2_ref, w_ref, x_ref, g_ref, o_ref, acc_ref, y_ref, ssq_ref,
                     *, nk1, n_tiles):
    r = pl.program_id(0)
    kk = pl.program_id(1)
    nk = pl.num_programs(1)
    d = acc_ref.shape[1]
    te = o_ref.shape[0]

    def residual_chunk():
        rows = pl.ds(pl.multiple_of(kk * te, te), te)
        inv = lax.rsqrt(ssq_ref[rows, :] * (1.0 / d) + NORM_EPS)
        o_ref[...] = x_ref[...] + y_ref[rows, :].astype(jnp.float32) * inv * g_ref[...]

    def accumulate(last):
        a = jnp.where(kk < nk1, a1_ref[...], a2_ref[...])
        ssq = None
        for n0 in range(0, d, OUT_COL_CHUNK):
            cols = slice(n0, n0 + OUT_COL_CHUNK)
            part = jnp.dot(a, w_ref[:, cols], preferred_element_type=jnp.float32)
            y = jnp.where(kk == 0, part, acc_ref[:, cols] + part)
            if last:
                y_ref[:, cols] = y.astype(y_ref.dtype)
                s = jnp.sum(y * y, axis=-1, keepdims=True)
                ssq = s if ssq is None else ssq + s
            else:
                acc_ref[:, cols] = y
        if last:
            ssq_ref[...] = ssq

    first, drain = r == 0, r == n_tiles
    mid = jnp.logical_not(first | drain)
    last_slab = kk == nk - 1

    @pl.when(first & (kk == 0))
    def _():
        acc_ref[...] = jnp.zeros_like(acc_ref)

    @pl.when(first & jnp.logical_not(last_slab))
    def _():
        accumulate(False)

    @pl.when(first & last_slab)
    def _():
        accumulate(True)

    @pl.when(mid & jnp.logical_not(last_slab))
    def _():
        residual_chunk()
        accumulate(False)

    @pl.when(mid & last_slab)
    def _():
        residual_chunk()
        accumulate(True)

    @pl.when(drain)
    def _():
        residual_chunk()


def _out_proj(a1, a2, w, x2d, gain, tm=OUT_TM, tk=OUT_TK):
    t, k1 = a1.shape
    k2 = a2.shape[1]
    d = w.shape[1]
    tm = min(tm, t)
    n_tiles = t // tm
    nk1, nk2 = k1 // tk, k2 // tk
    nk = nk1 + nk2
    te = tm // nk
    assert tm % nk == 0 and te % SUBLANES == 0

    def a_row(r):
        return jnp.minimum(r, n_tiles - 1)

    def residual_block(r, k):
        return (jnp.where(r == 0, 0, (r - 1) * nk + k), 0)

    return pl.pallas_call(
        functools.partial(_out_proj_kernel, nk1=nk1, n_tiles=n_tiles),
        out_shape=jax.ShapeDtypeStruct((t, d), jnp.float32),
        grid=(n_tiles + 1, nk),
        in_specs=[pl.BlockSpec((tm, tk), lambda r, k: (a_row(r), jnp.minimum(k, nk1 - 1))),
                  pl.BlockSpec((tm, tk), lambda r, k: (a_row(r), jnp.clip(k - nk1, 0, nk2 - 1))),
                  pl.BlockSpec((tk, d), lambda r, k: (jnp.where(r == n_tiles, nk - 1, k), 0)),
                  pl.BlockSpec((te, d), residual_block),
                  pl.BlockSpec((1, d), lambda r, k: (0, 0))],
        out_specs=pl.BlockSpec((te, d), residual_block),
        scratch_shapes=[pltpu.VMEM((tm, d), jnp.float32),
                        pltpu.VMEM((tm, d), jnp.bfloat16),
                        pltpu.VMEM((tm, 1), jnp.float32)],
        compiler_params=pltpu.CompilerParams(
            dimension_semantics=("arbitrary", "arbitrary"), vmem_limit_bytes=VMEM_LIMIT),
        name="out_proj",
    )(a1, a2, w, x2d, gain.reshape(1, d))


def kernel(x, w_in, attn_sinks, lb_logits, rnn_norm, w_out, pre_norm, post_norm):
    batch, seq, d_model = x.shape
    depth = w_in.shape[0]
    d_mix = w_out.shape[1]
    d_attn = d_mix // 2
    d_rnn = d_mix - d_attn
    x2d = x.reshape(batch * seq, d_model)
    tm = min(PROJ_TM, seq)
    for layer in range(depth):
        h = _prenorm(x2d, pre_norm[layer])
        proj, glo, worst, w_out_bf = _in_proj(h, w_in[layer], w_out[layer], lb_logits,
                                              d_attn, d_rnn, layer, tm)
        attn = _swa(proj, attn_sinks[layer], batch, seq, d_attn)
        rnn = _hgrn2(proj, glo, worst, rnn_norm[layer], batch, seq, d_attn, d_rnn, rows=tm)
        x2d = _out_proj(attn, rnn, w_out_bf, x2d, post_norm[layer])
    return x2d.reshape(batch, seq, d_model)
```

```python
import functools

import jax
import jax.numpy as jnp
import numpy as np
from jax import lax
from jax.experimental import pallas as pl
from jax.experimental.pallas import tpu as pltpu

ATTN_HEAD_DIM = 64
GQA_GROUP = 8
WINDOW = 128
RNN_HEAD_DIM = 128
NORM_EPS = 1e-6

LANES = 128
SUBLANES = 8
BF16_ROWS = 2 * SUBLANES
CHUNK = 128
N_LEVELS = 7
SAFE_LOG2 = 100.0
LOG2_E = 1.4426950408889634
VMEM_LIMIT = 56 * 1024 * 1024
SWA_BLOCKS = 4
SKEW_GAP = 3
PROJ_TM = 1024
PROJ_TN = 1024
GATE_ROW_CHUNKS = 8
W2_COL_BLOCKS = 16
PRENORM_ROWS = 512
OUT_TM = 1024
OUT_TK = 512
OUT_COL_CHUNK = 1024

_NT = (((1,), (1,)), ((), ()))
_TN = (((0,), (0,)), ((), ()))


def _skewed(n_items, stages, gap=SKEW_GAP):
    carry = [None] * n_items
    for step in range(n_items + gap * (len(stages) - 1)):
        for j in reversed(range(len(stages))):
            i = step - gap * j
            if 0 <= i < n_items:
                carry[i] = stages[j](i, carry[i])


def _silu(x):
    h = 0.5 * x
    return h + h * jnp.tanh(h)


def _prenorm_kernel(x_ref, g_ref, o_ref):
    x = x_ref[...]
    ms = jnp.mean(x * x, axis=-1, keepdims=True)
    o_ref[...] = (x * lax.rsqrt(ms + NORM_EPS) * g_ref[...]).astype(o_ref.dtype)


def _prenorm(x2d, gain, rows=PRENORM_ROWS):
    t, d = x2d.shape
    rows = min(rows, t)
    return pl.pallas_call(
        _prenorm_kernel,
        out_shape=jax.ShapeDtypeStruct((t, d), jnp.bfloat16),
        grid=(t // rows,),
        in_specs=[pl.BlockSpec((rows, d), lambda i: (i, 0)),
                  pl.BlockSpec((1, d), lambda i: (0, 0))],
        out_specs=pl.BlockSpec((rows, d), lambda i: (i, 0)),
        compiler_params=pltpu.CompilerParams(
            dimension_semantics=("parallel",), vmem_limit_bytes=VMEM_LIMIT),
        name="prenorm",
    )(x2d, gain.reshape(1, d))


def _in_proj_kernel(h_ref, wchunk_ref, w2_ref, lbl_ref, o_ref, glo_ref, worst_ref, w2o_ref, wbuf_ref,
                    *, silu_tiles, gate_tiles, layer):
    jo = pl.program_id(0)
    i = pl.program_id(1)
    rows = wchunk_ref.shape[0]
    chunk = pl.ds(pl.multiple_of(i * rows, rows), rows)
    tile = jo - 1

    def in_ranges(ranges):
        hits = [(tile >= lo) & (tile < hi) for lo, hi in ranges]
        return functools.reduce(jnp.logical_or, hits)

    is_silu = in_ranges(silu_tiles)
    is_gate = in_ranges(gate_tiles)

    def casts_and_dot(slot):
        wbuf_ref[slot, chunk, :] = wchunk_ref[...].astype(wbuf_ref.dtype)
        w2o_ref[...] = w2_ref[...].astype(w2o_ref.dtype)
        return jnp.dot(h_ref[...], wbuf_ref[1 - slot], preferred_element_type=jnp.float32)

    @pl.when(jo == 0)
    def _():
        wbuf_ref[0, chunk, :] = wchunk_ref[...].astype(wbuf_ref.dtype)
        w2o_ref[...] = w2_ref[...].astype(w2o_ref.dtype)

    @pl.when((jo > 0) & jnp.logical_not(is_silu | is_gate))
    def _():
        o_ref[...] = casts_and_dot(jo % 2).astype(o_ref.dtype)

    @pl.when(is_silu)
    def _():
        o_ref[...] = _silu(casts_and_dot(jo % 2)).astype(o_ref.dtype)

    @pl.when(is_gate)
    def _():
        lbl = lbl_ref[...]
        e = jnp.exp(lbl - jnp.max(lbl, axis=0, keepdims=True))
        lb = jnp.sum(e[:layer + 1], axis=0, keepdims=True) / jnp.sum(e, axis=0, keepdims=True)
        fa = 0.5 * (1.0 + lb)
        fb = 0.5 * (1.0 - lb)
        slot = jo % 2
        wbuf_ref[slot, chunk, :] = wchunk_ref[...].astype(wbuf_ref.dtype)
        w2o_ref[...] = w2_ref[...].astype(w2o_ref.dtype)
        rc = o_ref.shape[0] // GATE_ROW_CHUNKS
        half = CHUNK // 2
        worst = None
        for r in range(GATE_ROW_CHUNKS):
            rs = slice(r * rc, (r + 1) * rc)
            acc = jnp.dot(h_ref[rs, :], wbuf_ref[1 - slot], preferred_element_type=jnp.float32)
            g2 = jnp.log(fa + fb * jnp.tanh(0.5 * acc)) * LOG2_E
            g_hi = g2.astype(o_ref.dtype)
            o_ref[rs, :] = g_hi
            glo_ref[rs, :] = (g2 - g_hi.astype(jnp.float32)).astype(glo_ref.dtype)
            sums = jnp.sum(g2.reshape(rc // half, half, g2.shape[1]), axis=1)
            low = jnp.min(sums, axis=0, keepdims=True)
            worst = low if worst is None else jnp.minimum(worst, low)
        worst_ref[...] = jnp.broadcast_to(jnp.min(worst, axis=1, keepdims=True), worst_ref.shape)


def _tile_ranges(col_ranges, tn):
    assert all(lo % tn == 0 and hi % tn == 0 for lo, hi in col_ranges)
    return tuple((lo // tn, hi // tn) for lo, hi in col_ranges)


def _in_proj(h, w, w2, lb_logits, d_attn, d_rnn, layer, tm, tn=PROJ_TN,
             w2_col_blocks=W2_COL_BLOCKS):
    t, d = h.shape
    n = w.shape[1]
    k2, d2 = w2.shape
    ni, nj = t // tm, n // tn
    assert d % ni == 0 and (d // ni) % BF16_ROWS == 0 and k2 % ni == 0 and nj + 1 >= w2_col_blocks
    w2_blk = (k2 // ni, d2 // w2_col_blocks)
    rnn0 = d_attn + d_attn // 4 + d_attn
    assert rnn0 + 4 * d_rnn == n
    silu_tiles = _tile_ranges([(rnn0 - d_attn, rnn0), (rnn0, rnn0 + d_rnn),
                               (rnn0 + 3 * d_rnn, n)], tn)
    gate_tiles = _tile_ranges([(rnn0 + d_rnn, rnn0 + 2 * d_rnn)], tn)
    g0, g1 = gate_tiles[0]

    def w2_index(jo, i):
        done = jo >= w2_col_blocks
        return (jnp.where(done, ni - 1, i), jnp.where(done, w2_col_blocks - 1, jo))

    def glo_index(jo, i):
        tile = jo - 1
        row = jnp.where(tile < g0, 0, jnp.where(tile >= g1, ni - 1, i))
        return (row, jnp.clip(tile - g0, 0, g1 - g0 - 1))

    return pl.pallas_call(
        functools.partial(_in_proj_kernel, silu_tiles=silu_tiles, gate_tiles=gate_tiles,
                          layer=layer),
        out_shape=(jax.ShapeDtypeStruct((t, n), jnp.bfloat16),
                   jax.ShapeDtypeStruct((t, d_rnn), jnp.bfloat16),
                   jax.ShapeDtypeStruct((ni * SUBLANES, (g1 - g0) * LANES), jnp.float32),
                   jax.ShapeDtypeStruct((k2, d2), jnp.bfloat16)),
        grid=(nj + 1, ni),
        in_specs=[pl.BlockSpec((tm, d), lambda jo, i: (jnp.where(jo == 0, 0, i), 0)),
                  pl.BlockSpec((d // ni, tn), lambda jo, i: (i, jnp.minimum(jo, nj - 1))),
                  pl.BlockSpec(w2_blk, w2_index),
                  pl.BlockSpec((lb_logits.shape[0], tn),
                               lambda jo, i: (0, jnp.clip(jo - 1 - g0, 0, g1 - g0 - 1)))],
        out_specs=(pl.BlockSpec((tm, tn),
                                lambda jo, i: (jnp.where(jo == 0, 0, i), jnp.maximum(jo - 1, 0))),
                   pl.BlockSpec((tm, tn), glo_index),
                   pl.BlockSpec((SUBLANES, LANES), glo_index),
                   pl.BlockSpec(w2_blk, w2_index)),
        scratch_shapes=[pltpu.VMEM((2, d, tn), jnp.bfloat16)],
        compiler_params=pltpu.CompilerParams(
            dimension_semantics=("arbitrary", "arbitrary"), vmem_limit_bytes=VMEM_LIMIT),
        name="in_proj",
    )(h, w, w2, lb_logits)


def _lane_halves(slab, head_in_high_half):
    lane = lax.broadcasted_iota(jnp.int32, slab.shape, 1)
    swapped = pltpu.roll(slab, ATTN_HEAD_DIM, axis=1)
    zero = jnp.zeros_like(slab)
    if head_in_high_half:
        lo, hi = swapped, slab
    else:
        lo, hi = slab, swapped
    return (jnp.where(lane < ATTN_HEAD_DIM, lo, zero), jnp.where(lane >= ATTN_HEAD_DIM, hi, zero))


def _swa_kernel(sink_ref, q_ref, kvp_ref, kvc_ref, g0_ref, g1_ref, g2_ref, g3_ref, o_ref,
                *, n_kv_heads, d_kv, blocks):
    n = pl.program_id(1)
    gate_refs = (g0_ref, g1_ref, g2_ref, g3_ref)
    w = WINDOW
    qi = lax.broadcasted_iota(jnp.int32, (w, w), 0)
    kj = lax.broadcasted_iota(jnp.int32, (w, w), 1)
    lower = kj <= qi
    first_bias = jnp.where(n > 0, 0.0, -jnp.inf)
    lane = lax.broadcasted_iota(jnp.int32, (w, LANES), 1)
    scale = ATTN_HEAD_DIM ** -0.5 * LOG2_E
    pairs_per_kv = GQA_GROUP // 2
    n_pairs = n_kv_heads * pairs_per_kv
    pairs_per_gate_ref = n_pairs // len(gate_refs)

    kv_cache = {}

    def kv_operands(j, h):
        if (j, h) not in kv_cache:
            c0 = (h // 2) * LANES
            high = (h % 2) == 1
            cur = slice(j * w, (j + 1) * w)

            def slab(col):
                prev = (kvp_ref[:, col:col + LANES] if j == 0
                        else kvc_ref[(j - 1) * w:j * w, col:col + LANES])
                return jnp.concatenate([prev, kvc_ref[cur, col:col + LANES]],
                                       axis=0).astype(jnp.float32)

            k_lo, k_hi = _lane_halves(slab(c0) * scale, high)
            v_lo, v_hi = _lane_halves(slab(d_kv + c0), high)
            kv_cache[(j, h)] = (jnp.concatenate([k_lo, k_hi], axis=0).astype(jnp.bfloat16),
                                jnp.concatenate([v_lo, v_hi], axis=0).astype(jnp.bfloat16))
        return kv_cache[(j, h)]

    def where(item):
        j, p = divmod(item, n_pairs)
        return j, p, slice(j * w, (j + 1) * w)

    def score_stage(item, _):
        j, p, rows = where(item)
        qp = q_ref[rows, p * LANES:(p + 1) * LANES]
        kcat = kv_operands(j, p // pairs_per_kv)[0]
        return lax.dot_general(qp, kcat, _NT, preferred_element_type=jnp.float32)

    def max_stage(item, s):
        j, p, _ = where(item)
        merged, mx = [], []
        for hh in range(2):
            prev = s[:, hh * 2 * w:hh * 2 * w + w]
            if j == 0:
                prev = prev + first_bias
            cur = s[:, hh * 2 * w + w:(hh + 1) * 2 * w]
            sh = jnp.where(lower, cur, prev)
            merged.append(sh)
            mx.append(jnp.maximum(jnp.max(sh, axis=-1, keepdims=True), sink_ref[2 * p + hh] * LOG2_E))
        return merged, mx

    def exp_stage(item, carry):
        _, p, _ = where(item)
        merged, mx = carry
        probs, denom = [], []
        for hh in range(2):
            e = jnp.exp2(merged[hh] - mx[hh])
            denom.append(jnp.sum(e, axis=-1, keepdims=True)
                         + jnp.exp2(sink_ref[2 * p + hh] * LOG2_E - mx[hh]))
            e = e.astype(jnp.bfloat16)
            zero = jnp.zeros_like(e)
            probs += [jnp.where(lower, zero, e), jnp.where(lower, e, zero)]
        return jnp.concatenate(probs, axis=1), denom

    def value_stage(item, carry):
        j, p, _ = where(item)
        pcat, denom = carry
        vcat = kv_operands(j, p // pairs_per_kv)[1]
        return jnp.dot(pcat, vcat, preferred_element_type=jnp.float32), denom

    def out_stage(item, carry):
        _, p, rows = where(item)
        o, denom = carry
        o = o * jnp.where(lane < ATTN_HEAD_DIM, 1.0 / denom[0], 1.0 / denom[1])
        gi, gp = divmod(p, pairs_per_gate_ref)
        gate = gate_refs[gi][rows, gp * LANES:(gp + 1) * LANES].astype(jnp.float32)
        o_ref[rows, p * LANES:(p + 1) * LANES] = (o * gate).astype(o_ref.dtype)

    _skewed(blocks * n_pairs, [score_stage, max_stage, exp_stage, value_stage, out_stage])


def _swa(proj, sinks, batch, seq, d_attn, blocks=SWA_BLOCKS):
    t = proj.shape[0]
    w = WINDOW
    nb = seq // w
    blocks = min(blocks, nb)
    assert nb % blocks == 0
    nbs = nb // blocks
    n_q = d_attn // ATTN_HEAD_DIM
    n_kv = n_q // GQA_GROUP
    d_kv = n_kv * ATTN_HEAD_DIM
    kvw = 2 * d_kv
    assert d_attn == 4 * kvw
    kv_blk = d_attn // kvw
    gate_blk0 = (d_attn + kvw) // kvw

    def rows(b, n):
        return b * nbs + n

    def prev_block(b, n):
        return b * nb + jnp.maximum(n * blocks - 1, 0)

    in_specs = [
        pl.BlockSpec(memory_space=pltpu.SMEM),
        pl.BlockSpec((blocks * w, d_attn), lambda b, n: (rows(b, n), 0)),
        pl.BlockSpec((w, kvw), lambda b, n: (prev_block(b, n), kv_blk)),
        pl.BlockSpec((blocks * w, kvw), lambda b, n: (rows(b, n), kv_blk)),
    ] + [
        pl.BlockSpec((blocks * w, kvw),
                     functools.partial(lambda b, n, j: (rows(b, n), gate_blk0 + j), j=j))
        for j in range(4)
    ]
    return pl.pallas_call(
        functools.partial(_swa_kernel, n_kv_heads=n_kv, d_kv=d_kv, blocks=blocks),
        out_shape=jax.ShapeDtypeStruct((t, d_attn), jnp.bfloat16),
        grid=(batch, nbs),
        in_specs=in_specs,
        out_specs=pl.BlockSpec((blocks * w, d_attn), lambda b, n: (rows(b, n), 0)),
        compiler_params=pltpu.CompilerParams(
            dimension_semantics=("parallel", "parallel"), vmem_limit_bytes=VMEM_LIMIT),
        name="swa",
    )(sinks, proj, proj, proj, proj, proj, proj, proj)


def _decay_tables():
    c = CHUNK
    n = np.zeros((N_LEVELS + 1, c, c), np.float32)
    r = np.arange(c)
    for l in range(1, N_LEVELS):
        half = 1 << l
        m = 2 * half
        for t in range(c):
            start = (t // m) * m
            mid = start + half - 1
            if t > mid:
                n[l - 1, t] = (r > mid) & (r <= t)
            else:
                n[l - 1, t] = (r > t) & (r <= mid)
    n[N_LEVELS - 1] = r[None, :] <= r[:, None]
    n[N_LEVELS] = r[None, :] > r[:, None]
    cumsum = np.concatenate([n[N_LEVELS - 1], n[N_LEVELS - 1]], axis=1)
    n = n.reshape((N_LEVELS + 1) * c, c)
    ncat = np.concatenate([n, n], axis=1)
    tt, ss = np.meshgrid(r, r, indexing="ij")
    x = tt ^ ss
    level = np.full((c, c), -1, np.int32)
    for l in range(N_LEVELS):
        level[(ss < tt) & (x >= (1 << l)) & (x < (2 << l))] = l
    return ncat, cumsum, level


def _interleave_rows(k, q, half):
    pieces = []
    for r0 in range(0, k.shape[0], half):
        src = q if (r0 // half) % 2 else k
        pieces.append(src[r0:r0 + half])
    return jnp.concatenate(pieces, axis=0)


def _hgrn2_kernel(ncat_ref, cum_ref, level_ref, gain_ref, worst_ref, q_ref, ghi_ref, glo_ref, v_ref,
                  gate_ref, o_ref, state_ref, *, heads, chunks):
    c = CHUNK
    pw = 2 * LANES
    n_pairs = heads // 2

    @pl.when(pl.program_id(2) == 0)
    def _():
        state_ref[...] = jnp.zeros_like(state_ref)

    mild = jnp.min(worst_ref[...]) > -SAFE_LOG2

    def key_gate(rows, cols):
        g2 = ghi_ref[rows, cols].astype(jnp.float32) + glo_ref[rows, cols].astype(jnp.float32)
        return 1.0 - jnp.exp2(g2)

    def head_matmuls(a_bf, q_op, k_op, v_bf, st):
        intra = jnp.dot(a_bf, v_bf, preferred_element_type=jnp.float32)
        inter = lax.dot_general(q_op, st.astype(jnp.bfloat16), _NT,
                                preferred_element_type=jnp.float32)
        upd = lax.dot_general(v_bf, k_op, _TN, preferred_element_type=jnp.float32)
        return inter + intra, upd

    def head_tail(a_bf, qi, kl, egl, v_bf, idx):
        st = state_ref[idx]
        o, upd = head_matmuls(a_bf, qi, kl, v_bf, st)
        state_ref[idx] = egl * st + upd
        return o

    def write_out(outs, rows, cols):
        normed = []
        for o in outs:
            ms = jnp.mean(o * o, axis=-1, keepdims=True)
            normed.append(o * lax.rsqrt(ms + NORM_EPS))
        o2 = jnp.concatenate(normed, axis=1) * gain_ref[:, cols]
        gate = gate_ref[rows, cols].astype(jnp.float32)
        o_ref[rows, cols] = (o2 * gate).astype(o_ref.dtype)

    @pl.when(mild)
    def _():
        ti = lax.broadcasted_iota(jnp.int32, (c, c), 0)
        si = lax.broadcasted_iota(jnp.int32, (c, c), 1)
        causal = si <= ti
        halves = [slice(hh * LANES, (hh + 1) * LANES) for hh in range(2)]

        def where(item):
            ci, pr = divmod(item, n_pairs)
            return slice(ci * c, (ci + 1) * c), slice(pr * pw, (pr + 1) * pw), pr

        def cumsum_stage(item, _):
            rows, cols, _ = where(item)
            gcat = jnp.concatenate([ghi_ref[rows, cols], glo_ref[rows, cols]], axis=0)
            return jnp.dot(cum_ref[...], gcat, preferred_element_type=jnp.float32)

        def scale_stage(item, gsum):
            rows, cols, _ = where(item)
            ref = gsum[c // 2 - 1:c // 2, :]
            last = gsum[c - 1:c, :]
            decay = jnp.exp2(gsum - ref)
            qp = (q_ref[rows, cols].astype(jnp.float32) * decay).astype(jnp.bfloat16)
            kp = (key_gate(rows, cols) * (1.0 / decay)).astype(jnp.bfloat16)
            a = [lax.dot_general(qp[:, hs], kp[:, hs], _NT, preferred_element_type=jnp.float32)
                 for hs in halves]
            return a, qp, kp, jnp.exp2(ref), jnp.exp2(last - ref)

        def mix_stage(item, carry):
            a, qp, kp, e_ref, e_rest = carry
            rows, cols, pr = where(item)
            v_bf = v_ref[rows, cols]
            heads_out = []
            for hh, hs in enumerate(halves):
                st = state_ref[2 * pr + hh] * e_ref[:, hs]
                o, upd = head_matmuls(jnp.where(causal, a[hh], 0.0).astype(jnp.bfloat16),
                                      qp[:, hs], kp[:, hs], v_bf[:, hs], st)
                heads_out.append((o, st, upd))
            return heads_out, e_rest

        def out_stage(item, carry):
            heads_out, e_rest = carry
            rows, cols, pr = where(item)
            for hh, hs in enumerate(halves):
                _, st, upd = heads_out[hh]
                state_ref[2 * pr + hh] = (st + upd) * e_rest[:, hs]
            write_out([o for o, _, _ in heads_out], rows, cols)

        _skewed(chunks * n_pairs, [cumsum_stage, scale_stage, mix_stage, out_stage],
                gap=min(SKEW_GAP, n_pairs))

    @pl.when(jnp.logical_not(mild))
    def _():
        level = level_ref[...]
        trow = lax.broadcasted_iota(jnp.int32, (c, pw), 0)

        def chunk_body(ci, carry):
            rows = pl.ds(pl.multiple_of(ci * c, c), c)
            for pr in range(n_pairs):
                cols = slice(pr * pw, (pr + 1) * pw)
                q = q_ref[rows, cols].astype(jnp.float32)
                g_hi = ghi_ref[rows, cols]
                g_lo = glo_ref[rows, cols]
                f = jnp.exp2(g_hi.astype(jnp.float32) + g_lo.astype(jnp.float32))
                k = 1.0 - f
                gcat = jnp.concatenate([g_hi, g_lo], axis=0)
                ex = jnp.dot(ncat_ref[...], gcat, preferred_element_type=jnp.float32)

                z = [jnp.where((trow & 1) == 1, q * f, k).astype(jnp.bfloat16)]
                for l in range(1, N_LEVELS):
                    half = 1 << l
                    if half < SUBLANES:
                        qk = jnp.where((trow & half) != 0, q, k)
                    else:
                        qk = _interleave_rows(k, q, half)
                    z.append((qk * jnp.exp2(ex[(l - 1) * c:l * c])).astype(jnp.bfloat16))
                eg = jnp.exp2(ex[(N_LEVELS - 1) * c:N_LEVELS * c])
                erest = jnp.exp2(ex[N_LEVELS * c:(N_LEVELS + 1) * c])
                qi = (q * eg).astype(jnp.bfloat16)
                kl = (k * erest).astype(jnp.bfloat16)
                qk_diag = q * k
                v_bf = v_ref[rows, cols]
                outs = []
                for hh in range(2):
                    hs = slice(hh * LANES, (hh + 1) * LANES)
                    attn = jnp.zeros((c, c), jnp.float32)
                    for l in range(N_LEVELS):
                        zl = z[l][:, hs]
                        a = lax.dot_general(zl, zl, _NT, preferred_element_type=jnp.float32)
                        attn = jnp.where(level == l, a, attn)
                    diag = jnp.sum(qk_diag[:, hs], axis=-1, keepdims=True)
                    o = head_tail(attn.astype(jnp.bfloat16), qi[:, hs], kl[:, hs],
                                  eg[c - 1:c, hs], v_bf[:, hs], 2 * pr + hh)
                    outs.append(o + diag * v_bf[:, hs].astype(jnp.float32))
                write_out(outs, rows, cols)
            return carry

        lax.fori_loop(0, chunks, chunk_body, 0)


def _hgrn2(proj, glo, worst, gain, batch, seq, d_attn, d_rnn, rows, wb=PROJ_TN):
    t = proj.shape[0]
    assert seq % rows == 0 and rows % CHUNK == 0 and wb % (2 * RNN_HEAD_DIM) == 0
    chunks = rows // CHUNK
    heads = wb // RNN_HEAD_DIM
    nl = seq // rows
    n_hg = d_rnn // wb
    kvw = d_attn // 4
    base = d_attn + kvw + d_attn
    assert base % wb == 0
    ncat, cumsum, level = _decay_tables()

    def seg(i):
        blk0 = (base + i * d_rnn) // wb
        return pl.BlockSpec((rows, wb), lambda b, hg, l: (b * nl + l, blk0 + hg))

    const = lambda shape: pl.BlockSpec(shape, lambda b, hg, l: (0, 0))
    own = pl.BlockSpec((rows, wb), lambda b, hg, l: (b * nl + l, hg))
    return pl.pallas_call(
        functools.partial(_hgrn2_kernel, heads=heads, chunks=chunks),
        out_shape=jax.ShapeDtypeStruct((t, d_rnn), jnp.bfloat16),
        grid=(batch, n_hg, nl),
        in_specs=[
            const(ncat.shape), const(cumsum.shape), const(level.shape),
            pl.BlockSpec((1, wb), lambda b, hg, l: (0, hg)),
            pl.BlockSpec((SUBLANES, LANES), lambda b, hg, l: (b * nl + l, hg)),
            seg(0), seg(1), own, seg(2), seg(3),
        ],
        out_specs=own,
        scratch_shapes=[pltpu.VMEM((heads, RNN_HEAD_DIM, RNN_HEAD_DIM), jnp.float32)],
        compiler_params=pltpu.CompilerParams(
            dimension_semantics=("parallel", "parallel", "arbitrary"),
            vmem_limit_bytes=VMEM_LIMIT),
        name="hgrn2",
    )(jnp.asarray(ncat, jnp.bfloat16), jnp.asarray(cumsum, jnp.bfloat16), jnp.asarray(level),
      gain.reshape(1, d_rnn), worst, proj, proj, glo, proj, proj)


def _out_proj_kernel(a1_ref, a2_ref, w_ref, x_ref, g_ref, o_ref, acc_ref, y_ref, ssq_ref,
                     *, nk1, n_tiles):
    r = pl.program_id(0)
    kk = pl.program_id(1)
    nk = pl.num_programs(1)
    d = acc_ref.shape[1]
    te = o_ref.shape[0]

    def residual_chunk():
        rows = pl.ds(pl.multiple_of(kk * te, te), te)
        inv = lax.rsqrt(ssq_ref[rows, :] * (1.0 / d) + NORM_EPS)
        o_ref[...] = x_ref[...] + y_ref[rows, :].astype(jnp.float32) * inv * g_ref[...]

    def accumulate(last):
        a = jnp.where(kk < nk1, a1_ref[...], a2_ref[...])
        ssq = None
        for n0 in range(0, d, OUT_COL_CHUNK):
            cols = slice(n0, n0 + OUT_COL_CHUNK)
            part = jnp.dot(a, w_ref[:, cols], preferred_element_type=jnp.float32)
            y = jnp.where(kk == 0, part, acc_ref[:, cols] + part)
            if last:
                y_ref[:, cols] = y.astype(y_ref.dtype)
                s = jnp.sum(y * y, axis=-1, keepdims=True)
                ssq = s if ssq is None else ssq + s
            else:
                acc_ref[:, cols] = y
        if last:
            ssq_ref[...] = ssq

    first, drain = r == 0, r == n_tiles
    mid = jnp.logical_not(first | drain)
    last_slab = kk == nk - 1

    @pl.when(first & (kk == 0))
    def _():
        acc_ref[...] = jnp.zeros_like(acc_ref)

    @pl.when(first & jnp.logical_not(last_slab))
    def _():
        accumulate(False)

    @pl.when(first & last_slab)
    def _():
        accumulate(True)

    @pl.when(mid & jnp.logical_not(last_slab))
    def _():
        residual_chunk()
        accumulate(False)

    @pl.when(mid & last_slab)
    def _():
        residual_chunk()
        accumulate(True)

    @pl.when(drain)
    def _():
        residual_chunk()


def _out_proj(a1, a2, w, x2d, gain, tm=OUT_TM, tk=OUT_TK):
    t, k1 = a1.shape
    k2 = a2.shape[1]
    d = w.shape[1]
    tm = min(tm, t)
    n_tiles = t // tm
    nk1, nk2 = k1 // tk, k2 // tk
    nk = nk1 + nk2
    te = tm // nk
    assert tm % nk == 0 and te % SUBLANES == 0

    def a_row(r):
        return jnp.minimum(r, n_tiles - 1)

    def residual_block(r, k):
        return (jnp.where(r == 0, 0, (r - 1) * nk + k), 0)

    return pl.pallas_call(
        functools.partial(_out_proj_kernel, nk1=nk1, n_tiles=n_tiles),
        out_shape=jax.ShapeDtypeStruct((t, d), jnp.float32),
        grid=(n_tiles + 1, nk),
        in_specs=[pl.BlockSpec((tm, tk), lambda r, k: (a_row(r), jnp.minimum(k, nk1 - 1))),
                  pl.BlockSpec((tm, tk), lambda r, k: (a_row(r), jnp.clip(k - nk1, 0, nk2 - 1))),
                  pl.BlockSpec((tk, d), lambda r, k: (jnp.where(r == n_tiles, nk - 1, k), 0)),
                  pl.BlockSpec((te, d), residual_block),
                  pl.BlockSpec((1, d), lambda r, k: (0, 0))],
        out_specs=pl.BlockSpec((te, d), residual_block),
        scratch_shapes=[pltpu.VMEM((tm, d), jnp.float32),
                        pltpu.VMEM((tm, d), jnp.bfloat16),
                        pltpu.VMEM((tm, 1), jnp.float32)],
        compiler_params=pltpu.CompilerParams(
            dimension_semantics=("arbitrary", "arbitrary"), vmem_limit_bytes=VMEM_LIMIT),
        name="out_proj",
    )(a1, a2, w, x2d, gain.reshape(1, d))


def kernel(x, w_in, attn_sinks, lb_logits, rnn_norm, w_out, pre_norm, post_norm):
    batch, seq, d_model = x.shape
    depth = w_in.shape[0]
    d_mix = w_out.shape[1]
    d_attn = d_mix // 2
    d_rnn = d_mix - d_attn
    x2d = x.reshape(batch * seq, d_model)
    tm = min(PROJ_TM, seq)
    for layer in range(depth):
        h = _prenorm(x2d, pre_norm[layer])
        proj, glo, worst, w_out_bf = _in_proj(h, w_in[layer], w_out[layer], lb_logits,
                                              d_attn, d_rnn, layer, tm)
        attn = _swa(proj, attn_sinks[layer], batch, seq, d_attn)
        rnn = _hgrn2(proj, glo, worst, rnn_norm[layer], batch, seq, d_attn, d_rnn, rows=tm)
        x2d = _out_proj(attn, rnn, w_out_bf, x2d, post_norm[layer])
    return x2d.reshape(batch, seq, d_model)
```

```python
import functools

import jax
import jax.numpy as jnp
import numpy as np
from jax import lax
from jax.experimental import pallas as pl
from jax.experimental.pallas import tpu as pltpu

ATTN_HEAD_DIM = 64
GQA_GROUP = 8
WINDOW = 128
RNN_HEAD_DIM = 128
NORM_EPS = 1e-6

LANES = 128
SUBLANES = 8
BF16_ROWS = 2 * SUBLANES
CHUNK = 128
N_LEVELS = 7
SAFE_LOG2 = 100.0
LOG2_E = 1.4426950408889634
VMEM_LIMIT = 56 * 1024 * 1024
SWA_BLOCKS = 4
SKEW_GAP = 3
PROJ_TM = 1024
PROJ_TN = 1024
GATE_ROW_CHUNKS = 8
W2_COL_BLOCKS = 16
PRENORM_ROWS = 512
OUT_TM = 1024
OUT_TK = 512
OUT_COL_CHUNK = 1024

_NT = (((1,), (1,)), ((), ()))
_TN = (((0,), (0,)), ((), ()))


def _skewed(n_items, stages, gap=SKEW_GAP):
    carry = [None] * n_items
    for step in range(n_items + gap * (len(stages) - 1)):
        for j in reversed(range(len(stages))):
            i = step - gap * j
            if 0 <= i < n_items:
                carry[i] = stages[j](i, carry[i])


def _silu(x):
    h = 0.5 * x
    return h + h * jnp.tanh(h)


def _prenorm_kernel(x_ref, g_ref, o_ref):
    x = x_ref[...]
    ms = jnp.mean(x * x, axis=-1, keepdims=True)
    o_ref[...] = (x * lax.rsqrt(ms + NORM_EPS) * g_ref[...]).astype(o_ref.dtype)


def _prenorm(x2d, gain, rows=PRENORM_ROWS):
    t, d = x2d.shape
    rows = min(rows, t)
    return pl.pallas_call(
        _prenorm_kernel,
        out_shape=jax.ShapeDtypeStruct((t, d), jnp.bfloat16),
        grid=(t // rows,),
        in_specs=[pl.BlockSpec((rows, d), lambda i: (i, 0)),
                  pl.BlockSpec((1, d), lambda i: (0, 0))],
        out_specs=pl.BlockSpec((rows, d), lambda i: (i, 0)),
        compiler_params=pltpu.CompilerParams(
            dimension_semantics=("parallel",), vmem_limit_bytes=VMEM_LIMIT),
        name="prenorm",
    )(x2d, gain.reshape(1, d))


def _in_proj_kernel(h_ref, wchunk_ref, w2_ref, lbl_ref, o_ref, glo_ref, worst_ref, w2o_ref, wbuf_ref,
                    *, silu_tiles, gate_tiles, layer):
    jo = pl.program_id(0)
    i = pl.program_id(1)
    rows = wchunk_ref.shape[0]
    chunk = pl.ds(pl.multiple_of(i * rows, rows), rows)
    tile = jo - 1

    def in_ranges(ranges):
        hits = [(tile >= lo) & (tile < hi) for lo, hi in ranges]
        return functools.reduce(jnp.logical_or, hits)

    is_silu = in_ranges(silu_tiles)
    is_gate = in_ranges(gate_tiles)

    def casts(slot):
        wbuf_ref[slot, chunk, :] = wchunk_ref[...].astype(wbuf_ref.dtype)
        w2o_ref[...] = w2_ref[...].astype(w2o_ref.dtype)

    def casts_and_dot(slot):
        acc = jnp.dot(h_ref[...], wbuf_ref[1 - slot], preferred_element_type=jnp.float32)
        casts(slot)
        return acc

    @pl.when(jo == 0)
    def _():
        casts(0)

    @pl.when((jo > 0) & jnp.logical_not(is_silu | is_gate))
    def _():
        o_ref[...] = casts_and_dot(jo % 2).astype(o_ref.dtype)

    @pl.when(is_silu)
    def _():
        o_ref[...] = _silu(casts_and_dot(jo % 2)).astype(o_ref.dtype)

    @pl.when(is_gate)
    def _():
        lbl = lbl_ref[...]
        e = jnp.exp(lbl - jnp.max(lbl, axis=0, keepdims=True))
        lb = jnp.sum(e[:layer + 1], axis=0, keepdims=True) / jnp.sum(e, axis=0, keepdims=True)
        fa = 0.5 * (1.0 + lb)
        fb = 0.5 * (1.0 - lb)
        slot = jo % 2
        rc = o_ref.shape[0] // GATE_ROW_CHUNKS
        half = CHUNK // 2
        worst = None
        for r in range(GATE_ROW_CHUNKS):
            rs = slice(r * rc, (r + 1) * rc)
            acc = jnp.dot(h_ref[rs, :], wbuf_ref[1 - slot], preferred_element_type=jnp.float32)
            g2 = jnp.log(fa + fb * jnp.tanh(0.5 * acc)) * LOG2_E
            g_hi = g2.astype(o_ref.dtype)
            o_ref[rs, :] = g_hi
            glo_ref[rs, :] = (g2 - g_hi.astype(jnp.float32)).astype(glo_ref.dtype)
            sums = jnp.sum(g2.reshape(rc // half, half, g2.shape[1]), axis=1)
            low = jnp.min(sums, axis=0, keepdims=True)
            worst = low if worst is None else jnp.minimum(worst, low)
        worst_ref[...] = jnp.broadcast_to(jnp.min(worst, axis=1, keepdims=True), worst_ref.shape)
        casts(slot)


def _tile_ranges(col_ranges, tn):
    assert all(lo % tn == 0 and hi % tn == 0 for lo, hi in col_ranges)
    return tuple((lo // tn, hi // tn) for lo, hi in col_ranges)


def _in_proj(h, w, w2, lb_logits, d_attn, d_rnn, layer, tm, tn=PROJ_TN,
             w2_col_blocks=W2_COL_BLOCKS):
    t, d = h.shape
    n = w.shape[1]
    k2, d2 = w2.shape
    ni, nj = t // tm, n // tn
    assert d % ni == 0 and (d // ni) % BF16_ROWS == 0 and k2 % ni == 0 and nj + 1 >= w2_col_blocks
    w2_blk = (k2 // ni, d2 // w2_col_blocks)
    rnn0 = d_attn + d_attn // 4 + d_attn
    assert rnn0 + 4 * d_rnn == n
    silu_tiles = _tile_ranges([(rnn0 - d_attn, rnn0), (rnn0, rnn0 + d_rnn),
                               (rnn0 + 3 * d_rnn, n)], tn)
    gate_tiles = _tile_ranges([(rnn0 + d_rnn, rnn0 + 2 * d_rnn)], tn)
    g0, g1 = gate_tiles[0]

    def w2_index(jo, i):
        done = jo >= w2_col_blocks
        return (jnp.where(done, ni - 1, i), jnp.where(done, w2_col_blocks - 1, jo))

    def glo_index(jo, i):
        tile = jo - 1
        row = jnp.where(tile < g0, 0, jnp.where(tile >= g1, ni - 1, i))
        return (row, jnp.clip(tile - g0, 0, g1 - g0 - 1))

    return pl.pallas_call(
        functools.partial(_in_proj_kernel, silu_tiles=silu_tiles, gate_tiles=gate_tiles,
                          layer=layer),
        out_shape=(jax.ShapeDtypeStruct((t, n), jnp.bfloat16),
                   jax.ShapeDtypeStruct((t, d_rnn), jnp.bfloat16),
                   jax.ShapeDtypeStruct((ni * SUBLANES, (g1 - g0) * LANES), jnp.float32),
                   jax.ShapeDtypeStruct((k2, d2), jnp.bfloat16)),
        grid=(nj + 1, ni),
        in_specs=[pl.BlockSpec((tm, d), lambda jo, i: (jnp.where(jo == 0, 0, i), 0)),
                  pl.BlockSpec((d // ni, tn), lambda jo, i: (i, jnp.minimum(jo, nj - 1))),
                  pl.BlockSpec(w2_blk, w2_index),
                  pl.BlockSpec((lb_logits.shape[0], tn),
                               lambda jo, i: (0, jnp.clip(jo - 1 - g0, 0, g1 - g0 - 1)))],
        out_specs=(pl.BlockSpec((tm, tn),
                                lambda jo, i: (jnp.where(jo == 0, 0, i), jnp.maximum(jo - 1, 0))),
                   pl.BlockSpec((tm, tn), glo_index),
                   pl.BlockSpec((SUBLANES, LANES), glo_index),
                   pl.BlockSpec(w2_blk, w2_index)),
        scratch_shapes=[pltpu.VMEM((2, d, tn), jnp.bfloat16)],
        compiler_params=pltpu.CompilerParams(
            dimension_semantics=("arbitrary", "arbitrary"), vmem_limit_bytes=VMEM_LIMIT),
        name="in_proj",
    )(h, w, w2, lb_logits)


def _lane_halves(slab, head_in_high_half):
    lane = lax.broadcasted_iota(jnp.int32, slab.shape, 1)
    swapped = pltpu.roll(slab, ATTN_HEAD_DIM, axis=1)
    zero = jnp.zeros_like(slab)
    if head_in_high_half:
        lo, hi = swapped, slab
    else:
        lo, hi = slab, swapped
    return (jnp.where(lane < ATTN_HEAD_DIM, lo, zero), jnp.where(lane >= ATTN_HEAD_DIM, hi, zero))


def _swa_kernel(sink_ref, q_ref, kvp_ref, kvc_ref, g0_ref, g1_ref, g2_ref, g3_ref, o_ref,
                *, n_kv_heads, d_kv, blocks):
    n = pl.program_id(1)
    gate_refs = (g0_ref, g1_ref, g2_ref, g3_ref)
    w = WINDOW
    qi = lax.broadcasted_iota(jnp.int32, (w, w), 0)
    kj = lax.broadcasted_iota(jnp.int32, (w, w), 1)
    lower = kj <= qi
    first_bias = jnp.where(n > 0, 0.0, -jnp.inf)
    lane = lax.broadcasted_iota(jnp.int32, (w, LANES), 1)
    scale = ATTN_HEAD_DIM ** -0.5 * LOG2_E
    pairs_per_kv = GQA_GROUP // 2
    n_pairs = n_kv_heads * pairs_per_kv
    pairs_per_gate_ref = n_pairs // len(gate_refs)

    kv_cache = {}

    def kv_operands(j, h):
        if (j, h) not in kv_cache:
            c0 = (h // 2) * LANES
            high = (h % 2) == 1
            cur = slice(j * w, (j + 1) * w)

            def slab(col):
                prev = (kvp_ref[:, col:col + LANES] if j == 0
                        else kvc_ref[(j - 1) * w:j * w, col:col + LANES])
                return jnp.concatenate([prev, kvc_ref[cur, col:col + LANES]],
                                       axis=0).astype(jnp.float32)

            k_lo, k_hi = _lane_halves(slab(c0) * scale, high)
            v_lo, v_hi = _lane_halves(slab(d_kv + c0), high)
            kv_cache[(j, h)] = (jnp.concatenate([k_lo, k_hi], axis=0).astype(jnp.bfloat16),
                                jnp.concatenate([v_lo, v_hi], axis=0).astype(jnp.bfloat16))
        return kv_cache[(j, h)]

    def where(item):
        j, p = divmod(item, n_pairs)
        return j, p, slice(j * w, (j + 1) * w)

    def score_stage(item, _):
        j, p, rows = where(item)
        qp = q_ref[rows, p * LANES:(p + 1) * LANES]
        kcat = kv_operands(j, p // pairs_per_kv)[0]
        return lax.dot_general(qp, kcat, _NT, preferred_element_type=jnp.float32)

    def max_stage(item, s):
        j, p, _ = where(item)
        merged, mx = [], []
        for hh in range(2):
            prev = s[:, hh * 2 * w:hh * 2 * w + w]
            if j == 0:
                prev = prev + first_bias
            cur = s[:, hh * 2 * w + w:(hh + 1) * 2 * w]
            sh = jnp.where(lower, cur, prev)
            merged.append(sh)
            mx.append(jnp.maximum(jnp.max(sh, axis=-1, keepdims=True), sink_ref[2 * p + hh] * LOG2_E))
        return merged, mx

    def exp_stage(item, carry):
        _, p, _ = where(item)
        merged, mx = carry
        probs, denom = [], []
        for hh in range(2):
            e = jnp.exp2(merged[hh] - mx[hh])
            denom.append(jnp.sum(e, axis=-1, keepdims=True)
                         + jnp.exp2(sink_ref[2 * p + hh] * LOG2_E - mx[hh]))
            e = e.astype(jnp.bfloat16)
            zero = jnp.zeros_like(e)
            probs += [jnp.where(lower, zero, e), jnp.where(lower, e, zero)]
        return jnp.concatenate(probs, axis=1), denom

    def value_stage(item, carry):
        j, p, _ = where(item)
        pcat, denom = carry
        vcat = kv_operands(j, p // pairs_per_kv)[1]
        return jnp.dot(pcat, vcat, preferred_element_type=jnp.float32), denom

    def out_stage(item, carry):
        _, p, rows = where(item)
        o, denom = carry
        o = o * jnp.where(lane < ATTN_HEAD_DIM, 1.0 / denom[0], 1.0 / denom[1])
        gi, gp = divmod(p, pairs_per_gate_ref)
        gate = gate_refs[gi][rows, gp * LANES:(gp + 1) * LANES].astype(jnp.float32)
        o_ref[rows, p * LANES:(p + 1) * LANES] = (o * gate).astype(o_ref.dtype)

    _skewed(blocks * n_pairs, [score_stage, max_stage, exp_stage, value_stage, out_stage])


def _swa(proj, sinks, batch, seq, d_attn, blocks=SWA_BLOCKS):
    t = proj.shape[0]
    w = WINDOW
    nb = seq // w
    blocks = min(blocks, nb)
    assert nb % blocks == 0
    nbs = nb // blocks
    n_q = d_attn // ATTN_HEAD_DIM
    n_kv = n_q // GQA_GROUP
    d_kv = n_kv * ATTN_HEAD_DIM
    kvw = 2 * d_kv
    assert d_attn == 4 * kvw
    kv_blk = d_attn // kvw
    gate_blk0 = (d_attn + kvw) // kvw

    def rows(b, n):
        return b * nbs + n

    def prev_block(b, n):
        return b * nb + jnp.maximum(n * blocks - 1, 0)

    in_specs = [
        pl.BlockSpec(memory_space=pltpu.SMEM),
        pl.BlockSpec((blocks * w, d_attn), lambda b, n: (rows(b, n), 0)),
        pl.BlockSpec((w, kvw), lambda b, n: (prev_block(b, n), kv_blk)),
        pl.BlockSpec((blocks * w, kvw), lambda b, n: (rows(b, n), kv_blk)),
    ] + [
        pl.BlockSpec((blocks * w, kvw),
                     functools.partial(lambda b, n, j: (rows(b, n), gate_blk0 + j), j=j))
        for j in range(4)
    ]
    return pl.pallas_call(
        functools.partial(_swa_kernel, n_kv_heads=n_kv, d_kv=d_kv, blocks=blocks),
        out_shape=jax.ShapeDtypeStruct((t, d_attn), jnp.bfloat16),
        grid=(batch, nbs),
        in_specs=in_specs,
        out_specs=pl.BlockSpec((blocks * w, d_attn), lambda b, n: (rows(b, n), 0)),
        compiler_params=pltpu.CompilerParams(
            dimension_semantics=("parallel", "parallel"), vmem_limit_bytes=VMEM_LIMIT),
        name="swa",
    )(sinks, proj, proj, proj, proj, proj, proj, proj)


def _decay_tables():
    c = CHUNK
    n = np.zeros((N_LEVELS + 1, c, c), np.float32)
    r = np.arange(c)
    for l in range(1, N_LEVELS):
        half = 1 << l
        m = 2 * half
        for t in range(c):
            start = (t // m) * m
            mid = start + half - 1
            if t > mid:
                n[l - 1, t] = (r > mid) & (r <= t)
            else:
                n[l - 1, t] = (r > t) & (r <= mid)
    n[N_LEVELS - 1] = r[None, :] <= r[:, None]
    n[N_LEVELS] = r[None, :] > r[:, None]
    cumsum = np.concatenate([n[N_LEVELS - 1], n[N_LEVELS - 1]], axis=1)
    n = n.reshape((N_LEVELS + 1) * c, c)
    ncat = np.concatenate([n, n], axis=1)
    tt, ss = np.meshgrid(r, r, indexing="ij")
    x = tt ^ ss
    level = np.full((c, c), -1, np.int32)
    for l in range(N_LEVELS):
        level[(ss < tt) & (x >= (1 << l)) & (x < (2 << l))] = l
    return ncat, cumsum, level


def _interleave_rows(k, q, half):
    pieces = []
    for r0 in range(0, k.shape[0], half):
        src = q if (r0 // half) % 2 else k
        pieces.append(src[r0:r0 + half])
    return jnp.concatenate(pieces, axis=0)


def _hgrn2_kernel(ncat_ref, cum_ref, level_ref, gain_ref, worst_ref, q_ref, ghi_ref, glo_ref, v_ref,
                  gate_ref, o_ref, state_ref, *, heads, chunks):
    c = CHUNK
    pw = 2 * LANES
    n_pairs = heads // 2

    @pl.when(pl.program_id(2) == 0)
    def _():
        state_ref[...] = jnp.zeros_like(state_ref)

    mild = jnp.min(worst_ref[...]) > -SAFE_LOG2

    def key_gate(rows, cols):
        g2 = ghi_ref[rows, cols].astype(jnp.float32) + glo_ref[rows, cols].astype(jnp.float32)
        return 1.0 - jnp.exp2(g2)

    def head_matmuls(a_bf, q_op, k_op, v_bf, st):
        intra = jnp.dot(a_bf, v_bf, preferred_element_type=jnp.float32)
        inter = lax.dot_general(q_op, st.astype(jnp.bfloat16), _NT,
                                preferred_element_type=jnp.float32)
        upd = lax.dot_general(v_bf, k_op, _TN, preferred_element_type=jnp.float32)
        return inter + intra, upd

    def head_tail(a_bf, qi, kl, egl, v_bf, idx):
        st = state_ref[idx]
        o, upd = head_matmuls(a_bf, qi, kl, v_bf, st)
        state_ref[idx] = egl * st + upd
        return o

    def write_out(outs, rows, cols):
        normed = []
        for o in outs:
            ms = jnp.mean(o * o, axis=-1, keepdims=True)
            normed.append(o * lax.rsqrt(ms + NORM_EPS))
        o2 = jnp.concatenate(normed, axis=1) * gain_ref[:, cols]
        gate = gate_ref[rows, cols].astype(jnp.float32)
        o_ref[rows, cols] = (o2 * gate).astype(o_ref.dtype)

    @pl.when(mild)
    def _():
        ti = lax.broadcasted_iota(jnp.int32, (c, c), 0)
        si = lax.broadcasted_iota(jnp.int32, (c, c), 1)
        causal = si <= ti
        halves = [slice(hh * LANES, (hh + 1) * LANES) for hh in range(2)]

        def where(item):
            ci, pr = divmod(item, n_pairs)
            return slice(ci * c, (ci + 1) * c), slice(pr * pw, (pr + 1) * pw), pr

        def cumsum_stage(item, _):
            rows, cols, _ = where(item)
            gcat = jnp.concatenate([ghi_ref[rows, cols], glo_ref[rows, cols]], axis=0)
            return jnp.dot(cum_ref[...], gcat, preferred_element_type=jnp.float32)

        def scale_stage(item, gsum):
            rows, cols, _ = where(item)
            ref = gsum[c // 2 - 1:c // 2, :]
            last = gsum[c - 1:c, :]
            decay = jnp.exp2(gsum - ref)
            qp = (q_ref[rows, cols].astype(jnp.float32) * decay).astype(jnp.bfloat16)
            kp = (key_gate(rows, cols) * (1.0 / decay)).astype(jnp.bfloat16)
            a = [lax.dot_general(qp[:, hs], kp[:, hs], _NT, preferred_element_type=jnp.float32)
                 for hs in halves]
            return a, qp, kp, jnp.exp2(ref), jnp.exp2(last - ref)

        def mix_stage(item, carry):
            a, qp, kp, e_ref, e_rest = carry
            rows, cols, pr = where(item)
            v_bf = v_ref[rows, cols]
            heads_out = []
            for hh, hs in enumerate(halves):
                st = state_ref[2 * pr + hh] * e_ref[:, hs]
                o, upd = head_matmuls(jnp.where(causal, a[hh], 0.0).astype(jnp.bfloat16),
                                      qp[:, hs], kp[:, hs], v_bf[:, hs], st)
                heads_out.append((o, st, upd))
            return heads_out, e_rest

        def out_stage(item, carry):
            heads_out, e_rest = carry
            rows, cols, pr = where(item)
            for hh, hs in enumerate(halves):
                _, st, upd = heads_out[hh]
                state_ref[2 * pr + hh] = (st + upd) * e_rest[:, hs]
            write_out([o for o, _, _ in heads_out], rows, cols)

        _skewed(chunks * n_pairs, [cumsum_stage, scale_stage, mix_stage, out_stage],
                gap=min(SKEW_GAP, n_pairs))

    @pl.when(jnp.logical_not(mild))
    def _():
        level = level_ref[...]
        trow = lax.broadcasted_iota(jnp.int32, (c, pw), 0)

        def chunk_body(ci, carry):
            rows = pl.ds(pl.multiple_of(ci * c, c), c)
            for pr in range(n_pairs):
                cols = slice(pr * pw, (pr + 1) * pw)
                q = q_ref[rows, cols].astype(jnp.float32)
                g_hi = ghi_ref[rows, cols]
                g_lo = glo_ref[rows, cols]
                f = jnp.exp2(g_hi.astype(jnp.float32) + g_lo.astype(jnp.float32))
                k = 1.0 - f
                gcat = jnp.concatenate([g_hi, g_lo], axis=0)
                ex = jnp.dot(ncat_ref[...], gcat, preferred_element_type=jnp.float32)

                z = [jnp.where((trow & 1) == 1, q * f, k).astype(jnp.bfloat16)]
                for l in range(1, N_LEVELS):
                    half = 1 << l
                    if half < SUBLANES:
                        qk = jnp.where((trow & half) != 0, q, k)
                    else:
                        qk = _interleave_rows(k, q, half)
                    z.append((qk * jnp.exp2(ex[(l - 1) * c:l * c])).astype(jnp.bfloat16))
                eg = jnp.exp2(ex[(N_LEVELS - 1) * c:N_LEVELS * c])
                erest = jnp.exp2(ex[N_LEVELS * c:(N_LEVELS + 1) * c])
                qi = (q * eg).astype(jnp.bfloat16)
                kl = (k * erest).astype(jnp.bfloat16)
                qk_diag = q * k
                v_bf = v_ref[rows, cols]
                outs = []
                for hh in range(2):
                    hs = slice(hh * LANES, (hh + 1) * LANES)
                    attn = jnp.zeros((c, c), jnp.float32)
                    for l in range(N_LEVELS):
                        zl = z[l][:, hs]
                        a = lax.dot_general(zl, zl, _NT, preferred_element_type=jnp.float32)
                        attn = jnp.where(level == l, a, attn)
                    diag = jnp.sum(qk_diag[:, hs], axis=-1, keepdims=True)
                    o = head_tail(attn.astype(jnp.bfloat16), qi[:, hs], kl[:, hs],
                                  eg[c - 1:c, hs], v_bf[:, hs], 2 * pr + hh)
                    outs.append(o + diag * v_bf[:, hs].astype(jnp.float32))
                write_out(outs, rows, cols)
            return carry

        lax.fori_loop(0, chunks, chunk_body, 0)


def _hgrn2(proj, glo, worst, gain, batch, seq, d_attn, d_rnn, rows, wb=PROJ_TN):
    t = proj.shape[0]
    assert seq % rows == 0 and rows % CHUNK == 0 and wb % (2 * RNN_HEAD_DIM) == 0
    chunks = rows // CHUNK
    heads = wb // RNN_HEAD_DIM
    nl = seq // rows
    n_hg = d_rnn // wb
    kvw = d_attn // 4
    base = d_attn + kvw + d_attn
    assert base % wb == 0
    ncat, cumsum, level = _decay_tables()

    def seg(i):
        blk0 = (base + i * d_rnn) // wb
        return pl.BlockSpec((rows, wb), lambda b, hg, l: (b * nl + l, blk0 + hg))

    const = lambda shape: pl.BlockSpec(shape, lambda b, hg, l: (0, 0))
    own = pl.BlockSpec((rows, wb), lambda b, hg, l: (b * nl + l, hg))
    return pl.pallas_call(
        functools.partial(_hgrn2_kernel, heads=heads, chunks=chunks),
        out_shape=jax.ShapeDtypeStruct((t, d_rnn), jnp.bfloat16),
        grid=(batch, n_hg, nl),
        in_specs=[
            const(ncat.shape), const(cumsum.shape), const(level.shape),
            pl.BlockSpec((1, wb), lambda b, hg, l: (0, hg)),
            pl.BlockSpec((SUBLANES, LANES), lambda b, hg, l: (b * nl + l, hg)),
            seg(0), seg(1), own, seg(2), seg(3),
        ],
        out_specs=own,
        scratch_shapes=[pltpu.VMEM((heads, RNN_HEAD_DIM, RNN_HEAD_DIM), jnp.float32)],
        compiler_params=pltpu.CompilerParams(
            dimension_semantics=("parallel", "parallel", "arbitrary"),
            vmem_limit_bytes=VMEM_LIMIT),
        name="hgrn2",
    )(jnp.asarray(ncat, jnp.bfloat16), jnp.asarray(cumsum, jnp.bfloat16), jnp.asarray(level),
      gain.reshape(1, d_rnn), worst, proj, proj, glo, proj, proj)


def _out_proj_kernel(a1_ref, a2_ref, w_ref, x_ref, g_ref, o_ref, acc_ref, y_ref, ssq_ref,
                     *, nk1, n_tiles):
    r = pl.program_id(0)
    kk = pl.program_id(1)
    nk = pl.num_programs(1)
    d = acc_ref.shape[1]
    te = o_ref.shape[0]

    def residual_chunk():
        rows = pl.ds(pl.multiple_of(kk * te, te), te)
        inv = lax.rsqrt(ssq_ref[rows, :] * (1.0 / d) + NORM_EPS)
        o_ref[...] = x_ref[...] + y_ref[rows, :].astype(jnp.float32) * inv * g_ref[...]

    def accumulate(last):
        a = jnp.where(kk < nk1, a1_ref[...], a2_ref[...])
        ssq = None
        for n0 in range(0, d, OUT_COL_CHUNK):
            cols = slice(n0, n0 + OUT_COL_CHUNK)
            part = jnp.dot(a, w_ref[:, cols], preferred_element_type=jnp.float32)
            y = jnp.where(kk == 0, part, acc_ref[:, cols] + part)
            if last:
                y_ref[:, cols] = y.astype(y_ref.dtype)
                s = jnp.sum(y * y, axis=-1, keepdims=True)
                ssq = s if ssq is None else ssq + s
            else:
                acc_ref[:, cols] = y
        if last:
            ssq_ref[...] = ssq

    first, drain = r == 0, r == n_tiles
    mid = jnp.logical_not(first | drain)
    last_slab = kk == nk - 1

    @pl.when(first & (kk == 0))
    def _():
        acc_ref[...] = jnp.zeros_like(acc_ref)

    @pl.when(first & jnp.logical_not(last_slab))
    def _():
        accumulate(False)

    @pl.when(first & last_slab)
    def _():
        accumulate(True)

    @pl.when(mid & jnp.logical_not(last_slab))
    def _():
        residual_chunk()
        accumulate(False)

    @pl.when(mid & last_slab)
    def _():
        residual_chunk()
        accumulate(True)

    @pl.when(drain)
    def _():
        residual_chunk()


def _out_proj(a1, a2, w, x2d, gain, tm=OUT_TM, tk=OUT_TK):
    t, k1 = a1.shape
    k2 = a2.shape[1]
    d = w.shape[1]
    tm = min(tm, t)
    n_tiles = t // tm
    nk1, nk2 = k1 // tk, k2 // tk
    nk = nk1 + nk2
    te = tm // nk
    assert tm % nk == 0 and te % SUBLANES == 0

    def a_row(r):
        return jnp.minimum(r, n_tiles - 1)

    def residual_block(r, k):
        return (jnp.where(r == 0, 0, (r - 1) * nk + k), 0)

    return pl.pallas_call(
        functools.partial(_out_proj_kernel, nk1=nk1, n_tiles=n_tiles),
        out_shape=jax.ShapeDtypeStruct((t, d), jnp.float32),
        grid=(n_tiles + 1, nk),
        in_specs=[pl.BlockSpec((tm, tk), lambda r, k: (a_row(r), jnp.minimum(k, nk1 - 1))),
                  pl.BlockSpec((tm, tk), lambda r, k: (a_row(r), jnp.clip(k - nk1, 0, nk2 - 1))),
                  pl.BlockSpec((tk, d), lambda r, k: (jnp.where(r == n_tiles, nk - 1, k), 0)),
                  pl.BlockSpec((te, d), residual_block),
                  pl.BlockSpec((1, d), lambda r, k: (0, 0))],
        out_specs=pl.BlockSpec((te, d), residual_block),
        scratch_shapes=[pltpu.VMEM((tm, d), jnp.float32),
                        pltpu.VMEM((tm, d), jnp.bfloat16),
                        pltpu.VMEM((tm, 1), jnp.float32)],
        compiler_params=pltpu.CompilerParams(
            dimension_semantics=("arbitrary", "arbitrary"), vmem_limit_bytes=VMEM_LIMIT),
        name="out_proj",
    )(a1, a2, w, x2d, gain.reshape(1, d))


def kernel(x, w_in, attn_sinks, lb_logits, rnn_norm, w_out, pre_norm, post_norm):
    batch, seq, d_model = x.shape
    depth = w_in.shape[0]
    d_mix = w_out.shape[1]
    d_attn = d_mix // 2
    d_rnn = d_mix - d_attn
    x2d = x.reshape(batch * seq, d_model)
    tm = min(PROJ_TM, seq)
    for layer in range(depth):
        h = _prenorm(x2d, pre_norm[layer])
        proj, glo, worst, w_out_bf = _in_proj(h, w_in[layer], w_out[layer], lb_logits,
                                              d_attn, d_rnn, layer, tm)
        attn = _swa(proj, attn_sinks[layer], batch, seq, d_attn)
        rnn = _hgrn2(proj, glo, worst, rnn_norm[layer], batch, seq, d_attn, d_rnn, rows=tm)
        x2d = _out_proj(attn, rnn, w_out_bf, x2d, post_norm[layer])
    return x2d.reshape(batch, seq, d_model)
```

```python
import functools

import jax
import jax.numpy as jnp
import numpy as np
from jax import lax
from jax.experimental import pallas as pl
from jax.experimental.pallas import tpu as pltpu

ATTN_HEAD_DIM = 64
GQA_GROUP = 8
WINDOW = 128
RNN_HEAD_DIM = 128
NORM_EPS = 1e-6

LANES = 128
SUBLANES = 8
BF16_ROWS = 2 * SUBLANES
CHUNK = 128
N_LEVELS = 7
SAFE_LOG2 = 100.0
LOG2_E = 1.4426950408889634
VMEM_LIMIT = 56 * 1024 * 1024
SWA_BLOCKS = 4
SKEW_GAP = 3
PROJ_TM = 1024
PROJ_TN = 1024
GATE_ROW_CHUNKS = 8
W2_COL_BLOCKS = 16
PRENORM_ROWS = 512
OUT_TM = 1024
OUT_TK = 512
OUT_COL_CHUNK = 1024

_NT = (((1,), (1,)), ((), ()))
_TN = (((0,), (0,)), ((), ()))


def _skewed(n_items, stages, gap=SKEW_GAP, oldest_first=True):
    carry = [None] * n_items
    order = range(len(stages))
    for step in range(n_items + gap * (len(stages) - 1)):
        for j in (reversed(order) if oldest_first else order):
            i = step - gap * j
            if 0 <= i < n_items:
                carry[i] = stages[j](i, carry[i])


def _silu(x):
    h = 0.5 * x
    return h + h * jnp.tanh(h)


def _prenorm_kernel(x_ref, g_ref, o_ref):
    x = x_ref[...]
    ms = jnp.mean(x * x, axis=-1, keepdims=True)
    o_ref[...] = (x * lax.rsqrt(ms + NORM_EPS) * g_ref[...]).astype(o_ref.dtype)


def _prenorm(x2d, gain, rows=PRENORM_ROWS):
    t, d = x2d.shape
    rows = min(rows, t)
    return pl.pallas_call(
        _prenorm_kernel,
        out_shape=jax.ShapeDtypeStruct((t, d), jnp.bfloat16),
        grid=(t // rows,),
        in_specs=[pl.BlockSpec((rows, d), lambda i: (i, 0)),
                  pl.BlockSpec((1, d), lambda i: (0, 0))],
        out_specs=pl.BlockSpec((rows, d), lambda i: (i, 0)),
        compiler_params=pltpu.CompilerParams(
            dimension_semantics=("parallel",), vmem_limit_bytes=VMEM_LIMIT),
        name="prenorm",
    )(x2d, gain.reshape(1, d))


def _in_proj_kernel(h_ref, wchunk_ref, w2_ref, lbl_ref, o_ref, glo_ref, worst_ref, w2o_ref, wbuf_ref,
                    *, silu_tiles, gate_tiles, layer):
    jo = pl.program_id(0)
    i = pl.program_id(1)
    rows = wchunk_ref.shape[0]
    chunk = pl.ds(pl.multiple_of(i * rows, rows), rows)
    tile = jo - 1

    def in_ranges(ranges):
        hits = [(tile >= lo) & (tile < hi) for lo, hi in ranges]
        return functools.reduce(jnp.logical_or, hits)

    is_silu = in_ranges(silu_tiles)
    is_gate = in_ranges(gate_tiles)

    def casts(slot):
        wbuf_ref[slot, chunk, :] = wchunk_ref[...].astype(wbuf_ref.dtype)
        w2o_ref[...] = w2_ref[...].astype(w2o_ref.dtype)

    def casts_and_dot(slot):
        acc = jnp.dot(h_ref[...], wbuf_ref[1 - slot], preferred_element_type=jnp.float32)
        casts(slot)
        return acc

    @pl.when(jo == 0)
    def _():
        casts(0)

    @pl.when((jo > 0) & jnp.logical_not(is_silu | is_gate))
    def _():
        o_ref[...] = casts_and_dot(jo % 2).astype(o_ref.dtype)

    @pl.when(is_silu)
    def _():
        o_ref[...] = _silu(casts_and_dot(jo % 2)).astype(o_ref.dtype)

    @pl.when(is_gate)
    def _():
        lbl = lbl_ref[...]
        e = jnp.exp(lbl - jnp.max(lbl, axis=0, keepdims=True))
        lb = jnp.sum(e[:layer + 1], axis=0, keepdims=True) / jnp.sum(e, axis=0, keepdims=True)
        fa = 0.5 * (1.0 + lb)
        fb = 0.5 * (1.0 - lb)
        slot = jo % 2
        rc = o_ref.shape[0] // GATE_ROW_CHUNKS
        half = CHUNK // 2
        worst = None
        for r in range(GATE_ROW_CHUNKS):
            rs = slice(r * rc, (r + 1) * rc)
            acc = jnp.dot(h_ref[rs, :], wbuf_ref[1 - slot], preferred_element_type=jnp.float32)
            g2 = jnp.log(fa + fb * jnp.tanh(0.5 * acc)) * LOG2_E
            g_hi = g2.astype(o_ref.dtype)
            o_ref[rs, :] = g_hi
            glo_ref[rs, :] = (g2 - g_hi.astype(jnp.float32)).astype(glo_ref.dtype)
            sums = jnp.sum(g2.reshape(rc // half, half, g2.shape[1]), axis=1)
            low = jnp.min(sums, axis=0, keepdims=True)
            worst = low if worst is None else jnp.minimum(worst, low)
        worst_ref[...] = jnp.broadcast_to(jnp.min(worst, axis=1, keepdims=True), worst_ref.shape)
        casts(slot)


def _tile_ranges(col_ranges, tn):
    assert all(lo % tn == 0 and hi % tn == 0 for lo, hi in col_ranges)
    return tuple((lo // tn, hi // tn) for lo, hi in col_ranges)


def _in_proj(h, w, w2, lb_logits, d_attn, d_rnn, layer, tm, tn=PROJ_TN,
             w2_col_blocks=W2_COL_BLOCKS):
    t, d = h.shape
    n = w.shape[1]
    k2, d2 = w2.shape
    ni, nj = t // tm, n // tn
    assert d % ni == 0 and (d // ni) % BF16_ROWS == 0 and k2 % ni == 0 and nj + 1 >= w2_col_blocks
    w2_blk = (k2 // ni, d2 // w2_col_blocks)
    rnn0 = d_attn + d_attn // 4 + d_attn
    assert rnn0 + 4 * d_rnn == n
    silu_tiles = _tile_ranges([(rnn0 - d_attn, rnn0), (rnn0, rnn0 + d_rnn),
                               (rnn0 + 3 * d_rnn, n)], tn)
    gate_tiles = _tile_ranges([(rnn0 + d_rnn, rnn0 + 2 * d_rnn)], tn)
    g0, g1 = gate_tiles[0]

    def w2_index(jo, i):
        done = jo >= w2_col_blocks
        return (jnp.where(done, ni - 1, i), jnp.where(done, w2_col_blocks - 1, jo))

    def glo_index(jo, i):
        tile = jo - 1
        row = jnp.where(tile < g0, 0, jnp.where(tile >= g1, ni - 1, i))
        return (row, jnp.clip(tile - g0, 0, g1 - g0 - 1))

    return pl.pallas_call(
        functools.partial(_in_proj_kernel, silu_tiles=silu_tiles, gate_tiles=gate_tiles,
                          layer=layer),
        out_shape=(jax.ShapeDtypeStruct((t, n), jnp.bfloat16),
                   jax.ShapeDtypeStruct((t, d_rnn), jnp.bfloat16),
                   jax.ShapeDtypeStruct((ni * SUBLANES, (g1 - g0) * LANES), jnp.float32),
                   jax.ShapeDtypeStruct((k2, d2), jnp.bfloat16)),
        grid=(nj + 1, ni),
        in_specs=[pl.BlockSpec((tm, d), lambda jo, i: (jnp.where(jo == 0, 0, i), 0)),
                  pl.BlockSpec((d // ni, tn), lambda jo, i: (i, jnp.minimum(jo, nj - 1))),
                  pl.BlockSpec(w2_blk, w2_index),
                  pl.BlockSpec((lb_logits.shape[0], tn),
                               lambda jo, i: (0, jnp.clip(jo - 1 - g0, 0, g1 - g0 - 1)))],
        out_specs=(pl.BlockSpec((tm, tn),
                                lambda jo, i: (jnp.where(jo == 0, 0, i), jnp.maximum(jo - 1, 0))),
                   pl.BlockSpec((tm, tn), glo_index),
                   pl.BlockSpec((SUBLANES, LANES), glo_index),
                   pl.BlockSpec(w2_blk, w2_index)),
        scratch_shapes=[pltpu.VMEM((2, d, tn), jnp.bfloat16)],
        compiler_params=pltpu.CompilerParams(
            dimension_semantics=("arbitrary", "arbitrary"), vmem_limit_bytes=VMEM_LIMIT),
        name="in_proj",
    )(h, w, w2, lb_logits)


def _lane_halves(slab, head_in_high_half):
    lane = lax.broadcasted_iota(jnp.int32, slab.shape, 1)
    swapped = pltpu.roll(slab, ATTN_HEAD_DIM, axis=1)
    zero = jnp.zeros_like(slab)
    if head_in_high_half:
        lo, hi = swapped, slab
    else:
        lo, hi = slab, swapped
    return (jnp.where(lane < ATTN_HEAD_DIM, lo, zero), jnp.where(lane >= ATTN_HEAD_DIM, hi, zero))


def _swa_kernel(sink_ref, q_ref, kvp_ref, kvc_ref, g0_ref, g1_ref, g2_ref, g3_ref, o_ref,
                *, n_kv_heads, d_kv, blocks):
    n = pl.program_id(1)
    gate_refs = (g0_ref, g1_ref, g2_ref, g3_ref)
    w = WINDOW
    qi = lax.broadcasted_iota(jnp.int32, (w, w), 0)
    kj = lax.broadcasted_iota(jnp.int32, (w, w), 1)
    lower = kj <= qi
    first_bias = jnp.where(n > 0, 0.0, -jnp.inf)
    lane = lax.broadcasted_iota(jnp.int32, (w, LANES), 1)
    scale = ATTN_HEAD_DIM ** -0.5 * LOG2_E
    pairs_per_kv = GQA_GROUP // 2
    n_pairs = n_kv_heads * pairs_per_kv
    pairs_per_gate_ref = n_pairs // len(gate_refs)

    kv_cache = {}

    def kv_operands(j, h):
        if (j, h) not in kv_cache:
            c0 = (h // 2) * LANES
            high = (h % 2) == 1
            cur = slice(j * w, (j + 1) * w)

            def slab(col):
                prev = (kvp_ref[:, col:col + LANES] if j == 0
                        else kvc_ref[(j - 1) * w:j * w, col:col + LANES])
                return jnp.concatenate([prev, kvc_ref[cur, col:col + LANES]],
                                       axis=0).astype(jnp.float32)

            k_lo, k_hi = _lane_halves(slab(c0) * scale, high)
            v_lo, v_hi = _lane_halves(slab(d_kv + c0), high)
            kv_cache[(j, h)] = (jnp.concatenate([k_lo, k_hi], axis=0).astype(jnp.bfloat16),
                                jnp.concatenate([v_lo, v_hi], axis=0).astype(jnp.bfloat16))
        return kv_cache[(j, h)]

    def where(item):
        j, p = divmod(item, n_pairs)
        return j, p, slice(j * w, (j + 1) * w)

    def score_stage(item, _):
        j, p, rows = where(item)
        qp = q_ref[rows, p * LANES:(p + 1) * LANES]
        kcat = kv_operands(j, p // pairs_per_kv)[0]
        return lax.dot_general(qp, kcat, _NT, preferred_element_type=jnp.float32)

    def max_stage(item, s):
        j, p, _ = where(item)
        merged, mx = [], []
        for hh in range(2):
            prev = s[:, hh * 2 * w:hh * 2 * w + w]
            if j == 0:
                prev = prev + first_bias
            cur = s[:, hh * 2 * w + w:(hh + 1) * 2 * w]
            sh = jnp.where(lower, cur, prev)
            merged.append(sh)
            mx.append(jnp.maximum(jnp.max(sh, axis=-1, keepdims=True), sink_ref[2 * p + hh] * LOG2_E))
        return merged, mx

    def exp_stage(item, carry):
        _, p, _ = where(item)
        merged, mx = carry
        probs, denom = [], []
        for hh in range(2):
            e = jnp.exp2(merged[hh] - mx[hh])
            denom.append(jnp.sum(e, axis=-1, keepdims=True)
                         + jnp.exp2(sink_ref[2 * p + hh] * LOG2_E - mx[hh]))
            e = e.astype(jnp.bfloat16)
            zero = jnp.zeros_like(e)
            probs += [jnp.where(lower, zero, e), jnp.where(lower, e, zero)]
        return jnp.concatenate(probs, axis=1), denom

    def value_stage(item, carry):
        j, p, _ = where(item)
        pcat, denom = carry
        vcat = kv_operands(j, p // pairs_per_kv)[1]
        return jnp.dot(pcat, vcat, preferred_element_type=jnp.float32), denom

    def out_stage(item, carry):
        _, p, rows = where(item)
        o, denom = carry
        o = o * jnp.where(lane < ATTN_HEAD_DIM, 1.0 / denom[0], 1.0 / denom[1])
        gi, gp = divmod(p, pairs_per_gate_ref)
        gate = gate_refs[gi][rows, gp * LANES:(gp + 1) * LANES].astype(jnp.float32)
        o_ref[rows, p * LANES:(p + 1) * LANES] = (o * gate).astype(o_ref.dtype)

    _skewed(blocks * n_pairs, [score_stage, max_stage, exp_stage, value_stage, out_stage])


def _swa(proj, sinks, batch, seq, d_attn, blocks=SWA_BLOCKS):
    t = proj.shape[0]
    w = WINDOW
    nb = seq // w
    blocks = min(blocks, nb)
    assert nb % blocks == 0
    nbs = nb // blocks
    n_q = d_attn // ATTN_HEAD_DIM
    n_kv = n_q // GQA_GROUP
    d_kv = n_kv * ATTN_HEAD_DIM
    kvw = 2 * d_kv
    assert d_attn == 4 * kvw
    kv_blk = d_attn // kvw
    gate_blk0 = (d_attn + kvw) // kvw

    def rows(b, n):
        return b * nbs + n

    def prev_block(b, n):
        return b * nb + jnp.maximum(n * blocks - 1, 0)

    in_specs = [
        pl.BlockSpec(memory_space=pltpu.SMEM),
        pl.BlockSpec((blocks * w, d_attn), lambda b, n: (rows(b, n), 0)),
        pl.BlockSpec((w, kvw), lambda b, n: (prev_block(b, n), kv_blk)),
        pl.BlockSpec((blocks * w, kvw), lambda b, n: (rows(b, n), kv_blk)),
    ] + [
        pl.BlockSpec((blocks * w, kvw),
                     functools.partial(lambda b, n, j: (rows(b, n), gate_blk0 + j), j=j))
        for j in range(4)
    ]
    return pl.pallas_call(
        functools.partial(_swa_kernel, n_kv_heads=n_kv, d_kv=d_kv, blocks=blocks),
        out_shape=jax.ShapeDtypeStruct((t, d_attn), jnp.bfloat16),
        grid=(batch, nbs),
        in_specs=in_specs,
        out_specs=pl.BlockSpec((blocks * w, d_attn), lambda b, n: (rows(b, n), 0)),
        compiler_params=pltpu.CompilerParams(
            dimension_semantics=("parallel", "parallel"), vmem_limit_bytes=VMEM_LIMIT),
        name="swa",
    )(sinks, proj, proj, proj, proj, proj, proj, proj)


def _decay_tables():
    c = CHUNK
    n = np.zeros((N_LEVELS + 1, c, c), np.float32)
    r = np.arange(c)
    for l in range(1, N_LEVELS):
        half = 1 << l
        m = 2 * half
        for t in range(c):
            start = (t // m) * m
            mid = start + half - 1
            if t > mid:
                n[l - 1, t] = (r > mid) & (r <= t)
            else:
                n[l - 1, t] = (r > t) & (r <= mid)
    n[N_LEVELS - 1] = r[None, :] <= r[:, None]
    n[N_LEVELS] = r[None, :] > r[:, None]
    cumsum = np.concatenate([n[N_LEVELS - 1], n[N_LEVELS - 1]], axis=1)
    n = n.reshape((N_LEVELS + 1) * c, c)
    ncat = np.concatenate([n, n], axis=1)
    tt, ss = np.meshgrid(r, r, indexing="ij")
    x = tt ^ ss
    level = np.full((c, c), -1, np.int32)
    for l in range(N_LEVELS):
        level[(ss < tt) & (x >= (1 << l)) & (x < (2 << l))] = l
    return ncat, cumsum, level


def _interleave_rows(k, q, half):
    pieces = []
    for r0 in range(0, k.shape[0], half):
        src = q if (r0 // half) % 2 else k
        pieces.append(src[r0:r0 + half])
    return jnp.concatenate(pieces, axis=0)


def _hgrn2_kernel(ncat_ref, cum_ref, level_ref, gain_ref, worst_ref, q_ref, ghi_ref, glo_ref, v_ref,
                  gate_ref, o_ref, state_ref, *, heads, chunks):
    c = CHUNK
    pw = 2 * LANES
    n_pairs = heads // 2

    @pl.when(pl.program_id(2) == 0)
    def _():
        state_ref[...] = jnp.zeros_like(state_ref)

    mild = jnp.min(worst_ref[...]) > -SAFE_LOG2

    def key_gate(rows, cols):
        g2 = ghi_ref[rows, cols].astype(jnp.float32) + glo_ref[rows, cols].astype(jnp.float32)
        return 1.0 - jnp.exp2(g2)

    def head_matmuls(a_bf, q_op, k_op, v_bf, st):
        intra = jnp.dot(a_bf, v_bf, preferred_element_type=jnp.float32)
        inter = lax.dot_general(q_op, st.astype(jnp.bfloat16), _NT,
                                preferred_element_type=jnp.float32)
        upd = lax.dot_general(v_bf, k_op, _TN, preferred_element_type=jnp.float32)
        return inter + intra, upd

    def head_tail(a_bf, qi, kl, egl, v_bf, idx):
        st = state_ref[idx]
        o, upd = head_matmuls(a_bf, qi, kl, v_bf, st)
        state_ref[idx] = egl * st + upd
        return o

    def write_out(outs, rows, cols):
        normed = []
        for o in outs:
            ms = jnp.mean(o * o, axis=-1, keepdims=True)
            normed.append(o * lax.rsqrt(ms + NORM_EPS))
        o2 = jnp.concatenate(normed, axis=1) * gain_ref[:, cols]
        gate = gate_ref[rows, cols].astype(jnp.float32)
        o_ref[rows, cols] = (o2 * gate).astype(o_ref.dtype)

    @pl.when(mild)
    def _():
        ti = lax.broadcasted_iota(jnp.int32, (c, c), 0)
        si = lax.broadcasted_iota(jnp.int32, (c, c), 1)
        causal = si <= ti
        halves = [slice(hh * LANES, (hh + 1) * LANES) for hh in range(2)]

        def where(item):
            ci, pr = divmod(item, n_pairs)
            return slice(ci * c, (ci + 1) * c), slice(pr * pw, (pr + 1) * pw), pr

        def cumsum_stage(item, _):
            rows, cols, _ = where(item)
            gcat = jnp.concatenate([ghi_ref[rows, cols], glo_ref[rows, cols]], axis=0)
            return jnp.dot(cum_ref[...], gcat, preferred_element_type=jnp.float32)

        def scale_stage(item, gsum):
            rows, cols, _ = where(item)
            ref = gsum[c // 2 - 1:c // 2, :]
            last = gsum[c - 1:c, :]
            decay = jnp.exp2(gsum - ref)
            qp = (q_ref[rows, cols].astype(jnp.float32) * decay).astype(jnp.bfloat16)
            kp = (key_gate(rows, cols) * (1.0 / decay)).astype(jnp.bfloat16)
            a = [lax.dot_general(qp[:, hs], kp[:, hs], _NT, preferred_element_type=jnp.float32)
                 for hs in halves]
            return a, qp, kp, jnp.exp2(ref), jnp.exp2(last - ref)

        def mix_stage(item, carry):
            a, qp, kp, e_ref, e_rest = carry
            rows, cols, pr = where(item)
            v_bf = v_ref[rows, cols]
            heads_out = []
            for hh, hs in enumerate(halves):
                st = state_ref[2 * pr + hh] * e_ref[:, hs]
                o, upd = head_matmuls(jnp.where(causal, a[hh], 0.0).astype(jnp.bfloat16),
                                      qp[:, hs], kp[:, hs], v_bf[:, hs], st)
                heads_out.append((o, st, upd))
            return heads_out, e_rest

        def out_stage(item, carry):
            heads_out, e_rest = carry
            rows, cols, pr = where(item)
            for hh, hs in enumerate(halves):
                _, st, upd = heads_out[hh]
                state_ref[2 * pr + hh] = (st + upd) * e_rest[:, hs]
            write_out([o for o, _, _ in heads_out], rows, cols)

        gap = min(SKEW_GAP, n_pairs)
        _skewed(chunks * n_pairs, [cumsum_stage, scale_stage, mix_stage, out_stage],
                gap=gap, oldest_first=n_pairs <= gap)

    @pl.when(jnp.logical_not(mild))
    def _():
        level = level_ref[...]
        trow = lax.broadcasted_iota(jnp.int32, (c, pw), 0)

        def chunk_body(ci, carry):
            rows = pl.ds(pl.multiple_of(ci * c, c), c)
            for pr in range(n_pairs):
                cols = slice(pr * pw, (pr + 1) * pw)
                q = q_ref[rows, cols].astype(jnp.float32)
                g_hi = ghi_ref[rows, cols]
                g_lo = glo_ref[rows, cols]
                f = jnp.exp2(g_hi.astype(jnp.float32) + g_lo.astype(jnp.float32))
                k = 1.0 - f
                gcat = jnp.concatenate([g_hi, g_lo], axis=0)
                ex = jnp.dot(ncat_ref[...], gcat, preferred_element_type=jnp.float32)

                z = [jnp.where((trow & 1) == 1, q * f, k).astype(jnp.bfloat16)]
                for l in range(1, N_LEVELS):
                    half = 1 << l
                    if half < SUBLANES:
                        qk = jnp.where((trow & half) != 0, q, k)
                    else:
                        qk = _interleave_rows(k, q, half)
                    z.append((qk * jnp.exp2(ex[(l - 1) * c:l * c])).astype(jnp.bfloat16))
                eg = jnp.exp2(ex[(N_LEVELS - 1) * c:N_LEVELS * c])
                erest = jnp.exp2(ex[N_LEVELS * c:(N_LEVELS + 1) * c])
                qi = (q * eg).astype(jnp.bfloat16)
                kl = (k * erest).astype(jnp.bfloat16)
                qk_diag = q * k
                v_bf = v_ref[rows, cols]
                outs = []
                for hh in range(2):
                    hs = slice(hh * LANES, (hh + 1) * LANES)
                    attn = jnp.zeros((c, c), jnp.float32)
                    for l in range(N_LEVELS):
                        zl = z[l][:, hs]
                        a = lax.dot_general(zl, zl, _NT, preferred_element_type=jnp.float32)
                        attn = jnp.where(level == l, a, attn)
                    diag = jnp.sum(qk_diag[:, hs], axis=-1, keepdims=True)
                    o = head_tail(attn.astype(jnp.bfloat16), qi[:, hs], kl[:, hs],
                                  eg[c - 1:c, hs], v_bf[:, hs], 2 * pr + hh)
                    outs.append(o + diag * v_bf[:, hs].astype(jnp.float32))
                write_out(outs, rows, cols)
            return carry

        lax.fori_loop(0, chunks, chunk_body, 0)


def _hgrn2(proj, glo, worst, gain, batch, seq, d_attn, d_rnn, rows, wb=PROJ_TN):
    t = proj.shape[0]
    assert seq % rows == 0 and rows % CHUNK == 0 and wb % (2 * RNN_HEAD_DIM) == 0
    chunks = rows // CHUNK
    heads = wb // RNN_HEAD_DIM
    nl = seq // rows
    n_hg = d_rnn // wb
    kvw = d_attn // 4
    base = d_attn + kvw + d_attn
    assert base % wb == 0
    ncat, cumsum, level = _decay_tables()

    def seg(i):
        blk0 = (base + i * d_rnn) // wb
        return pl.BlockSpec((rows, wb), lambda b, hg, l: (b * nl + l, blk0 + hg))

    const = lambda shape: pl.BlockSpec(shape, lambda b, hg, l: (0, 0))
    own = pl.BlockSpec((rows, wb), lambda b, hg, l: (b * nl + l, hg))
    return pl.pallas_call(
        functools.partial(_hgrn2_kernel, heads=heads, chunks=chunks),
        out_shape=jax.ShapeDtypeStruct((t, d_rnn), jnp.bfloat16),
        grid=(batch, n_hg, nl),
        in_specs=[
            const(ncat.shape), const(cumsum.shape), const(level.shape),
            pl.BlockSpec((1, wb), lambda b, hg, l: (0, hg)),
            pl.BlockSpec((SUBLANES, LANES), lambda b, hg, l: (b * nl + l, hg)),
            seg(0), seg(1), own, seg(2), seg(3),
        ],
        out_specs=own,
        scratch_shapes=[pltpu.VMEM((heads, RNN_HEAD_DIM, RNN_HEAD_DIM), jnp.float32)],
        compiler_params=pltpu.CompilerParams(
            dimension_semantics=("parallel", "parallel", "arbitrary"),
            vmem_limit_bytes=VMEM_LIMIT),
        name="hgrn2",
    )(jnp.asarray(ncat, jnp.bfloat16), jnp.asarray(cumsum, jnp.bfloat16), jnp.asarray(level),
      gain.reshape(1, d_rnn), worst, proj, proj, glo, proj, proj)


def _out_proj_kernel(a1_ref, a2_ref, w_ref, x_ref, g_ref, o_ref, acc_ref, y_ref, ssq_ref,
                     *, nk1, n_tiles):
    r = pl.program_id(0)
    kk = pl.program_id(1)
    nk = pl.num_programs(1)
    d = acc_ref.shape[1]
    te = o_ref.shape[0]

    def residual_chunk(cols=slice(None)):
        rows = pl.ds(pl.multiple_of(kk * te, te), te)
        inv = lax.rsqrt(ssq_ref[rows, :] * (1.0 / d) + NORM_EPS)
        o_ref[:, cols] = (x_ref[:, cols]
                          + y_ref[rows, cols].astype(jnp.float32) * inv * g_ref[:, cols])

    def accumulate(last, ride=False):
        a = jnp.where(kk < nk1, a1_ref[...], a2_ref[...])
        ssq = None
        for n0 in range(0, d, OUT_COL_CHUNK):
            cols = slice(n0, n0 + OUT_COL_CHUNK)
            part = jnp.dot(a, w_ref[:, cols], preferred_element_type=jnp.float32)
            y = jnp.where(kk == 0, part, acc_ref[:, cols] + part)
            if ride:
                residual_chunk(cols)
            if last:
                y_ref[:, cols] = y.astype(y_ref.dtype)
                s = jnp.sum(y * y, axis=-1, keepdims=True)
                ssq = s if ssq is None else ssq + s
            else:
                acc_ref[:, cols] = y
        if last:
            ssq_ref[...] = ssq

    first, drain = r == 0, r == n_tiles
    mid = jnp.logical_not(first | drain)
    last_slab = kk == nk - 1

    @pl.when(first & (kk == 0))
    def _():
        acc_ref[...] = jnp.zeros_like(acc_ref)

    @pl.when(first & jnp.logical_not(last_slab))
    def _():
        accumulate(False)

    @pl.when(first & last_slab)
    def _():
        accumulate(True)

    @pl.when(mid & jnp.logical_not(last_slab))
    def _():
        accumulate(False, ride=True)

    @pl.when(mid & last_slab)
    def _():
        accumulate(True, ride=True)

    @pl.when(drain)
    def _():
        residual_chunk()


def _out_proj(a1, a2, w, x2d, gain, tm=OUT_TM, tk=OUT_TK):
    t, k1 = a1.shape
    k2 = a2.shape[1]
    d = w.shape[1]
    tm = min(tm, t)
    n_tiles = t // tm
    nk1, nk2 = k1 // tk, k2 // tk
    nk = nk1 + nk2
    te = tm // nk
    assert tm % nk == 0 and te % SUBLANES == 0

    def a_row(r):
        return jnp.minimum(r, n_tiles - 1)

    def residual_block(r, k):
        return (jnp.where(r == 0, 0, (r - 1) * nk + k), 0)

    return pl.pallas_call(
        functools.partial(_out_proj_kernel, nk1=nk1, n_tiles=n_tiles),
        out_shape=jax.ShapeDtypeStruct((t, d), jnp.float32),
        grid=(n_tiles + 1, nk),
        in_specs=[pl.BlockSpec((tm, tk), lambda r, k: (a_row(r), jnp.minimum(k, nk1 - 1))),
                  pl.BlockSpec((tm, tk), lambda r, k: (a_row(r), jnp.clip(k - nk1, 0, nk2 - 1))),
                  pl.BlockSpec((tk, d), lambda r, k: (jnp.where(r == n_tiles, nk - 1, k), 0)),
                  pl.BlockSpec((te, d), residual_block),
                  pl.BlockSpec((1, d), lambda r, k: (0, 0))],
        out_specs=pl.BlockSpec((te, d), residual_block),
        scratch_shapes=[pltpu.VMEM((tm, d), jnp.float32),
                        pltpu.VMEM((tm, d), jnp.bfloat16),
                        pltpu.VMEM((tm, 1), jnp.float32)],
        compiler_params=pltpu.CompilerParams(
            dimension_semantics=("arbitrary", "arbitrary"), vmem_limit_bytes=VMEM_LIMIT),
        name="out_proj",
    )(a1, a2, w, x2d, gain.reshape(1, d))


def kernel(x, w_in, attn_sinks, lb_logits, rnn_norm, w_out, pre_norm, post_norm):
    batch, seq, d_model = x.shape
    depth = w_in.shape[0]
    d_mix = w_out.shape[1]
    d_attn = d_mix // 2
    d_rnn = d_mix - d_attn
    x2d = x.reshape(batch * seq, d_model)
    tm = min(PROJ_TM, seq)
    for layer in range(depth):
        h = _prenorm(x2d, pre_norm[layer])
        proj, glo, worst, w_out_bf = _in_proj(h, w_in[layer], w_out[layer], lb_logits,
                                              d_attn, d_rnn, layer, tm)
        attn = _swa(proj, attn_sinks[layer], batch, seq, d_attn)
        rnn = _hgrn2(proj, glo, worst, rnn_norm[layer], batch, seq, d_attn, d_rnn, rows=tm)
        x2d = _out_proj(attn, rnn, w_out_bf, x2d, post_norm[layer])
    return x2d.reshape(batch, seq, d_model)
```

```python
import functools

import jax
import jax.numpy as jnp
import numpy as np
from jax import lax
from jax.experimental import pallas as pl
from jax.experimental.pallas import tpu as pltpu

ATTN_HEAD_DIM = 64
GQA_GROUP = 8
WINDOW = 128
RNN_HEAD_DIM = 128
NORM_EPS = 1e-6

LANES = 128
SUBLANES = 8
BF16_ROWS = 2 * SUBLANES
CHUNK = 128
N_LEVELS = 7
SAFE_LOG2 = 100.0
LOG2_E = 1.4426950408889634
VMEM_LIMIT = 56 * 1024 * 1024
SWA_BLOCKS = 4
SKEW_GAP = 3
PROJ_TM = 1024
PROJ_TN = 1024
GATE_ROW_CHUNKS = 8
W2_COL_BLOCKS = 16
PRENORM_ROWS = 512
OUT_TM = 1024
OUT_TK = 512
OUT_COL_CHUNK = 1024

_NT = (((1,), (1,)), ((), ()))
_TN = (((0,), (0,)), ((), ()))


def _skewed(n_items, stages, gap=SKEW_GAP):
    carry = [None] * n_items
    for step in range(n_items + gap * (len(stages) - 1)):
        for j in reversed(range(len(stages))):
            i = step - gap * j
            if 0 <= i < n_items:
                carry[i] = stages[j](i, carry[i])


def _silu(x):
    h = 0.5 * x
    return h + h * jnp.tanh(h)


def _prenorm_kernel(x_ref, g_ref, o_ref):
    x = x_ref[...]
    ms = jnp.mean(x * x, axis=-1, keepdims=True)
    o_ref[...] = (x * lax.rsqrt(ms + NORM_EPS) * g_ref[...]).astype(o_ref.dtype)


def _prenorm(x2d, gain, rows=PRENORM_ROWS):
    t, d = x2d.shape
    rows = min(rows, t)
    return pl.pallas_call(
        _prenorm_kernel,
        out_shape=jax.ShapeDtypeStruct((t, d), jnp.bfloat16),
        grid=(t // rows,),
        in_specs=[pl.BlockSpec((rows, d), lambda i: (i, 0)),
                  pl.BlockSpec((1, d), lambda i: (0, 0))],
        out_specs=pl.BlockSpec((rows, d), lambda i: (i, 0)),
        compiler_params=pltpu.CompilerParams(
            dimension_semantics=("parallel",), vmem_limit_bytes=VMEM_LIMIT),
        name="prenorm",
    )(x2d, gain.reshape(1, d))


def _in_proj_kernel(h_ref, wchunk_ref, w2_ref, lbl_ref, o_ref, glo_ref, worst_ref, w2o_ref, wbuf_ref,
                    *, silu_tiles, gate_tiles, layer):
    jo = pl.program_id(0)
    i = pl.program_id(1)
    rows = wchunk_ref.shape[0]
    chunk = pl.ds(pl.multiple_of(i * rows, rows), rows)
    tile = jo - 1

    def in_ranges(ranges):
        hits = [(tile >= lo) & (tile < hi) for lo, hi in ranges]
        return functools.reduce(jnp.logical_or, hits)

    is_silu = in_ranges(silu_tiles)
    is_gate = in_ranges(gate_tiles)

    def casts(slot):
        wbuf_ref[slot, chunk, :] = wchunk_ref[...].astype(wbuf_ref.dtype)
        w2o_ref[...] = w2_ref[...].astype(w2o_ref.dtype)

    def casts_and_dot(slot):
        acc = jnp.dot(h_ref[...], wbuf_ref[1 - slot], preferred_element_type=jnp.float32)
        casts(slot)
        return acc

    @pl.when(jo == 0)
    def _():
        casts(0)

    @pl.when((jo > 0) & jnp.logical_not(is_silu | is_gate))
    def _():
        o_ref[...] = casts_and_dot(jo % 2).astype(o_ref.dtype)

    @pl.when(is_silu)
    def _():
        o_ref[...] = _silu(casts_and_dot(jo % 2)).astype(o_ref.dtype)

    @pl.when(is_gate)
    def _():
        lbl = lbl_ref[...]
        e = jnp.exp(lbl - jnp.max(lbl, axis=0, keepdims=True))
        lb = jnp.sum(e[:layer + 1], axis=0, keepdims=True) / jnp.sum(e, axis=0, keepdims=True)
        fa = 0.5 * (1.0 + lb)
        fb = 0.5 * (1.0 - lb)
        slot = jo % 2
        rc = o_ref.shape[0] // GATE_ROW_CHUNKS
        half = CHUNK // 2
        worst = None
        for r in range(GATE_ROW_CHUNKS):
            rs = slice(r * rc, (r + 1) * rc)
            acc = jnp.dot(h_ref[rs, :], wbuf_ref[1 - slot], preferred_element_type=jnp.float32)
            g2 = jnp.log(fa + fb * jnp.tanh(0.5 * acc)) * LOG2_E
            g_hi = g2.astype(o_ref.dtype)
            o_ref[rs, :] = g_hi
            glo_ref[rs, :] = (g2 - g_hi.astype(jnp.float32)).astype(glo_ref.dtype)
            sums = jnp.sum(g2.reshape(rc // half, half, g2.shape[1]), axis=1)
            low = jnp.min(sums, axis=0, keepdims=True)
            worst = low if worst is None else jnp.minimum(worst, low)
        worst_ref[...] = jnp.broadcast_to(jnp.min(worst, axis=1, keepdims=True), worst_ref.shape)
        casts(slot)


def _tile_ranges(col_ranges, tn):
    assert all(lo % tn == 0 and hi % tn == 0 for lo, hi in col_ranges)
    return tuple((lo // tn, hi // tn) for lo, hi in col_ranges)


def _in_proj(h, w, w2, lb_logits, d_attn, d_rnn, layer, tm, tn=PROJ_TN,
             w2_col_blocks=W2_COL_BLOCKS):
    t, d = h.shape
    n = w.shape[1]
    k2, d2 = w2.shape
    ni, nj = t // tm, n // tn
    assert d % ni == 0 and (d // ni) % BF16_ROWS == 0 and k2 % ni == 0 and nj + 1 >= w2_col_blocks
    w2_blk = (k2 // ni, d2 // w2_col_blocks)
    rnn0 = d_attn + d_attn // 4 + d_attn
    assert rnn0 + 4 * d_rnn == n
    silu_tiles = _tile_ranges([(rnn0 - d_attn, rnn0), (rnn0, rnn0 + d_rnn),
                               (rnn0 + 3 * d_rnn, n)], tn)
    gate_tiles = _tile_ranges([(rnn0 + d_rnn, rnn0 + 2 * d_rnn)], tn)
    g0, g1 = gate_tiles[0]

    def w2_index(jo, i):
        done = jo >= w2_col_blocks
        return (jnp.where(done, ni - 1, i), jnp.where(done, w2_col_blocks - 1, jo))

    def glo_index(jo, i):
        tile = jo - 1
        row = jnp.where(tile < g0, 0, jnp.where(tile >= g1, ni - 1, i))
        return (row, jnp.clip(tile - g0, 0, g1 - g0 - 1))

    return pl.pallas_call(
        functools.partial(_in_proj_kernel, silu_tiles=silu_tiles, gate_tiles=gate_tiles,
                          layer=layer),
        out_shape=(jax.ShapeDtypeStruct((t, n), jnp.bfloat16),
                   jax.ShapeDtypeStruct((t, d_rnn), jnp.bfloat16),
                   jax.ShapeDtypeStruct((ni * SUBLANES, (g1 - g0) * LANES), jnp.float32),
                   jax.ShapeDtypeStruct((k2, d2), jnp.bfloat16)),
        grid=(nj + 1, ni),
        in_specs=[pl.BlockSpec((tm, d), lambda jo, i: (jnp.where(jo == 0, 0, i), 0)),
                  pl.BlockSpec((d // ni, tn), lambda jo, i: (i, jnp.minimum(jo, nj - 1))),
                  pl.BlockSpec(w2_blk, w2_index),
                  pl.BlockSpec((lb_logits.shape[0], tn),
                               lambda jo, i: (0, jnp.clip(jo - 1 - g0, 0, g1 - g0 - 1)))],
        out_specs=(pl.BlockSpec((tm, tn),
                                lambda jo, i: (jnp.where(jo == 0, 0, i), jnp.maximum(jo - 1, 0))),
                   pl.BlockSpec((tm, tn), glo_index),
                   pl.BlockSpec((SUBLANES, LANES), glo_index),
                   pl.BlockSpec(w2_blk, w2_index)),
        scratch_shapes=[pltpu.VMEM((2, d, tn), jnp.bfloat16)],
        compiler_params=pltpu.CompilerParams(
            dimension_semantics=("arbitrary", "arbitrary"), vmem_limit_bytes=VMEM_LIMIT),
        name="in_proj",
    )(h, w, w2, lb_logits)


def _lane_halves(slab, head_in_high_half):
    lane = lax.broadcasted_iota(jnp.int32, slab.shape, 1)
    swapped = pltpu.roll(slab, ATTN_HEAD_DIM, axis=1)
    zero = jnp.zeros_like(slab)
    if head_in_high_half:
        lo, hi = swapped, slab
    else:
        lo, hi = slab, swapped
    return (jnp.where(lane < ATTN_HEAD_DIM, lo, zero), jnp.where(lane >= ATTN_HEAD_DIM, hi, zero))


def _swa_kernel(sink_ref, q_ref, kvp_ref, kvc_ref, g0_ref, g1_ref, g2_ref, g3_ref, o_ref,
                kc_ref, vc_ref, *, n_kv_heads, d_kv, blocks):
    n = pl.program_id(1)
    gate_refs = (g0_ref, g1_ref, g2_ref, g3_ref)
    w = WINDOW
    qi = lax.broadcasted_iota(jnp.int32, (w, w), 0)
    kj = lax.broadcasted_iota(jnp.int32, (w, w), 1)
    lower = kj <= qi
    first_bias = jnp.where(n > 0, 0.0, -jnp.inf)
    lane = lax.broadcasted_iota(jnp.int32, (w, LANES), 1)
    scale = ATTN_HEAD_DIM ** -0.5 * LOG2_E
    pairs_per_kv = GQA_GROUP // 2
    n_pairs = n_kv_heads * pairs_per_kv
    pairs_per_gate_ref = n_pairs // len(gate_refs)

    def stage_kv(j, h):
        c0 = (h // 2) * LANES
        high = (h % 2) == 1
        cur = slice(j * w, (j + 1) * w)

        def slab(col):
            prev = (kvp_ref[:, col:col + LANES] if j == 0
                    else kvc_ref[(j - 1) * w:j * w, col:col + LANES])
            return jnp.concatenate([prev, kvc_ref[cur, col:col + LANES]],
                                   axis=0).astype(jnp.float32)

        k_lo, k_hi = _lane_halves(slab(c0) * scale, high)
        v_lo, v_hi = _lane_halves(slab(d_kv + c0), high)
        slot = j * n_kv_heads + h
        kc_ref[slot] = jnp.concatenate([k_lo, k_hi], axis=0).astype(kc_ref.dtype)
        vc_ref[slot] = jnp.concatenate([v_lo, v_hi], axis=0).astype(vc_ref.dtype)

    def where(item):
        j, p = divmod(item, n_pairs)
        return j, p, slice(j * w, (j + 1) * w)

    def prep_stage(item, _):
        j, p, _ = where(item)
        if p % pairs_per_kv == 0:
            stage_kv(j, p // pairs_per_kv)

    def score_stage(item, _):
        j, p, rows = where(item)
        qp = q_ref[rows, p * LANES:(p + 1) * LANES]
        kcat = kc_ref[j * n_kv_heads + p // pairs_per_kv]
        return lax.dot_general(qp, kcat, _NT, preferred_element_type=jnp.float32)

    def max_stage(item, s):
        j, p, _ = where(item)
        merged, mx = [], []
        for hh in range(2):
            prev = s[:, hh * 2 * w:hh * 2 * w + w]
            if j == 0:
                prev = prev + first_bias
            cur = s[:, hh * 2 * w + w:(hh + 1) * 2 * w]
            sh = jnp.where(lower, cur, prev)
            merged.append(sh)
            mx.append(jnp.maximum(jnp.max(sh, axis=-1, keepdims=True), sink_ref[2 * p + hh] * LOG2_E))
        return merged, mx

    def exp_stage(item, carry):
        _, p, _ = where(item)
        merged, mx = carry
        probs, denom = [], []
        for hh in range(2):
            e = jnp.exp2(merged[hh] - mx[hh])
            denom.append(jnp.sum(e, axis=-1, keepdims=True)
                         + jnp.exp2(sink_ref[2 * p + hh] * LOG2_E - mx[hh]))
            e = e.astype(jnp.bfloat16)
            zero = jnp.zeros_like(e)
            probs += [jnp.where(lower, zero, e), jnp.where(lower, e, zero)]
        return jnp.concatenate(probs, axis=1), denom

    def value_stage(item, carry):
        j, p, _ = where(item)
        pcat, denom = carry
        vcat = vc_ref[j * n_kv_heads + p // pairs_per_kv]
        return jnp.dot(pcat, vcat, preferred_element_type=jnp.float32), denom

    def out_stage(item, carry):
        _, p, rows = where(item)
        o, denom = carry
        o = o * jnp.where(lane < ATTN_HEAD_DIM, 1.0 / denom[0], 1.0 / denom[1])
        gi, gp = divmod(p, pairs_per_gate_ref)
        gate = gate_refs[gi][rows, gp * LANES:(gp + 1) * LANES].astype(jnp.float32)
        o_ref[rows, p * LANES:(p + 1) * LANES] = (o * gate).astype(o_ref.dtype)

    _skewed(blocks * n_pairs,
            [prep_stage, score_stage, max_stage, exp_stage, value_stage, out_stage])


def _swa(proj, sinks, batch, seq, d_attn, blocks=SWA_BLOCKS):
    t = proj.shape[0]
    w = WINDOW
    nb = seq // w
    blocks = min(blocks, nb)
    assert nb % blocks == 0
    nbs = nb // blocks
    n_q = d_attn // ATTN_HEAD_DIM
    n_kv = n_q // GQA_GROUP
    d_kv = n_kv * ATTN_HEAD_DIM
    kvw = 2 * d_kv
    assert d_attn == 4 * kvw
    kv_blk = d_attn // kvw
    gate_blk0 = (d_attn + kvw) // kvw

    def rows(b, n):
        return b * nbs + n

    def prev_block(b, n):
        return b * nb + jnp.maximum(n * blocks - 1, 0)

    in_specs = [
        pl.BlockSpec(memory_space=pltpu.SMEM),
        pl.BlockSpec((blocks * w, d_attn), lambda b, n: (rows(b, n), 0)),
        pl.BlockSpec((w, kvw), lambda b, n: (prev_block(b, n), kv_blk)),
        pl.BlockSpec((blocks * w, kvw), lambda b, n: (rows(b, n), kv_blk)),
    ] + [
        pl.BlockSpec((blocks * w, kvw),
                     functools.partial(lambda b, n, j: (rows(b, n), gate_blk0 + j), j=j))
        for j in range(4)
    ]
    return pl.pallas_call(
        functools.partial(_swa_kernel, n_kv_heads=n_kv, d_kv=d_kv, blocks=blocks),
        out_shape=jax.ShapeDtypeStruct((t, d_attn), jnp.bfloat16),
        grid=(batch, nbs),
        in_specs=in_specs,
        out_specs=pl.BlockSpec((blocks * w, d_attn), lambda b, n: (rows(b, n), 0)),
        scratch_shapes=[pltpu.VMEM((blocks * n_kv, 4 * w, LANES), jnp.bfloat16),
                        pltpu.VMEM((blocks * n_kv, 4 * w, LANES), jnp.bfloat16)],
        compiler_params=pltpu.CompilerParams(
            dimension_semantics=("parallel", "parallel"), vmem_limit_bytes=VMEM_LIMIT),
        name="swa",
    )(sinks, proj, proj, proj, proj, proj, proj, proj)


def _decay_tables():
    c = CHUNK
    n = np.zeros((N_LEVELS + 1, c, c), np.float32)
    r = np.arange(c)
    for l in range(1, N_LEVELS):
        half = 1 << l
        m = 2 * half
        for t in range(c):
            start = (t // m) * m
            mid = start + half - 1
            if t > mid:
                n[l - 1, t] = (r > mid) & (r <= t)
            else:
                n[l - 1, t] = (r > t) & (r <= mid)
    n[N_LEVELS - 1] = r[None, :] <= r[:, None]
    n[N_LEVELS] = r[None, :] > r[:, None]
    cumsum = np.concatenate([n[N_LEVELS - 1], n[N_LEVELS - 1]], axis=1)
    n = n.reshape((N_LEVELS + 1) * c, c)
    ncat = np.concatenate([n, n], axis=1)
    tt, ss = np.meshgrid(r, r, indexing="ij")
    x = tt ^ ss
    level = np.full((c, c), -1, np.int32)
    for l in range(N_LEVELS):
        level[(ss < tt) & (x >= (1 << l)) & (x < (2 << l))] = l
    return ncat, cumsum, level


def _interleave_rows(k, q, half):
    pieces = []
    for r0 in range(0, k.shape[0], half):
        src = q if (r0 // half) % 2 else k
        pieces.append(src[r0:r0 + half])
    return jnp.concatenate(pieces, axis=0)


def _hgrn2_kernel(ncat_ref, cum_ref, level_ref, gain_ref, worst_ref, q_ref, ghi_ref, glo_ref, v_ref,
                  gate_ref, o_ref, state_ref, *, heads, chunks):
    c = CHUNK
    pw = 2 * LANES
    n_pairs = heads // 2

    @pl.when(pl.program_id(2) == 0)
    def _():
        state_ref[...] = jnp.zeros_like(state_ref)

    mild = jnp.min(worst_ref[...]) > -SAFE_LOG2

    def key_gate(rows, cols):
        g2 = ghi_ref[rows, cols].astype(jnp.float32) + glo_ref[rows, cols].astype(jnp.float32)
        return 1.0 - jnp.exp2(g2)

    def head_matmuls(a_bf, q_op, k_op, v_bf, st):
        intra = jnp.dot(a_bf, v_bf, preferred_element_type=jnp.float32)
        inter = lax.dot_general(q_op, st.astype(jnp.bfloat16), _NT,
                                preferred_element_type=jnp.float32)
        upd = lax.dot_general(v_bf, k_op, _TN, preferred_element_type=jnp.float32)
        return inter + intra, upd

    def head_tail(a_bf, qi, kl, egl, v_bf, idx):
        st = state_ref[idx]
        o, upd = head_matmuls(a_bf, qi, kl, v_bf, st)
        state_ref[idx] = egl * st + upd
        return o

    def write_out(outs, rows, cols):
        normed = []
        for o in outs:
            ms = jnp.mean(o * o, axis=-1, keepdims=True)
            normed.append(o * lax.rsqrt(ms + NORM_EPS))
        o2 = jnp.concatenate(normed, axis=1) * gain_ref[:, cols]
        gate = gate_ref[rows, cols].astype(jnp.float32)
        o_ref[rows, cols] = (o2 * gate).astype(o_ref.dtype)

    @pl.when(mild)
    def _():
        ti = lax.broadcasted_iota(jnp.int32, (c, c), 0)
        si = lax.broadcasted_iota(jnp.int32, (c, c), 1)
        causal = si <= ti
        halves = [slice(hh * LANES, (hh + 1) * LANES) for hh in range(2)]

        def where(item):
            ci, pr = divmod(item, n_pairs)
            return slice(ci * c, (ci + 1) * c), slice(pr * pw, (pr + 1) * pw), pr

        def cumsum_stage(item, _):
            rows, cols, _ = where(item)
            gcat = jnp.concatenate([ghi_ref[rows, cols], glo_ref[rows, cols]], axis=0)
            return jnp.dot(cum_ref[...], gcat, preferred_element_type=jnp.float32)

        def scale_stage(item, gsum):
            rows, cols, _ = where(item)
            ref = gsum[c // 2 - 1:c // 2, :]
            last = gsum[c - 1:c, :]
            decay = jnp.exp2(gsum - ref)
            qp = (q_ref[rows, cols].astype(jnp.float32) * decay).astype(jnp.bfloat16)
            kp = (key_gate(rows, cols) * (1.0 / decay)).astype(jnp.bfloat16)
            a = [lax.dot_general(qp[:, hs], kp[:, hs], _NT, preferred_element_type=jnp.float32)
                 for hs in halves]
            return a, qp, kp, jnp.exp2(ref), jnp.exp2(last - ref)

        def mix_stage(item, carry):
            a, qp, kp, e_ref, e_rest = carry
            rows, cols, pr = where(item)
            v_bf = v_ref[rows, cols]
            heads_out = []
            for hh, hs in enumerate(halves):
                st = state_ref[2 * pr + hh] * e_ref[:, hs]
                o, upd = head_matmuls(jnp.where(causal, a[hh], 0.0).astype(jnp.bfloat16),
                                      qp[:, hs], kp[:, hs], v_bf[:, hs], st)
                heads_out.append((o, st, upd))
            return heads_out, e_rest

        def out_stage(item, carry):
            heads_out, e_rest = carry
            rows, cols, pr = where(item)
            for hh, hs in enumerate(halves):
                _, st, upd = heads_out[hh]
                state_ref[2 * pr + hh] = (st + upd) * e_rest[:, hs]
            write_out([o for o, _, _ in heads_out], rows, cols)

        _skewed(chunks * n_pairs, [cumsum_stage, scale_stage, mix_stage, out_stage],
                gap=min(SKEW_GAP, n_pairs))

    @pl.when(jnp.logical_not(mild))
    def _():
        level = level_ref[...]
        trow = lax.broadcasted_iota(jnp.int32, (c, pw), 0)

        def chunk_body(ci, carry):
            rows = pl.ds(pl.multiple_of(ci * c, c), c)
            for pr in range(n_pairs):
                cols = slice(pr * pw, (pr + 1) * pw)
                q = q_ref[rows, cols].astype(jnp.float32)
                g_hi = ghi_ref[rows, cols]
                g_lo = glo_ref[rows, cols]
                f = jnp.exp2(g_hi.astype(jnp.float32) + g_lo.astype(jnp.float32))
                k = 1.0 - f
                gcat = jnp.concatenate([g_hi, g_lo], axis=0)
                ex = jnp.dot(ncat_ref[...], gcat, preferred_element_type=jnp.float32)

                z = [jnp.where((trow & 1) == 1, q * f, k).astype(jnp.bfloat16)]
                for l in range(1, N_LEVELS):
                    half = 1 << l
                    if half < SUBLANES:
                        qk = jnp.where((trow & half) != 0, q, k)
                    else:
                        qk = _interleave_rows(k, q, half)
                    z.append((qk * jnp.exp2(ex[(l - 1) * c:l * c])).astype(jnp.bfloat16))
                eg = jnp.exp2(ex[(N_LEVELS - 1) * c:N_LEVELS * c])
                erest = jnp.exp2(ex[N_LEVELS * c:(N_LEVELS + 1) * c])
                qi = (q * eg).astype(jnp.bfloat16)
                kl = (k * erest).astype(jnp.bfloat16)
                qk_diag = q * k
                v_bf = v_ref[rows, cols]
                outs = []
                for hh in range(2):
                    hs = slice(hh * LANES, (hh + 1) * LANES)
                    attn = jnp.zeros((c, c), jnp.float32)
                    for l in range(N_LEVELS):
                        zl = z[l][:, hs]
                        a = lax.dot_general(zl, zl, _NT, preferred_element_type=jnp.float32)
                        attn = jnp.where(level == l, a, attn)
                    diag = jnp.sum(qk_diag[:, hs], axis=-1, keepdims=True)
                    o = head_tail(attn.astype(jnp.bfloat16), qi[:, hs], kl[:, hs],
                                  eg[c - 1:c, hs], v_bf[:, hs], 2 * pr + hh)
                    outs.append(o + diag * v_bf[:, hs].astype(jnp.float32))
                write_out(outs, rows, cols)
            return carry

        lax.fori_loop(0, chunks, chunk_body, 0)


def _hgrn2(proj, glo, worst, gain, batch, seq, d_attn, d_rnn, rows, wb=PROJ_TN):
    t = proj.shape[0]
    assert seq % rows == 0 and rows % CHUNK == 0 and wb % (2 * RNN_HEAD_DIM) == 0
    chunks = rows // CHUNK
    heads = wb // RNN_HEAD_DIM
    nl = seq // rows
    n_hg = d_rnn // wb
    kvw = d_attn // 4
    base = d_attn + kvw + d_attn
    assert base % wb == 0
    ncat, cumsum, level = _decay_tables()

    def seg(i):
        blk0 = (base + i * d_rnn) // wb
        return pl.BlockSpec((rows, wb), lambda b, hg, l: (b * nl + l, blk0 + hg))

    const = lambda shape: pl.BlockSpec(shape, lambda b, hg, l: (0, 0))
    own = pl.BlockSpec((rows, wb), lambda b, hg, l: (b * nl + l, hg))
    return pl.pallas_call(
        functools.partial(_hgrn2_kernel, heads=heads, chunks=chunks),
        out_shape=jax.ShapeDtypeStruct((t, d_rnn), jnp.bfloat16),
        grid=(batch, n_hg, nl),
        in_specs=[
            const(ncat.shape), const(cumsum.shape), const(level.shape),
            pl.BlockSpec((1, wb), lambda b, hg, l: (0, hg)),
            pl.BlockSpec((SUBLANES, LANES), lambda b, hg, l: (b * nl + l, hg)),
            seg(0), seg(1), own, seg(2), seg(3),
        ],
        out_specs=own,
        scratch_shapes=[pltpu.VMEM((heads, RNN_HEAD_DIM, RNN_HEAD_DIM), jnp.float32)],
        compiler_params=pltpu.CompilerParams(
            dimension_semantics=("parallel", "parallel", "arbitrary"),
            vmem_limit_bytes=VMEM_LIMIT),
        name="hgrn2",
    )(jnp.asarray(ncat, jnp.bfloat16), jnp.asarray(cumsum, jnp.bfloat16), jnp.asarray(level),
      gain.reshape(1, d_rnn), worst, proj, proj, glo, proj, proj)


def _out_proj_kernel(a1_ref, a2_ref, w_ref, x_ref, g_ref, o_ref, acc_ref, y_ref, ssq_ref,
                     *, nk1, n_tiles):
    r = pl.program_id(0)
    kk = pl.program_id(1)
    nk = pl.num_programs(1)
    d = acc_ref.shape[1]
    te = o_ref.shape[0]

    def residual_chunk():
        rows = pl.ds(pl.multiple_of(kk * te, te), te)
        inv = lax.rsqrt(ssq_ref[rows, :] * (1.0 / d) + NORM_EPS)
        o_ref[...] = x_ref[...] + y_ref[rows, :].astype(jnp.float32) * inv * g_ref[...]

    def accumulate(last):
        a = jnp.where(kk < nk1, a1_ref[...], a2_ref[...])
        ssq = None
        for n0 in range(0, d, OUT_COL_CHUNK):
            cols = slice(n0, n0 + OUT_COL_CHUNK)
            part = jnp.dot(a, w_ref[:, cols], preferred_element_type=jnp.float32)
            y = jnp.where(kk == 0, part, acc_ref[:, cols] + part)
            if last:
                y_ref[:, cols] = y.astype(y_ref.dtype)
                s = jnp.sum(y * y, axis=-1, keepdims=True)
                ssq = s if ssq is None else ssq + s
            else:
                acc_ref[:, cols] = y
        if last:
            ssq_ref[...] = ssq

    first, drain = r == 0, r == n_tiles
    mid = jnp.logical_not(first | drain)
    last_slab = kk == nk - 1

    @pl.when(first & (kk == 0))
    def _():
        acc_ref[...] = jnp.zeros_like(acc_ref)

    @pl.when(first & jnp.logical_not(last_slab))
    def _():
        accumulate(False)

    @pl.when(first & last_slab)
    def _():
        accumulate(True)

    @pl.when(mid & jnp.logical_not(last_slab))
    def _():
        residual_chunk()
        accumulate(False)

    @pl.when(mid & last_slab)
    def _():
        residual_chunk()
        accumulate(True)

    @pl.when(drain)
    def _():
        residual_chunk()


def _out_proj(a1, a2, w, x2d, gain, tm=OUT_TM, tk=OUT_TK):
    t, k1 = a1.shape
    k2 = a2.shape[1]
    d = w.shape[1]
    tm = min(tm, t)
    n_tiles = t // tm
    nk1, nk2 = k1 // tk, k2 // tk
    nk = nk1 + nk2
    te = tm // nk
    assert tm % nk == 0 and te % SUBLANES == 0

    def a_row(r):
        return jnp.minimum(r, n_tiles - 1)

    def residual_block(r, k):
        return (jnp.where(r == 0, 0, (r - 1) * nk + k), 0)

    return pl.pallas_call(
        functools.partial(_out_proj_kernel, nk1=nk1, n_tiles=n_tiles),
        out_shape=jax.ShapeDtypeStruct((t, d), jnp.float32),
        grid=(n_tiles + 1, nk),
        in_specs=[pl.BlockSpec((tm, tk), lambda r, k: (a_row(r), jnp.minimum(k, nk1 - 1))),
                  pl.BlockSpec((tm, tk), lambda r, k: (a_row(r), jnp.clip(k - nk1, 0, nk2 - 1))),
                  pl.BlockSpec((tk, d), lambda r, k: (jnp.where(r == n_tiles, nk - 1, k), 0)),
                  pl.BlockSpec((te, d), residual_block),
                  pl.BlockSpec((1, d), lambda r, k: (0, 0))],
        out_specs=pl.BlockSpec((te, d), residual_block),
        scratch_shapes=[pltpu.VMEM((tm, d), jnp.float32),
                        pltpu.VMEM((tm, d), jnp.bfloat16),
                        pltpu.VMEM((tm, 1), jnp.float32)],
        compiler_params=pltpu.CompilerParams(
            dimension_semantics=("arbitrary", "arbitrary"), vmem_limit_bytes=VMEM_LIMIT),
        name="out_proj",
    )(a1, a2, w, x2d, gain.reshape(1, d))


def kernel(x, w_in, attn_sinks, lb_logits, rnn_norm, w_out, pre_norm, post_norm):
    batch, seq, d_model = x.shape
    depth = w_in.shape[0]
    d_mix = w_out.shape[1]
    d_attn = d_mix // 2
    d_rnn = d_mix - d_attn
    x2d = x.reshape(batch * seq, d_model)
    tm = min(PROJ_TM, seq)
    for layer in range(depth):
        h = _prenorm(x2d, pre_norm[layer])
        proj, glo, worst, w_out_bf = _in_proj(h, w_in[layer], w_out[layer], lb_logits,
                                              d_attn, d_rnn, layer, tm)
        attn = _swa(proj, attn_sinks[layer], batch, seq, d_attn)
        rnn = _hgrn2(proj, glo, worst, rnn_norm[layer], batch, seq, d_attn, d_rnn, rows=tm)
        x2d = _out_proj(attn, rnn, w_out_bf, x2d, post_norm[layer])
    return x2d.reshape(batch, seq, d_model)
```

```python
import functools

import jax
import jax.numpy as jnp
import numpy as np
from jax import lax
from jax.experimental import pallas as pl
from jax.experimental.pallas import tpu as pltpu

ATTN_HEAD_DIM = 64
GQA_GROUP = 8
WINDOW = 128
RNN_HEAD_DIM = 128
NORM_EPS = 1e-6

LANES = 128
SUBLANES = 8
BF16_ROWS = 2 * SUBLANES
CHUNK = 128
N_LEVELS = 7
SAFE_LOG2 = 100.0
LOG2_E = 1.4426950408889634
VMEM_LIMIT = 56 * 1024 * 1024
SWA_BLOCKS = 4
SKEW_GAP = 3
PROJ_TM = 1024
PROJ_TN = 1024
GATE_ROW_CHUNKS = 8
W2_COL_BLOCKS = 16
PRENORM_ROWS = 512
OUT_TM = 1024
OUT_TK = 512
OUT_COL_CHUNK = 1024

_NT = (((1,), (1,)), ((), ()))
_TN = (((0,), (0,)), ((), ()))


def _skewed(n_items, stages, gap=SKEW_GAP, oldest_first=True):
    carry = [None] * n_items
    order = range(len(stages))
    for step in range(n_items + gap * (len(stages) - 1)):
        for j in (reversed(order) if oldest_first else order):
            i = step - gap * j
            if 0 <= i < n_items:
                carry[i] = stages[j](i, carry[i])


def _silu(x):
    h = 0.5 * x
    return h + h * jnp.tanh(h)


def _prenorm_kernel(x_ref, g_ref, o_ref):
    x = x_ref[...]
    ms = jnp.mean(x * x, axis=-1, keepdims=True)
    o_ref[...] = (x * lax.rsqrt(ms + NORM_EPS) * g_ref[...]).astype(o_ref.dtype)


def _prenorm(x2d, gain, rows=PRENORM_ROWS):
    t, d = x2d.shape
    rows = min(rows, t)
    return pl.pallas_call(
        _prenorm_kernel,
        out_shape=jax.ShapeDtypeStruct((t, d), jnp.bfloat16),
        grid=(t // rows,),
        in_specs=[pl.BlockSpec((rows, d), lambda i: (i, 0)),
                  pl.BlockSpec((1, d), lambda i: (0, 0))],
        out_specs=pl.BlockSpec((rows, d), lambda i: (i, 0)),
        compiler_params=pltpu.CompilerParams(
            dimension_semantics=("parallel",), vmem_limit_bytes=VMEM_LIMIT),
        name="prenorm",
    )(x2d, gain.reshape(1, d))


def _in_proj_kernel(h_ref, wchunk_ref, w2_ref, lbl_ref, o_ref, glo_ref, worst_ref, w2o_ref, wbuf_ref,
                    *, silu_tiles, gate_tiles, layer):
    jo = pl.program_id(0)
    i = pl.program_id(1)
    rows = wchunk_ref.shape[0]
    chunk = pl.ds(pl.multiple_of(i * rows, rows), rows)
    tile = jo - 1

    def in_ranges(ranges):
        hits = [(tile >= lo) & (tile < hi) for lo, hi in ranges]
        return functools.reduce(jnp.logical_or, hits)

    is_silu = in_ranges(silu_tiles)
    is_gate = in_ranges(gate_tiles)

    def casts(slot):
        wbuf_ref[slot, chunk, :] = wchunk_ref[...].astype(wbuf_ref.dtype)
        w2o_ref[...] = w2_ref[...].astype(w2o_ref.dtype)

    def casts_and_dot(slot):
        acc = jnp.dot(h_ref[...], wbuf_ref[1 - slot], preferred_element_type=jnp.float32)
        casts(slot)
        return acc

    @pl.when(jo == 0)
    def _():
        casts(0)

    @pl.when((jo > 0) & jnp.logical_not(is_silu | is_gate))
    def _():
        o_ref[...] = casts_and_dot(jo % 2).astype(o_ref.dtype)

    @pl.when(is_silu)
    def _():
        o_ref[...] = _silu(casts_and_dot(jo % 2)).astype(o_ref.dtype)

    @pl.when(is_gate)
    def _():
        lbl = lbl_ref[...]
        e = jnp.exp(lbl - jnp.max(lbl, axis=0, keepdims=True))
        lb = jnp.sum(e[:layer + 1], axis=0, keepdims=True) / jnp.sum(e, axis=0, keepdims=True)
        fa = 0.5 * (1.0 + lb)
        fb = 0.5 * (1.0 - lb)
        slot = jo % 2
        rc = o_ref.shape[0] // GATE_ROW_CHUNKS
        half = CHUNK // 2
        worst = None
        for r in range(GATE_ROW_CHUNKS):
            rs = slice(r * rc, (r + 1) * rc)
            acc = jnp.dot(h_ref[rs, :], wbuf_ref[1 - slot], preferred_element_type=jnp.float32)
            g2 = jnp.log(fa + fb * jnp.tanh(0.5 * acc)) * LOG2_E
            g_hi = g2.astype(o_ref.dtype)
            o_ref[rs, :] = g_hi
            glo_ref[rs, :] = (g2 - g_hi.astype(jnp.float32)).astype(glo_ref.dtype)
            sums = jnp.sum(g2.reshape(rc // half, half, g2.shape[1]), axis=1)
            low = jnp.min(sums, axis=0, keepdims=True)
            worst = low if worst is None else jnp.minimum(worst, low)
        worst_ref[...] = jnp.broadcast_to(jnp.min(worst, axis=1, keepdims=True), worst_ref.shape)
        casts(slot)


def _tile_ranges(col_ranges, tn):
    assert all(lo % tn == 0 and hi % tn == 0 for lo, hi in col_ranges)
    return tuple((lo // tn, hi // tn) for lo, hi in col_ranges)


def _in_proj(h, w, w2, lb_logits, d_attn, d_rnn, layer, tm, tn=PROJ_TN,
             w2_col_blocks=W2_COL_BLOCKS):
    t, d = h.shape
    n = w.shape[1]
    k2, d2 = w2.shape
    ni, nj = t // tm, n // tn
    assert d % ni == 0 and (d // ni) % BF16_ROWS == 0 and k2 % ni == 0 and nj + 1 >= w2_col_blocks
    w2_blk = (k2 // ni, d2 // w2_col_blocks)
    rnn0 = d_attn + d_attn // 4 + d_attn
    assert rnn0 + 4 * d_rnn == n
    silu_tiles = _tile_ranges([(rnn0 - d_attn, rnn0), (rnn0, rnn0 + d_rnn),
                               (rnn0 + 3 * d_rnn, n)], tn)
    gate_tiles = _tile_ranges([(rnn0 + d_rnn, rnn0 + 2 * d_rnn)], tn)
    g0, g1 = gate_tiles[0]

    def w2_index(jo, i):
        done = jo >= w2_col_blocks
        return (jnp.where(done, ni - 1, i), jnp.where(done, w2_col_blocks - 1, jo))

    def glo_index(jo, i):
        tile = jo - 1
        row = jnp.where(tile < g0, 0, jnp.where(tile >= g1, ni - 1, i))
        return (row, jnp.clip(tile - g0, 0, g1 - g0 - 1))

    return pl.pallas_call(
        functools.partial(_in_proj_kernel, silu_tiles=silu_tiles, gate_tiles=gate_tiles,
                          layer=layer),
        out_shape=(jax.ShapeDtypeStruct((t, n), jnp.bfloat16),
                   jax.ShapeDtypeStruct((t, d_rnn), jnp.bfloat16),
                   jax.ShapeDtypeStruct((ni * SUBLANES, (g1 - g0) * LANES), jnp.float32),
                   jax.ShapeDtypeStruct((k2, d2), jnp.bfloat16)),
        grid=(nj + 1, ni),
        in_specs=[pl.BlockSpec((tm, d), lambda jo, i: (jnp.where(jo == 0, 0, i), 0)),
                  pl.BlockSpec((d // ni, tn), lambda jo, i: (i, jnp.minimum(jo, nj - 1))),
                  pl.BlockSpec(w2_blk, w2_index),
                  pl.BlockSpec((lb_logits.shape[0], tn),
                               lambda jo, i: (0, jnp.clip(jo - 1 - g0, 0, g1 - g0 - 1)))],
        out_specs=(pl.BlockSpec((tm, tn),
                                lambda jo, i: (jnp.where(jo == 0, 0, i), jnp.maximum(jo - 1, 0))),
                   pl.BlockSpec((tm, tn), glo_index),
                   pl.BlockSpec((SUBLANES, LANES), glo_index),
                   pl.BlockSpec(w2_blk, w2_index)),
        scratch_shapes=[pltpu.VMEM((2, d, tn), jnp.bfloat16)],
        compiler_params=pltpu.CompilerParams(
            dimension_semantics=("arbitrary", "arbitrary"), vmem_limit_bytes=VMEM_LIMIT),
        name="in_proj",
    )(h, w, w2, lb_logits)


def _lane_halves(slab, head_in_high_half):
    lane = lax.broadcasted_iota(jnp.int32, slab.shape, 1)
    swapped = pltpu.roll(slab, ATTN_HEAD_DIM, axis=1)
    zero = jnp.zeros_like(slab)
    if head_in_high_half:
        lo, hi = swapped, slab
    else:
        lo, hi = slab, swapped
    return (jnp.where(lane < ATTN_HEAD_DIM, lo, zero), jnp.where(lane >= ATTN_HEAD_DIM, hi, zero))


def _swa_kernel(sink_ref, q_ref, kvp_ref, kvc_ref, g0_ref, g1_ref, g2_ref, g3_ref, o_ref,
                *, n_kv_heads, d_kv, blocks):
    n = pl.program_id(1)
    gate_refs = (g0_ref, g1_ref, g2_ref, g3_ref)
    w = WINDOW
    qi = lax.broadcasted_iota(jnp.int32, (w, w), 0)
    kj = lax.broadcasted_iota(jnp.int32, (w, w), 1)
    lower = kj <= qi
    first_bias = jnp.where(n > 0, 0.0, -jnp.inf)
    lane = lax.broadcasted_iota(jnp.int32, (w, LANES), 1)
    scale = ATTN_HEAD_DIM ** -0.5 * LOG2_E
    pairs_per_kv = GQA_GROUP // 2
    n_pairs = n_kv_heads * pairs_per_kv
    pairs_per_gate_ref = n_pairs // len(gate_refs)

    kv_cache = {}

    def kv_operands(j, h):
        if (j, h) not in kv_cache:
            c0 = (h // 2) * LANES
            high = (h % 2) == 1
            cur = slice(j * w, (j + 1) * w)

            def slab(col):
                prev = (kvp_ref[:, col:col + LANES] if j == 0
                        else kvc_ref[(j - 1) * w:j * w, col:col + LANES])
                return jnp.concatenate([prev, kvc_ref[cur, col:col + LANES]],
                                       axis=0).astype(jnp.float32)

            k_lo, k_hi = _lane_halves(slab(c0) * scale, high)
            v_lo, v_hi = _lane_halves(slab(d_kv + c0), high)
            kv_cache[(j, h)] = (jnp.concatenate([k_lo, k_hi], axis=0).astype(jnp.bfloat16),
                                jnp.concatenate([v_lo, v_hi], axis=0).astype(jnp.bfloat16))
        return kv_cache[(j, h)]

    def where(item):
        j, p = divmod(item, n_pairs)
        return j, p, slice(j * w, (j + 1) * w)

    def score_stage(item, _):
        j, p, rows = where(item)
        qp = q_ref[rows, p * LANES:(p + 1) * LANES]
        kcat = kv_operands(j, p // pairs_per_kv)[0]
        return lax.dot_general(qp, kcat, _NT, preferred_element_type=jnp.float32)

    def max_stage(item, s):
        j, p, _ = where(item)
        merged, mx = [], []
        for hh in range(2):
            prev = s[:, hh * 2 * w:hh * 2 * w + w]
            if j == 0:
                prev = prev + first_bias
            cur = s[:, hh * 2 * w + w:(hh + 1) * 2 * w]
            sh = jnp.where(lower, cur, prev)
            merged.append(sh)
            mx.append(jnp.maximum(jnp.max(sh, axis=-1, keepdims=True), sink_ref[2 * p + hh] * LOG2_E))
        return merged, mx

    def exp_stage(item, carry):
        _, p, _ = where(item)
        merged, mx = carry
        probs, denom = [], []
        for hh in range(2):
            e = jnp.exp2(merged[hh] - mx[hh])
            denom.append(jnp.sum(e, axis=-1, keepdims=True)
                         + jnp.exp2(sink_ref[2 * p + hh] * LOG2_E - mx[hh]))
            e = e.astype(jnp.bfloat16)
            zero = jnp.zeros_like(e)
            probs += [jnp.where(lower, zero, e), jnp.where(lower, e, zero)]
        return jnp.concatenate(probs, axis=1), denom

    def value_stage(item, carry):
        j, p, _ = where(item)
        pcat, denom = carry
        vcat = kv_operands(j, p // pairs_per_kv)[1]
        return jnp.dot(pcat, vcat, preferred_element_type=jnp.float32), denom

    def out_stage(item, carry):
        _, p, rows = where(item)
        o, denom = carry
        o = o * jnp.where(lane < ATTN_HEAD_DIM, 1.0 / denom[0], 1.0 / denom[1])
        gi, gp = divmod(p, pairs_per_gate_ref)
        gate = gate_refs[gi][rows, gp * LANES:(gp + 1) * LANES].astype(jnp.float32)
        o_ref[rows, p * LANES:(p + 1) * LANES] = (o * gate).astype(o_ref.dtype)

    _skewed(blocks * n_pairs, [score_stage, max_stage, exp_stage, value_stage, out_stage])


def _swa(proj, sinks, batch, seq, d_attn, blocks=SWA_BLOCKS):
    t = proj.shape[0]
    w = WINDOW
    nb = seq // w
    blocks = min(blocks, nb)
    assert nb % blocks == 0
    nbs = nb // blocks
    n_q = d_attn // ATTN_HEAD_DIM
    n_kv = n_q // GQA_GROUP
    d_kv = n_kv * ATTN_HEAD_DIM
    kvw = 2 * d_kv
    assert d_attn == 4 * kvw
    kv_blk = d_attn // kvw
    gate_blk0 = (d_attn + kvw) // kvw

    def rows(b, n):
        return b * nbs + n

    def prev_block(b, n):
        return b * nb + jnp.maximum(n * blocks - 1, 0)

    in_specs = [
        pl.BlockSpec(memory_space=pltpu.SMEM),
        pl.BlockSpec((blocks * w, d_attn), lambda b, n: (rows(b, n), 0)),
        pl.BlockSpec((w, kvw), lambda b, n: (prev_block(b, n), kv_blk)),
        pl.BlockSpec((blocks * w, kvw), lambda b, n: (rows(b, n), kv_blk)),
    ] + [
        pl.BlockSpec((blocks * w, kvw),
                     functools.partial(lambda b, n, j: (rows(b, n), gate_blk0 + j), j=j))
        for j in range(4)
    ]
    return pl.pallas_call(
        functools.partial(_swa_kernel, n_kv_heads=n_kv, d_kv=d_kv, blocks=blocks),
        out_shape=jax.ShapeDtypeStruct((t, d_attn), jnp.bfloat16),
        grid=(batch, nbs),
        in_specs=in_specs,
        out_specs=pl.BlockSpec((blocks * w, d_attn), lambda b, n: (rows(b, n), 0)),
        compiler_params=pltpu.CompilerParams(
            dimension_semantics=("parallel", "parallel"), vmem_limit_bytes=VMEM_LIMIT),
        name="swa",
    )(sinks, proj, proj, proj, proj, proj, proj, proj)


def _decay_tables():
    c = CHUNK
    n = np.zeros((N_LEVELS + 1, c, c), np.float32)
    r = np.arange(c)
    for l in range(1, N_LEVELS):
        half = 1 << l
        m = 2 * half
        for t in range(c):
            start = (t // m) * m
            mid = start + half - 1
            if t > mid:
                n[l - 1, t] = (r > mid) & (r <= t)
            else:
                n[l - 1, t] = (r > t) & (r <= mid)
    n[N_LEVELS - 1] = r[None, :] <= r[:, None]
    n[N_LEVELS] = r[None, :] > r[:, None]
    cumsum = np.concatenate([n[N_LEVELS - 1], n[N_LEVELS - 1]], axis=1)
    n = n.reshape((N_LEVELS + 1) * c, c)
    ncat = np.concatenate([n, n], axis=1)
    tt, ss = np.meshgrid(r, r, indexing="ij")
    x = tt ^ ss
    level = np.full((c, c), -1, np.int32)
    for l in range(N_LEVELS):
        level[(ss < tt) & (x >= (1 << l)) & (x < (2 << l))] = l
    return ncat, cumsum, level


def _interleave_rows(k, q, half):
    pieces = []
    for r0 in range(0, k.shape[0], half):
        src = q if (r0 // half) % 2 else k
        pieces.append(src[r0:r0 + half])
    return jnp.concatenate(pieces, axis=0)


def _hgrn2_kernel(ncat_ref, cum_ref, level_ref, gain_ref, worst_ref, q_ref, ghi_ref, glo_ref, v_ref,
                  gate_ref, o_ref, state_ref, *, heads, chunks):
    c = CHUNK
    pw = 2 * LANES
    n_pairs = heads // 2

    @pl.when(pl.program_id(2) == 0)
    def _():
        state_ref[...] = jnp.zeros_like(state_ref)

    mild = jnp.min(worst_ref[...]) > -SAFE_LOG2

    def key_gate(rows, cols):
        g2 = ghi_ref[rows, cols].astype(jnp.float32) + glo_ref[rows, cols].astype(jnp.float32)
        return 1.0 - jnp.exp2(g2)

    def head_matmuls(a_bf, q_op, k_op, v_bf, st):
        intra = jnp.dot(a_bf, v_bf, preferred_element_type=jnp.float32)
        inter = lax.dot_general(q_op, st.astype(jnp.bfloat16), _NT,
                                preferred_element_type=jnp.float32)
        upd = lax.dot_general(v_bf, k_op, _TN, preferred_element_type=jnp.float32)
        return inter + intra, upd

    def head_tail(a_bf, qi, kl, egl, v_bf, idx):
        st = state_ref[idx]
        o, upd = head_matmuls(a_bf, qi, kl, v_bf, st)
        state_ref[idx] = egl * st + upd
        return o

    def write_out(outs, rows, cols):
        normed = []
        for o in outs:
            ms = jnp.mean(o * o, axis=-1, keepdims=True)
            normed.append(o * lax.rsqrt(ms + NORM_EPS))
        o2 = jnp.concatenate(normed, axis=1) * gain_ref[:, cols]
        gate = gate_ref[rows, cols].astype(jnp.float32)
        o_ref[rows, cols] = (o2 * gate).astype(o_ref.dtype)

    @pl.when(mild)
    def _():
        ti = lax.broadcasted_iota(jnp.int32, (c, c), 0)
        si = lax.broadcasted_iota(jnp.int32, (c, c), 1)
        causal = si <= ti
        halves = [slice(hh * LANES, (hh + 1) * LANES) for hh in range(2)]

        def where(item):
            ci, pr = divmod(item, n_pairs)
            return slice(ci * c, (ci + 1) * c), slice(pr * pw, (pr + 1) * pw), pr

        def cumsum_stage(item, _):
            rows, cols, _ = where(item)
            gcat = jnp.concatenate([ghi_ref[rows, cols], glo_ref[rows, cols]], axis=0)
            return jnp.dot(cum_ref[...], gcat, preferred_element_type=jnp.float32)

        def scale_stage(item, gsum):
            rows, cols, _ = where(item)
            ref = gsum[c // 2 - 1:c // 2, :]
            last = gsum[c - 1:c, :]
            decay = jnp.exp2(gsum - ref)
            qp = (q_ref[rows, cols].astype(jnp.float32) * decay).astype(jnp.bfloat16)
            kp = (key_gate(rows, cols) * (1.0 / decay)).astype(jnp.bfloat16)
            a = [lax.dot_general(qp[:, hs], kp[:, hs], _NT, preferred_element_type=jnp.float32)
                 for hs in halves]
            return a, qp, kp, jnp.exp2(ref), jnp.exp2(last - ref)

        def mix_stage(item, carry):
            a, qp, kp, e_ref, e_rest = carry
            rows, cols, pr = where(item)
            v_bf = v_ref[rows, cols]
            heads_out = []
            for hh, hs in enumerate(halves):
                st = state_ref[2 * pr + hh] * e_ref[:, hs]
                o, upd = head_matmuls(jnp.where(causal, a[hh], 0.0).astype(jnp.bfloat16),
                                      qp[:, hs], kp[:, hs], v_bf[:, hs], st)
                heads_out.append((o, st, upd))
            return heads_out, e_rest

        def out_stage(item, carry):
            heads_out, e_rest = carry
            rows, cols, pr = where(item)
            for hh, hs in enumerate(halves):
                _, st, upd = heads_out[hh]
                state_ref[2 * pr + hh] = (st + upd) * e_rest[:, hs]
            write_out([o for o, _, _ in heads_out], rows, cols)

        gap = min(SKEW_GAP, n_pairs)
        _skewed(chunks * n_pairs, [cumsum_stage, scale_stage, mix_stage, out_stage],
                gap=gap, oldest_first=n_pairs <= gap)

    @pl.when(jnp.logical_not(mild))
    def _():
        level = level_ref[...]
        trow = lax.broadcasted_iota(jnp.int32, (c, pw), 0)

        def chunk_body(ci, carry):
            rows = pl.ds(pl.multiple_of(ci * c, c), c)
            for pr in range(n_pairs):
                cols = slice(pr * pw, (pr + 1) * pw)
                q = q_ref[rows, cols].astype(jnp.float32)
                g_hi = ghi_ref[rows, cols]
                g_lo = glo_ref[rows, cols]
                f = jnp.exp2(g_hi.astype(jnp.float32) + g_lo.astype(jnp.float32))
                k = 1.0 - f
                gcat = jnp.concatenate([g_hi, g_lo], axis=0)
                ex = jnp.dot(ncat_ref[...], gcat, preferred_element_type=jnp.float32)

                z = [jnp.where((trow & 1) == 1, q * f, k).astype(jnp.bfloat16)]
                for l in range(1, N_LEVELS):
                    half = 1 << l
                    if half < SUBLANES:
                        qk = jnp.where((trow & half) != 0, q, k)
                    else:
                        qk = _interleave_rows(k, q, half)
                    z.append((qk * jnp.exp2(ex[(l - 1) * c:l * c])).astype(jnp.bfloat16))
                eg = jnp.exp2(ex[(N_LEVELS - 1) * c:N_LEVELS * c])
                erest = jnp.exp2(ex[N_LEVELS * c:(N_LEVELS + 1) * c])
                qi = (q * eg).astype(jnp.bfloat16)
                kl = (k * erest).astype(jnp.bfloat16)
                qk_diag = q * k
                v_bf = v_ref[rows, cols]
                outs = []
                for hh in range(2):
                    hs = slice(hh * LANES, (hh + 1) * LANES)
                    attn = jnp.zeros((c, c), jnp.float32)
                    for l in range(N_LEVELS):
                        zl = z[l][:, hs]
                        a = lax.dot_general(zl, zl, _NT, preferred_element_type=jnp.float32)
                        attn = jnp.where(level == l, a, attn)
                    diag = jnp.sum(qk_diag[:, hs], axis=-1, keepdims=True)
                    o = head_tail(attn.astype(jnp.bfloat16), qi[:, hs], kl[:, hs],
                                  eg[c - 1:c, hs], v_bf[:, hs], 2 * pr + hh)
                    outs.append(o + diag * v_bf[:, hs].astype(jnp.float32))
                write_out(outs, rows, cols)
            return carry

        lax.fori_loop(0, chunks, chunk_body, 0)


def _hgrn2(proj, glo, worst, gain, batch, seq, d_attn, d_rnn, rows, wb=PROJ_TN):
    t = proj.shape[0]
    assert seq % rows == 0 and rows % CHUNK == 0 and wb % (2 * RNN_HEAD_DIM) == 0
    chunks = rows // CHUNK
    heads = wb // RNN_HEAD_DIM
    nl = seq // rows
    n_hg = d_rnn // wb
    kvw = d_attn // 4
    base = d_attn + kvw + d_attn
    assert base % wb == 0
    ncat, cumsum, level = _decay_tables()

    def seg(i):
        blk0 = (base + i * d_rnn) // wb
        return pl.BlockSpec((rows, wb), lambda b, hg, l: (b * nl + l, blk0 + hg))

    const = lambda shape: pl.BlockSpec(shape, lambda b, hg, l: (0, 0))
    own = pl.BlockSpec((rows, wb), lambda b, hg, l: (b * nl + l, hg))
    return pl.pallas_call(
        functools.partial(_hgrn2_kernel, heads=heads, chunks=chunks),
        out_shape=jax.ShapeDtypeStruct((t, d_rnn), jnp.bfloat16),
        grid=(batch, n_hg, nl),
        in_specs=[
            const(ncat.shape), const(cumsum.shape), const(level.shape),
            pl.BlockSpec((1, wb), lambda b, hg, l: (0, hg)),
            pl.BlockSpec((SUBLANES, LANES), lambda b, hg, l: (b * nl + l, hg)),
            seg(0), seg(1), own, seg(2), seg(3),
        ],
        out_specs=own,
        scratch_shapes=[pltpu.VMEM((heads, RNN_HEAD_DIM, RNN_HEAD_DIM), jnp.float32)],
        compiler_params=pltpu.CompilerParams(
            dimension_semantics=("parallel", "parallel", "arbitrary"),
            vmem_limit_bytes=VMEM_LIMIT),
        name="hgrn2",
    )(jnp.asarray(ncat, jnp.bfloat16), jnp.asarray(cumsum, jnp.bfloat16), jnp.asarray(level),
      gain.reshape(1, d_rnn), worst, proj, proj, glo, proj, proj)


def _out_proj_kernel(a1_ref, a2_ref, w_ref, x_ref, g_ref, o_ref, acc_ref, y_ref, ssq_ref,
                     *, nk1, n_tiles):
    r = pl.program_id(0)
    kk = pl.program_id(1)
    nk = pl.num_programs(1)
    d = acc_ref.shape[1]
    te = o_ref.shape[0]

    def residual_chunk():
        rows = pl.ds(pl.multiple_of(kk * te, te), te)
        inv = lax.rsqrt(ssq_ref[rows, :] * (1.0 / d) + NORM_EPS)
        o_ref[...] = x_ref[...] + y_ref[rows, :].astype(jnp.float32) * inv * g_ref[...]

    def accumulate(last):
        a = jnp.where(kk < nk1, a1_ref[...], a2_ref[...])
        ssq = None
        for n0 in range(0, d, OUT_COL_CHUNK):
            cols = slice(n0, n0 + OUT_COL_CHUNK)
            part = jnp.dot(a, w_ref[:, cols], preferred_element_type=jnp.float32)
            y = jnp.where(kk == 0, part, acc_ref[:, cols] + part)
            if last:
                y_ref[:, cols] = y.astype(y_ref.dtype)
                s = jnp.sum(y * y, axis=-1, keepdims=True)
                ssq = s if ssq is None else ssq + s
            else:
                acc_ref[:, cols] = y
        if last:
            ssq_ref[...] = ssq

    first, drain = r == 0, r == n_tiles
    mid = jnp.logical_not(first | drain)
    last_slab = kk == nk - 1

    @pl.when(first & (kk == 0))
    def _():
        acc_ref[...] = jnp.zeros_like(acc_ref)

    @pl.when(first & jnp.logical_not(last_slab))
    def _():
        accumulate(False)

    @pl.when(first & last_slab)
    def _():
        accumulate(True)

    @pl.when(mid & jnp.logical_not(last_slab))
    def _():
        residual_chunk()
        accumulate(False)

    @pl.when(mid & last_slab)
    def _():
        residual_chunk()
        accumulate(True)

    @pl.when(drain)
    def _():
        residual_chunk()


def _out_proj(a1, a2, w, x2d, gain, tm=OUT_TM, tk=OUT_TK):
    t, k1 = a1.shape
    k2 = a2.shape[1]
    d = w.shape[1]
    tm = min(tm, t)
    n_tiles = t // tm
    nk1, nk2 = k1 // tk, k2 // tk
    nk = nk1 + nk2
    te = tm // nk
    assert tm % nk == 0 and te % SUBLANES == 0

    def a_row(r):
        return jnp.minimum(r, n_tiles - 1)

    def residual_block(r, k):
        return (jnp.where(r == 0, 0, (r - 1) * nk + k), 0)

    return pl.pallas_call(
        functools.partial(_out_proj_kernel, nk1=nk1, n_tiles=n_tiles),
        out_shape=jax.ShapeDtypeStruct((t, d), jnp.float32),
        grid=(n_tiles + 1, nk),
        in_specs=[pl.BlockSpec((tm, tk), lambda r, k: (a_row(r), jnp.minimum(k, nk1 - 1))),
                  pl.BlockSpec((tm, tk), lambda r, k: (a_row(r), jnp.clip(k - nk1, 0, nk2 - 1))),
                  pl.BlockSpec((tk, d), lambda r, k: (jnp.where(r == n_tiles, nk - 1, k), 0)),
                  pl.BlockSpec((te, d), residual_block),
                  pl.BlockSpec((1, d), lambda r, k: (0, 0))],
        out_specs=pl.BlockSpec((te, d), residual_block),
        scratch_shapes=[pltpu.VMEM((tm, d), jnp.float32),
                        pltpu.VMEM((tm, d), jnp.bfloat16),
                        pltpu.VMEM((tm, 1), jnp.float32)],
        compiler_params=pltpu.CompilerParams(
            dimension_semantics=("arbitrary", "arbitrary"), vmem_limit_bytes=VMEM_LIMIT),
        name="out_proj",
    )(a1, a2, w, x2d, gain.reshape(1, d))


def kernel(x, w_in, attn_sinks, lb_logits, rnn_norm, w_out, pre_norm, post_norm):
    batch, seq, d_model = x.shape
    depth = w_in.shape[0]
    d_mix = w_out.shape[1]
    d_attn = d_mix // 2
    d_rnn = d_mix - d_attn
    x2d = x.reshape(batch * seq, d_model)
    tm = min(PROJ_TM, seq)
    for layer in range(depth):
        h = _prenorm(x2d, pre_norm[layer])
        proj, glo, worst, w_out_bf = _in_proj(h, w_in[layer], w_out[layer], lb_logits,
                                              d_attn, d_rnn, layer, tm)
        attn = _swa(proj, attn_sinks[layer], batch, seq, d_attn)
        rnn = _hgrn2(proj, glo, worst, rnn_norm[layer], batch, seq, d_attn, d_rnn, rows=tm)
        x2d = _out_proj(attn, rnn, w_out_bf, x2d, post_norm[layer])
    return x2d.reshape(batch, seq, d_model)
```
